```python
import jax, jax.numpy as jnp
from jax import lax
import numpy as np

D_MODEL = 1024
BATCH = 2
SEQ = 8192
DEPTH = 2

CTX_LEN = 256
GRID_W = 64
N_POOL_GROUPS = 4
POOL_GROUP = 128
POOL_WIDTH = N_POOL_GROUPS * POOL_GROUP
POOL_WINDOWS = (2, 4, 8, 16)
GLA_HEADS = 4
GLA_DK = 64
GLA_DV = 128
GLA_KW = GLA_HEADS * GLA_DK
GLA_VW = GLA_HEADS * GLA_DV
GATE_RANK = 16
GATE_TAU = 16.0
CHUNK = 64
D_FF = 2816
N_MOD = 9
EPS = 1e-6

OFF_K = 0
OFF_V = OFF_K + GLA_KW
OFF_AF = OFF_V + GLA_VW
OFF_AB = OFF_AF + GATE_RANK
OFF_Q = OFF_AB + GATE_RANK
OFF_R = OFF_Q + GLA_KW
OFF_POOL = OFF_R + GLA_VW
OFF_GP = OFF_POOL + POOL_WIDTH
OFF_GG = OFF_GP + D_MODEL
IN_COLS = OFF_GG + D_MODEL

kernel_name = "hybrid_pool_gla_macaron_dit"


def rmsnorm(x, w):
    xf = x.astype(jnp.float32)
    y = xf * lax.rsqrt(jnp.mean(xf * xf, axis=-1, keepdims=True) + EPS)
    return (y * w.astype(jnp.float32)).astype(x.dtype)


def modulate(x, w, shift, scale):
    return rmsnorm(x, w) * (1 + scale) + shift


def modulation(cond, w_mod, b_mod):
    m = jax.nn.silu(cond) @ w_mod + b_mod
    return jnp.split(m, N_MOD, axis=-1)


def swiglu(h, w_up, w_down):
    a, b = jnp.split(h @ w_up, 2, axis=-1)
    return (jax.nn.silu(a) * b) @ w_down


def box_mean(u, w, axis):
    n = u.shape[axis]
    cs = jnp.cumsum(u.astype(jnp.float32), axis=axis)
    pad = [(0, 0)] * u.ndim
    pad[axis] = (1, 0)
    cs = jnp.pad(cs, pad)
    t = jnp.arange(n)
    lo = jnp.clip(t - w // 2, 0, n)
    hi = jnp.clip(t + w // 2, 0, n)
    s = jnp.take(cs, hi, axis=axis) - jnp.take(cs, lo, axis=axis)
    shape = [1] * u.ndim
    shape[axis] = n
    cnt = (hi - lo).astype(jnp.float32).reshape(shape)
    return (s / cnt).astype(u.dtype)


def pool_mix(u, grid, pool_w, pool_scale):
    B, L, _ = u.shape
    outs = []
    for gi, w in enumerate(POOL_WINDOWS):
        ug = u[..., gi * POOL_GROUP:(gi + 1) * POOL_GROUP]
        if grid:
            rows = L // GRID_W
            ugr = ug.reshape(B, rows, GRID_W, POOL_GROUP)
            m = box_mean(box_mean(ugr, w, 1), w, 2).reshape(B, L, POOL_GROUP)
        else:
            m = box_mean(ug, w, 1)
        outs.append(m - ug)
    p = jnp.stack(outs, axis=2)
    y = jnp.einsum('blgp,gpq->blgq', p, pool_w).reshape(B, L, POOL_WIDTH)
    return y * pool_scale


def to_heads(t, d):
    B, L, _ = t.shape
    return t.reshape(B, L, GLA_HEADS, d).transpose(0, 2, 1, 3)


def log_forget(a_low, w_up, b):
    return jax.nn.log_sigmoid((a_low @ w_up + b).astype(jnp.float32)) / GATE_TAU


def gla_kvg(proj, lp):
    k = to_heads(proj[..., OFF_K:OFF_V], GLA_DK)
    v = to_heads(proj[..., OFF_V:OFF_AF], GLA_DV)
    g_f = to_heads(log_forget(proj[..., OFF_AF:OFF_AB], lp['w_af_up'], lp['b_af']), GLA_DK)
    g_b = to_heads(log_forget(proj[..., OFF_AB:OFF_Q], lp['w_ab_up'], lp['b_ab']), GLA_DK)
    return k, v, g_f, g_b


def flip_seq(t):
    return jnp.flip(t, axis=2)


def gla_chunked(q, k, v, g, s0, inclusive):
    f32 = jnp.float32
    B, H, L, dk = q.shape
    dv = v.shape[-1]
    n = L // CHUNK
    qc = q.astype(f32).reshape(B, H, n, CHUNK, dk)
    kc = k.astype(f32).reshape(B, H, n, CHUNK, dk)
    vc = v.astype(f32).reshape(B, H, n, CHUNK, dv)
    b = jnp.cumsum(g.astype(f32).reshape(B, H, n, CHUNK, dk), axis=3)
    b_last = b[:, :, :, -1:, :]
    q_t = qc * jnp.exp(b)
    k_t = kc * jnp.exp(-b)
    k_end = kc * jnp.exp(b_last - b)
    mask = jnp.tril(jnp.ones((CHUNK, CHUNK), bool), 0 if inclusive else -1)
    att = jnp.where(mask, jnp.einsum('bhncd,bhnmd->bhncm', q_t, k_t), 0.0)
    o = jnp.einsum('bhncm,bhnme->bhnce', att, vc)
    kv = jnp.einsum('bhncd,bhnce->bhnde', k_end, vc)
    decay = jnp.exp(b_last[:, :, :, 0, :])

    def step(s, inp):
        kv_n, dec_n = inp
        return s * dec_n[..., None] + kv_n, s

    _, s_prev = lax.scan(step, s0.astype(f32), (jnp.moveaxis(kv, 2, 0), jnp.moveaxis(decay, 2, 0)))
    s_prev = jnp.moveaxis(s_prev, 0, 2)
    o = o + jnp.einsum('bhncd,bhnde->bhnce', q_t, s_prev)
    return o.reshape(B, H, L, dv).astype(v.dtype)


def gla_final_state(k, v, g):
    f32 = jnp.float32
    b = jnp.cumsum(g.astype(f32), axis=2)
    dec = jnp.exp(b[:, :, -1:, :] - b)
    return jnp.einsum('bhld,bhle->bhde', k.astype(f32) * dec, v.astype(f32))


def context_states(proj, lp):
    k, v, g_f, g_b = gla_kvg(proj, lp)
    s_f = gla_final_state(k, v, g_f)
    s_b = gla_final_state(flip_seq(k), flip_seq(v), flip_seq(g_b))
    return s_f, s_b


def gla_branch(proj, s_f0, s_b0, lp):
    k, v, g_f, g_b = gla_kvg(proj, lp)
    q = to_heads(proj[..., OFF_Q:OFF_R], GLA_DK) * (GLA_DK ** -0.5)
    o_f = gla_chunked(q, k, v, g_f, s_f0, True)
    o_b = flip_seq(gla_chunked(flip_seq(q), flip_seq(k), flip_seq(v), flip_seq(g_b), s_b0, False))
    o = rmsnorm(o_f + o_b, lp['gla_norm_w'])
    B, H, L, _ = o.shape
    o = o.transpose(0, 2, 1, 3).reshape(B, L, GLA_VW)
    return o * jax.nn.silu(proj[..., OFF_R:OFF_POOL])


def token_mixer(h, grid, s_f0, s_b0, lp):
    proj = h @ lp['w_in']
    y_gla = gla_branch(proj, s_f0, s_b0, lp) @ lp['w_br_gla']
    y_pool = pool_mix(proj[..., OFF_POOL:OFF_GP], grid, lp['pool_w'], lp['pool_scale']) @ lp['w_br_pool']
    merged = jax.nn.sigmoid(proj[..., OFF_GP:OFF_GG]) * y_pool + jax.nn.sigmoid(proj[..., OFF_GG:IN_COLS]) * y_gla
    return merged @ lp['w_out'], proj


def setup_inputs(seed: int = 0) -> dict:
    key = jax.random.key(seed)
    ks = jax.random.split(key, 24)
    f32 = jnp.float32
    L = DEPTH

    def nrm(k, shape, fan):
        return jax.random.normal(k, shape, f32) * (fan ** -0.5)

    def near_one(k, shape):
        return 1.0 + 0.02 * jax.random.normal(k, shape, f32)

    def small(k, shape, s):
        return s * jax.random.normal(k, shape, f32)

    return {
        "x": jax.random.normal(ks[0], (BATCH, SEQ, D_MODEL), f32),
        "c": jax.random.normal(ks[1], (BATCH, D_MODEL), f32),
        "ctx": jax.random.normal(ks[2], (BATCH, CTX_LEN, D_MODEL), f32),
        "c_ctx": jax.random.normal(ks[3], (D_MODEL,), f32),
        "w_mod": nrm(ks[4], (L, D_MODEL, N_MOD * D_MODEL), D_MODEL),
        "b_mod": small(ks[5], (L, N_MOD * D_MODEL), 0.02),
        "norm_w": near_one(ks[6], (L, 3, D_MODEL)),
        "ffn1_up": nrm(ks[7], (L, D_MODEL, 2 * D_FF), D_MODEL),
        "ffn1_down": nrm(ks[8], (L, D_FF, D_MODEL), D_FF),
        "w_in": nrm(ks[9], (L, D_MODEL, IN_COLS), D_MODEL),
        "w_af_up": nrm(ks[10], (L, GATE_RANK, GLA_KW), GATE_RANK),
        "b_af": small(ks[11], (L, GLA_KW), 0.1),
        "w_ab_up": nrm(ks[12], (L, GATE_RANK, GLA_KW), GATE_RANK),
        "b_ab": small(ks[13], (L, GLA_KW), 0.1),
        "gla_norm_w": near_one(ks[14], (L, GLA_DV)),
        "pool_w": nrm(ks[15], (L, N_POOL_GROUPS, POOL_GROUP, POOL_GROUP), POOL_GROUP),
        "pool_scale": near_one(ks[16], (L, POOL_WIDTH)),
        "w_br_pool": nrm(ks[17], (L, POOL_WIDTH, D_MODEL), POOL_WIDTH),
        "w_br_gla": nrm(ks[18], (L, GLA_VW, D_MODEL), GLA_VW),
        "w_out": nrm(ks[19], (L, D_MODEL, D_MODEL), D_MODEL),
        "ffn2_up": nrm(ks[20], (L, D_MODEL, 2 * D_FF), D_MODEL),
        "ffn2_down": nrm(ks[21], (L, D_FF, D_MODEL), D_FF),
        "final_norm_w": near_one(ks[22], (D_MODEL,)),
    }


def reference(x, c, ctx, c_ctx, w_mod, b_mod, norm_w, ffn1_up, ffn1_down, w_in, w_af_up, b_af,
              w_ab_up, b_ab, gla_norm_w, pool_w, pool_scale, w_br_pool, w_br_gla, w_out,
              ffn2_up, ffn2_down, final_norm_w):
    B = x.shape[0]
    zero_state = jnp.zeros((B, GLA_HEADS, GLA_DK, GLA_DV), jnp.float32)
    for l in range(DEPTH):
        last = l == DEPTH - 1
        lp = {
            'w_in': w_in[l], 'w_af_up': w_af_up[l], 'b_af': b_af[l], 'w_ab_up': w_ab_up[l],
            'b_ab': b_ab[l], 'gla_norm_w': gla_norm_w[l], 'pool_w': pool_w[l],
            'pool_scale': pool_scale[l], 'w_br_pool': w_br_pool[l], 'w_br_gla': w_br_gla[l],
            'w_out': w_out[l],
        }
        nw = norm_w[l]
        sh1, sc1, g1, sh2, sc2, g2, sh3, sc3, g3 = modulation(c[:, None, :], w_mod[l], b_mod[l])
        ch1, cs1, cg1, ch2, cs2, cg2, ch3, cs3, cg3 = modulation(c_ctx[None, None, :], w_mod[l], b_mod[l])

        x = x + 0.5 * g1 * swiglu(modulate(x, nw[0], sh1, sc1), ffn1_up[l], ffn1_down[l])
        ctx = ctx + 0.5 * cg1 * swiglu(modulate(ctx, nw[0], ch1, cs1), ffn1_up[l], ffn1_down[l])

        hc = modulate(ctx, nw[1], ch2, cs2)
        if last:
            s_f, s_b = context_states(hc @ w_in[l][:, :OFF_Q], lp)
        else:
            yc, projc = token_mixer(hc, False, zero_state, zero_state, lp)
            s_f, s_b = context_states(projc, lp)
            ctx = ctx + cg2 * yc
            ctx = ctx + 0.5 * cg3 * swiglu(modulate(ctx, nw[2], ch3, cs3), ffn2_up[l], ffn2_down[l])

        yx, _ = token_mixer(modulate(x, nw[1], sh2, sc2), True, s_f, s_b, lp)
        x = x + g2 * yx

        x = x + 0.5 * g3 * swiglu(modulate(x, nw[2], sh3, sc3), ffn2_up[l], ffn2_down[l])
    return rmsnorm(x, final_norm_w)
```

```python
import functools

import jax
import jax.numpy as jnp
from jax import lax
from jax.experimental import pallas as pl
from jax.experimental.pallas import tpu as pltpu

F32 = jnp.float32
BF16 = jnp.bfloat16

D_MODEL = 1024
DEPTH = 2
GRID_W = 64
N_POOL_GROUPS = 4
POOL_GROUP = 128
POOL_WIDTH = N_POOL_GROUPS * POOL_GROUP
POOL_WINDOWS = (2, 4, 8, 16)
GLA_HEADS = 4
GLA_DK = 64
GLA_DV = 128
GLA_KW = GLA_HEADS * GLA_DK
GLA_VW = GLA_HEADS * GLA_DV
GATE_RANK = 16
GATE_TAU = 16.0
CHUNK = 64
D_FF = 2816
N_MOD = 9
EPS = 1e-6

OFF_K = 0
OFF_V = OFF_K + GLA_KW
OFF_AF = OFF_V + GLA_VW
OFF_AB = OFF_AF + GATE_RANK
OFF_Q = OFF_AB + GATE_RANK
OFF_R = OFF_Q + GLA_KW
OFF_POOL = OFF_R + GLA_VW
OFF_GP = OFF_POOL + POOL_WIDTH
OFF_GG = OFF_GP + D_MODEL
IN_COLS = OFF_GG + D_MODEL

V7X_VMEM_BYTES = 64 * 1024 * 1024
VMEM_LIMIT_BYTES = 56 * 1024 * 1024
SUBLANES = 8
TOKEN_TILE = 512
FF_CHUNK = 256
MOD_COL_TILE = 1152


def _resident(shape):
    return pl.BlockSpec(shape, lambda *_: (0,) * len(shape), pipeline_mode=pl.Buffered(1))


def _params(*semantics):
    return pltpu.CompilerParams(dimension_semantics=semantics, vmem_limit_bytes=VMEM_LIMIT_BYTES)


def _per_batch_or_shared(arr):
    zeros = (0,) * (arr.ndim - 1)
    if arr.shape[0] == 1:
        return lambda b, i: (0,) + zeros
    return lambda b, i: (b,) + zeros


def _token_tile(t):
    return min(TOKEN_TILE, t)


def _rms(x):
    return x * lax.rsqrt(jnp.mean(x * x, axis=-1, keepdims=True) + EPS)


def _modulated_norm(x, w, shift, scale):
    return (_rms(x) * w) * (1.0 + scale) + shift


def _silu(a):
    return a / (1.0 + jnp.exp(-a))


def _sigmoid(a):
    return 1.0 / (1.0 + jnp.exp(-a))


def _dot(a, b):
    return jnp.dot(a, b, preferred_element_type=F32)


def _dot_nt(a, b):
    return lax.dot_general(a, b, (((1,), (1,)), ((), ())), preferred_element_type=F32)


def _dot_tn(a, b):
    return lax.dot_general(a, b, (((0,), (0,)), ((), ())), preferred_element_type=F32)


def _mod_kernel(cond_ref, w_ref, b_ref, o_ref):
    cond = cond_ref[...]
    o_ref[0] = _dot(_silu(cond).astype(BF16), w_ref[0].astype(BF16)) + b_ref[0]


def _modulation(cond, w_mod, b_mod):
    n_cols = N_MOD * D_MODEL
    rows = cond.shape[0]
    return pl.pallas_call(
        _mod_kernel,
        grid=(DEPTH, n_cols // MOD_COL_TILE),
        in_specs=[
            pl.BlockSpec((rows, D_MODEL), lambda l, j: (0, 0)),
            pl.BlockSpec((1, D_MODEL, MOD_COL_TILE), lambda l, j: (l, 0, j)),
            pl.BlockSpec((1, 1, MOD_COL_TILE), lambda l, j: (l, 0, j)),
        ],
        out_specs=pl.BlockSpec((1, rows, MOD_COL_TILE), lambda l, j: (l, 0, j)),
        out_shape=jax.ShapeDtypeStruct((DEPTH, rows, n_cols), F32),
        compiler_params=_params("arbitrary", "arbitrary"),
        name="modulation",
    )(cond, w_mod, b_mod.reshape(DEPTH, 1, n_cols))


def _ffn_kernel(x_ref, mod_ref, nw_ref, up_ref, down_ref, fw_ref, o_ref, *, mod_row, final_norm):
    x = x_ref[0]
    mod = mod_ref[0]
    shift = mod[mod_row:mod_row + 1]
    scale = mod[mod_row + 1:mod_row + 2]
    gate = mod[mod_row + 2:mod_row + 3]
    h = _modulated_norm(x, nw_ref[...], shift, scale).astype(BF16)
    acc = jnp.zeros(x.shape, F32)
    for j in range(D_FF // FF_CHUNK):
        lo = j * FF_CHUNK
        a = _dot(h, up_ref[:, lo:lo + FF_CHUNK])
        b = _dot(h, up_ref[:, D_FF + lo:D_FF + lo + FF_CHUNK])
        act = (_silu(a) * b).astype(BF16)
        acc = acc + _dot(act, down_ref[lo:lo + FF_CHUNK, :])
    y = x + (0.5 * gate) * acc
    if final_norm:
        y = _rms(y) * fw_ref[...]
    o_ref[0] = y


def _ffn(x, mod, nw, up, down, fw, *, mod_row, final_norm):
    b, t, d = x.shape
    tm = _token_tile(t)
    kern = functools.partial(_ffn_kernel, mod_row=mod_row, final_norm=final_norm)
    return pl.pallas_call(
        kern,
        grid=(b, t // tm),
        in_specs=[
            pl.BlockSpec((1, tm, d), lambda b, i: (b, i, 0)),
            pl.BlockSpec((1, N_MOD, d), _per_batch_or_shared(mod)),
            _resident((1, d)),
            _resident((d, 2 * D_FF)),
            _resident((D_FF, d)),
            _resident((1, d)),
        ],
        out_specs=pl.BlockSpec((1, tm, d), lambda b, i: (b, i, 0)),
        out_shape=jax.ShapeDtypeStruct(x.shape, F32),
        compiler_params=_params("parallel", "parallel"),
        name="ffn",
    )(x, mod, nw, up, down, fw)


def _proj_kernel(x_ref, mod_ref, nw_ref, wk_ref, wv_ref, wa_ref, wq_ref, wr_ref, wu_ref, wgp_ref, wgg_ref,
                 wgate_ref, bgate_ref,
                 q_ref, k_ref, gf_ref, gb_ref, v_ref, r_ref, u_ref, gp_ref, gg_ref):
    x = x_ref[0]
    mod = mod_ref[0]
    h = _modulated_norm(x, nw_ref[...], mod[3:4], mod[4:5]).astype(BF16)
    k_ref[0] = _dot(h, wk_ref[...])
    v_ref[0] = _dot(h, wv_ref[...]).astype(BF16)
    q_ref[0] = _dot(h, wq_ref[...]) * (GLA_DK ** -0.5)
    r_ref[0] = _dot(h, wr_ref[...])
    u_ref[0] = _dot(h, wu_ref[...])
    gp_ref[0] = _dot(h, wgp_ref[...])
    gg_ref[0] = _dot(h, wgg_ref[...])
    a_low = _dot(h, wa_ref[...]).astype(BF16)
    z = _dot(a_low, wgate_ref[...]) + bgate_ref[...]
    log_sig = jnp.minimum(z, 0.0) - jnp.log1p(jnp.exp(-jnp.abs(z)))
    g = log_sig / GATE_TAU
    gf_ref[0] = g[:, :GLA_KW]
    gb_ref[0] = g[:, GLA_KW:]


def _proj(x, mod, nw, lw):
    b, t, d = x.shape
    tm = _token_tile(t)
    tok = lambda w: pl.BlockSpec((1, tm, w), lambda b, i: (b, i, 0))
    widths = dict(q=GLA_KW, k=GLA_KW, gf=GLA_KW, gb=GLA_KW, v=GLA_VW, r=GLA_VW, u=POOL_WIDTH, gp=d, gg=d)
    dtypes = dict(v=BF16)
    names = list(widths)
    weights = [lw['w_k'], lw['w_v'], lw['w_a'], lw['w_q'], lw['w_r'], lw['w_u'], lw['w_gp'], lw['w_gg'],
               lw['w_gate'], lw['b_gate']]
    outs = pl.pallas_call(
        _proj_kernel,
        grid=(b, t // tm),
        in_specs=[tok(d), pl.BlockSpec((1, N_MOD, d), _per_batch_or_shared(mod)), _resident((1, d))]
        + [_resident(w.shape) for w in weights],
        out_specs=[tok(widths[n]) for n in names],
        out_shape=[jax.ShapeDtypeStruct((b, t, widths[n]), dtypes.get(n, F32)) for n in names],
        compiler_params=_params("parallel", "parallel"),
        name="mixer_proj",
    )(x, mod, nw, *weights)
    return dict(zip(names, outs))


def _gla_kernel(*refs, reverse, finalize, n_chunks):
    if finalize:
        q_ref, k_ref, g_ref, v_ref, s0_ref, ob_ref, r_ref, nw_ref, o_ref, sfin_ref, st_ref = refs
    else:
        q_ref, k_ref, g_ref, v_ref, s0_ref, o_ref, sfin_ref, st_ref = refs
    step = pl.program_id(1)

    @pl.when(step == 0)
    def _():
        st_ref[...] = s0_ref[0]

    row = lax.broadcasted_iota(jnp.int32, (CHUNK, CHUNK), 0)
    col = lax.broadcasted_iota(jnp.int32, (CHUNK, CHUNK), 1)
    tri = jnp.where((col >= row) if reverse else (col <= row), 1.0, 0.0).astype(BF16)
    arow = lax.broadcasted_iota(jnp.int32, (CHUNK, GLA_KW), 0)
    acol = lax.broadcasted_iota(jnp.int32, (CHUNK, GLA_KW), 1) % CHUNK
    att_mask = (acol > arow) if reverse else (acol <= arow)
    k_head = lax.broadcasted_iota(jnp.int32, (CHUNK, GLA_KW), 1) // GLA_DK
    v_head = lax.broadcasted_iota(jnp.int32, (CHUNK, GLA_VW), 1) // GLA_DV
    st_mask = (lax.broadcasted_iota(jnp.int32, (GLA_VW, GLA_KW), 0) // GLA_DV
               == lax.broadcasted_iota(jnp.int32, (GLA_VW, GLA_KW), 1) // GLA_DK)
    edge = 0 if reverse else CHUNK - 1

    for cc in range(n_chunks):
        c = n_chunks - 1 - cc if reverse else cc
        rows = pl.ds(c * CHUNK, CHUNK)
        q = q_ref[0, rows, :]
        k = k_ref[0, rows, :]
        g = g_ref[0, rows, :]
        v = v_ref[0, rows, :]
        g_hi = g.astype(BF16)
        g_lo = (g - g_hi.astype(F32)).astype(BF16)
        bcum = _dot(tri, g_hi) + _dot(tri, g_lo)
        b_edge = bcum[edge:edge + 1]
        q_t = (q * jnp.exp(bcum)).astype(BF16)
        k_t = (k * jnp.exp(-bcum)).astype(BF16)
        k_end = (k * jnp.exp(b_edge - bcum)).astype(BF16)
        decay = jnp.exp(b_edge)
        zero = jnp.zeros((), BF16)
        k_blk = jnp.concatenate([jnp.where(k_head == h, k_t, zero) for h in range(GLA_HEADS)], axis=0)
        v_blk = jnp.concatenate([jnp.where(v_head == h, v, zero) for h in range(GLA_HEADS)], axis=0)
        att = jnp.where(att_mask, _dot_nt(q_t, k_blk), 0.0).astype(BF16)
        st = st_ref[...]
        o = _dot(att, v_blk) + _dot_nt(q_t, st.astype(BF16))
        st_ref[...] = st * decay + jnp.where(st_mask, _dot_tn(v, k_end), 0.0)
        if finalize:
            o = o + ob_ref[0, rows, :]
            r = r_ref[0, rows, :]
            nw = nw_ref[...]
            o = jnp.concatenate(
                [_rms(o[:, h * GLA_DV:(h + 1) * GLA_DV]) * nw for h in range(GLA_HEADS)], axis=1)
            o_ref[0, rows, :] = (o * _silu(r)).astype(o_ref.dtype)
        else:
            o_ref[0, rows, :] = o

    @pl.when(step == pl.num_programs(1) - 1)
    def _():
        sfin_ref[0] = st_ref[...]


def _gla(p, s0, *, reverse, finalize=None):
    b, t, _ = p['q'].shape
    tb = _token_tile(t)
    nb = t // tb
    blk = (lambda b, i: (b, nb - 1 - i, 0)) if reverse else (lambda b, i: (b, i, 0))
    tok = lambda w: pl.BlockSpec((1, tb, w), blk)
    state = pl.BlockSpec((1, GLA_VW, GLA_KW), lambda b, i: (b, 0, 0))
    g = p['gb'] if reverse else p['gf']
    args = [p['q'], p['k'], g, p['v'], s0]
    in_specs = [tok(GLA_KW), tok(GLA_KW), tok(GLA_KW), tok(GLA_VW), state]
    if finalize is not None:
        args += list(finalize)
        in_specs += [tok(GLA_VW), tok(GLA_VW), _resident((1, GLA_DV))]
    kern = functools.partial(_gla_kernel, reverse=reverse, finalize=finalize is not None, n_chunks=tb // CHUNK)
    return pl.pallas_call(
        kern,
        grid=(b, nb),
        in_specs=in_specs,
        out_specs=[tok(GLA_VW), state],
        out_shape=[jax.ShapeDtypeStruct((b, t, GLA_VW), F32 if finalize is None else BF16),
                   jax.ShapeDtypeStruct((b, GLA_VW, GLA_KW), F32)],
        scratch_shapes=[pltpu.VMEM((GLA_VW, GLA_KW), F32)],
        compiler_params=_params("arbitrary", "arbitrary"),
        name="gla_reverse" if reverse else "gla_forward",
    )(*args)


def _box_mean(u, idx, seg, unit, w):
    n = u.shape[0]
    half = w // 2

    def ahead(x, s):
        return jnp.where(idx + s < seg, pltpu.roll(x, (n - s * unit) % n, axis=0), 0.0)

    def behind(x, s):
        return jnp.where(idx - s >= 0, pltpu.roll(x, (s * unit) % n, axis=0), 0.0)

    fwd = u
    span = 1
    while span < half:
        fwd = fwd + ahead(fwd, span)
        span *= 2
    bwd = behind(u, 1)
    span = 1
    while span < half:
        bwd = bwd + behind(bwd, span)
        span *= 2
    cnt = (jnp.minimum(idx + half, seg) - jnp.maximum(idx - half, 0)).astype(F32)
    return (fwd + bwd) / cnt


def _pool_kernel(u_ref, w_ref, scale_ref, o_ref, *, n_rows, n_cols):
    group = pl.program_id(1)
    n = n_rows * n_cols
    tok = lax.broadcasted_iota(jnp.int32, (n, POOL_GROUP), 0)
    for gi, w in enumerate(POOL_WINDOWS):
        @pl.when(group == gi)
        def _(w=w):
            u = u_ref[0]
            m = u
            if n_rows > 1:
                m = _box_mean(m, tok // n_cols, n_rows, n_cols, w)
            m = _box_mean(m, tok % n_cols, n_cols, 1, w)
            y = _dot((m - u).astype(BF16), w_ref[0]) * scale_ref[0]
            o_ref[0] = y.astype(o_ref.dtype)


def _pool(u, pool_w, pool_scale, *, n_rows, n_cols):
    b, t, _ = u.shape
    kern = functools.partial(_pool_kernel, n_rows=n_rows, n_cols=n_cols)
    return pl.pallas_call(
        kern,
        grid=(b, N_POOL_GROUPS),
        in_specs=[
            pl.BlockSpec((1, t, POOL_GROUP), lambda b, g: (b, 0, g)),
            pl.BlockSpec((1, POOL_GROUP, POOL_GROUP), lambda b, g: (g, 0, 0)),
            pl.BlockSpec((1, 1, POOL_GROUP), lambda b, g: (g, 0, 0)),
        ],
        out_specs=pl.BlockSpec((1, t, POOL_GROUP), lambda b, g: (b, 0, g)),
        out_shape=jax.ShapeDtypeStruct((b, t, POOL_WIDTH), BF16),
        compiler_params=_params("parallel", "parallel"),
        name="pool_mix",
    )(u, pool_w, pool_scale)


def _merge_kernel(x_ref, mod_ref, og_ref, yp_ref, gp_ref, gg_ref, wbg_ref, wbp_ref, wo_ref, o_ref):
    y_gla = _dot(og_ref[0], wbg_ref[...])
    y_pool = _dot(yp_ref[0], wbp_ref[...])
    merged = _sigmoid(gp_ref[0]) * y_pool + _sigmoid(gg_ref[0]) * y_gla
    gate = mod_ref[0][5:6]
    o_ref[0] = x_ref[0] + gate * _dot(merged.astype(BF16), wo_ref[...])


def _merge(x, mod, og, yp, gp, gg, lw):
    b, t, d = x.shape
    tm = _token_tile(t)
    tok = lambda w: pl.BlockSpec((1, tm, w), lambda b, i: (b, i, 0))
    weights = [lw['w_br_gla'], lw['w_br_pool'], lw['w_out']]
    return pl.pallas_call(
        _merge_kernel,
        grid=(b, t // tm),
        in_specs=[tok(d), pl.BlockSpec((1, N_MOD, d), _per_batch_or_shared(mod)),
                  tok(GLA_VW), tok(POOL_WIDTH), tok(d), tok(d)] + [_resident(w.shape) for w in weights],
        out_specs=tok(d),
        out_shape=jax.ShapeDtypeStruct(x.shape, F32),
        compiler_params=_params("parallel", "parallel"),
        name="mixer_merge",
    )(x, mod, og, yp, gp, gg, *weights)


def _layer_weights(l, w_in, w_af_up, b_af, w_ab_up, b_ab, pool_w, pool_scale, w_br_pool, w_br_gla, w_out):
    wi = w_in[l].astype(BF16)
    zeros = jnp.zeros((GATE_RANK, GLA_KW), F32)
    w_gate = jnp.concatenate([jnp.concatenate([w_af_up[l], zeros], axis=1),
                              jnp.concatenate([zeros, w_ab_up[l]], axis=1)], axis=0)
    return {
        'w_k': wi[:, OFF_K:OFF_V], 'w_v': wi[:, OFF_V:OFF_AF], 'w_a': wi[:, OFF_AF:OFF_Q],
        'w_q': wi[:, OFF_Q:OFF_R], 'w_r': wi[:, OFF_R:OFF_POOL], 'w_u': wi[:, OFF_POOL:OFF_GP],
        'w_gp': wi[:, OFF_GP:OFF_GG], 'w_gg': wi[:, OFF_GG:IN_COLS],
        'w_gate': w_gate.astype(BF16),
        'b_gate': jnp.concatenate([b_af[l], b_ab[l]])[None, :],
        'pool_w': pool_w[l].astype(BF16),
        'pool_scale': pool_scale[l].reshape(N_POOL_GROUPS, 1, POOL_GROUP),
        'w_br_pool': w_br_pool[l].astype(BF16), 'w_br_gla': w_br_gla[l].astype(BF16),
        'w_out': w_out[l].astype(BF16),
    }


def _mixer(x, mod, nw, lw, s_f0, s_b0, gla_norm_w, *, n_rows, n_cols, states_only=False):
    p = _proj(x, mod, nw, lw)
    o_b, s_b = _gla(p, s_b0, reverse=True)
    og, s_f = _gla(p, s_f0, reverse=False, finalize=(o_b, p['r'], gla_norm_w))
    if states_only:
        return None, s_f, s_b
    yp = _pool(p['u'], lw['pool_w'], lw['pool_scale'], n_rows=n_rows, n_cols=n_cols)
    return _merge(x, mod, og, yp, p['gp'], p['gg'], lw), s_f, s_b


def kernel(x, c, ctx, c_ctx, w_mod, b_mod, norm_w, ffn1_up, ffn1_down, w_in, w_af_up, b_af, w_ab_up, b_ab,
           gla_norm_w, pool_w, pool_scale, w_br_pool, w_br_gla, w_out, ffn2_up, ffn2_down, final_norm_w):
    batch, seq, d = x.shape
    ctx_len = ctx.shape[1]
    cond = jnp.concatenate([c, c_ctx[None, :], jnp.zeros((SUBLANES - batch - 1, d), F32)], axis=0)
    mod_all = _modulation(cond, w_mod, b_mod)
    zero_state = jnp.zeros((batch, GLA_VW, GLA_KW), F32)
    fw = final_norm_w[None, :]
    for l in range(DEPTH):
        last = l == DEPTH - 1
        mod_x = mod_all[l, :batch].reshape(batch, N_MOD, d)
        mod_c = mod_all[l, batch:batch + 1].reshape(1, N_MOD, d)
        nw = norm_w[l][:, None, :]
        lw = _layer_weights(l, w_in, w_af_up, b_af, w_ab_up, b_ab, pool_w, pool_scale, w_br_pool, w_br_gla,
                            w_out)
        up1, down1 = ffn1_up[l].astype(BF16), ffn1_down[l].astype(BF16)
        up2, down2 = ffn2_up[l].astype(BF16), ffn2_down[l].astype(BF16)
        gnw = gla_norm_w[l][None, :]

        x = _ffn(x, mod_x, nw[0], up1, down1, fw, mod_row=0, final_norm=False)
        ctx = _ffn(ctx, mod_c, nw[0], up1, down1, fw, mod_row=0, final_norm=False)

        ctx_mixed, s_f, s_b = _mixer(ctx, mod_c, nw[1], lw, zero_state, zero_state, gnw,
                                     n_rows=1, n_cols=ctx_len, states_only=last)
        if not last:
            ctx = _ffn(ctx_mixed, mod_c, nw[2], up2, down2, fw, mod_row=6, final_norm=False)

        x, _, _ = _mixer(x, mod_x, nw[1], lw, s_f, s_b, gnw, n_rows=seq // GRID_W, n_cols=GRID_W)
        x = _ffn(x, mod_x, nw[2], up2, down2, fw, mod_row=6, final_norm=last)
    return x
```

```python
import functools

import jax
import jax.numpy as jnp
from jax import lax
from jax.experimental import pallas as pl
from jax.experimental.pallas import tpu as pltpu

F32 = jnp.float32
BF16 = jnp.bfloat16

D_MODEL = 1024
DEPTH = 2
GRID_W = 64
N_POOL_GROUPS = 4
POOL_GROUP = 128
POOL_WIDTH = N_POOL_GROUPS * POOL_GROUP
POOL_WINDOWS = (2, 4, 8, 16)
GLA_HEADS = 4
GLA_DK = 64
GLA_DV = 128
GLA_KW = GLA_HEADS * GLA_DK
GLA_VW = GLA_HEADS * GLA_DV
GATE_RANK = 16
GATE_TAU = 16.0
CHUNK = 64
D_FF = 2816
N_MOD = 9
EPS = 1e-6

OFF_K = 0
OFF_V = OFF_K + GLA_KW
OFF_AF = OFF_V + GLA_VW
OFF_AB = OFF_AF + GATE_RANK
OFF_Q = OFF_AB + GATE_RANK
OFF_R = OFF_Q + GLA_KW
OFF_POOL = OFF_R + GLA_VW
OFF_GP = OFF_POOL + POOL_WIDTH
OFF_GG = OFF_GP + D_MODEL
IN_COLS = OFF_GG + D_MODEL

V7X_VMEM_BYTES = 64 * 1024 * 1024
VMEM_LIMIT_BYTES = 56 * 1024 * 1024
SUBLANES = 8
TOKEN_TILE = 512
FF_CHUNK = 256
MOD_COL_TILE = 1152


def _resident(shape):
    return pl.BlockSpec(shape, lambda *_: (0,) * len(shape), pipeline_mode=pl.Buffered(1))


def _layer_resident(arr, l):
    tail = (0,) * (arr.ndim - 1)
    return pl.BlockSpec((1,) + arr.shape[1:], lambda *_: (l,) + tail, pipeline_mode=pl.Buffered(1))


def _params(*semantics):
    return pltpu.CompilerParams(dimension_semantics=semantics, vmem_limit_bytes=VMEM_LIMIT_BYTES)


def _per_batch_or_shared(arr):
    zeros = (0,) * (arr.ndim - 1)
    if arr.shape[0] == 1:
        return lambda b, i: (0,) + zeros
    return lambda b, i: (b,) + zeros


def _token_tile(t):
    return min(TOKEN_TILE, t)


def _rms(x):
    return x * lax.rsqrt(jnp.mean(x * x, axis=-1, keepdims=True) + EPS)


def _modulated_norm(x, w, shift, scale):
    return (_rms(x) * w) * (1.0 + scale) + shift


def _silu(a):
    return a / (1.0 + jnp.exp(-a))


def _sigmoid(a):
    return 1.0 / (1.0 + jnp.exp(-a))


def _dot(a, b):
    return jnp.dot(a, b, preferred_element_type=F32)


def _dot_nt(a, b):
    return lax.dot_general(a, b, (((1,), (1,)), ((), ())), preferred_element_type=F32)


def _head_blocks(x):
    head = lax.broadcasted_iota(jnp.int32, x.shape, 1) // GLA_DK
    zero = jnp.zeros((), x.dtype)
    return jnp.concatenate([jnp.where(head == h, x, zero) for h in range(GLA_HEADS)], axis=0)


def _mod_kernel(cond_ref, w_ref, b_ref, o_ref):
    cond = cond_ref[...]
    o_ref[0] = _dot(_silu(cond).astype(BF16), w_ref[0].astype(BF16)) + b_ref[0]


def _modulation(cond, w_mod, b_mod):
    n_cols = N_MOD * D_MODEL
    rows = cond.shape[0]
    return pl.pallas_call(
        _mod_kernel,
        grid=(DEPTH, n_cols // MOD_COL_TILE),
        in_specs=[
            pl.BlockSpec((rows, D_MODEL), lambda l, j: (0, 0)),
            pl.BlockSpec((1, D_MODEL, MOD_COL_TILE), lambda l, j: (l, 0, j)),
            pl.BlockSpec((1, 1, MOD_COL_TILE), lambda l, j: (l, 0, j)),
        ],
        out_specs=pl.BlockSpec((1, rows, MOD_COL_TILE), lambda l, j: (l, 0, j)),
        out_shape=jax.ShapeDtypeStruct((DEPTH, rows, n_cols), F32),
        compiler_params=_params("arbitrary", "arbitrary"),
        name="modulation",
    )(cond, w_mod, b_mod.reshape(DEPTH, 1, n_cols))


def _ffn_kernel(x_ref, mod_ref, nw_ref, up_ref, down_ref, fw_ref, o_ref, *, mod_row, final_norm):
    x = x_ref[0]
    mod = mod_ref[0]
    shift = mod[mod_row:mod_row + 1]
    scale = mod[mod_row + 1:mod_row + 2]
    gate = mod[mod_row + 2:mod_row + 3]
    h = _modulated_norm(x, nw_ref[...], shift, scale).astype(BF16)
    acc = jnp.zeros(x.shape, F32)
    for j in range(D_FF // FF_CHUNK):
        lo = j * FF_CHUNK
        a = _dot(h, up_ref[0, :, lo:lo + FF_CHUNK])
        b = _dot(h, up_ref[0, :, D_FF + lo:D_FF + lo + FF_CHUNK])
        act = (_silu(a) * b).astype(BF16)
        acc = acc + _dot(act, down_ref[0, lo:lo + FF_CHUNK, :])
    y = x + (0.5 * gate) * acc
    if final_norm:
        y = _rms(y) * fw_ref[...]
    o_ref[0] = y


def _ffn(x, mod, nw, up, down, fw, l, *, mod_row, final_norm):
    b, t, d = x.shape
    tm = _token_tile(t)
    kern = functools.partial(_ffn_kernel, mod_row=mod_row, final_norm=final_norm)
    return pl.pallas_call(
        kern,
        grid=(b, t // tm),
        in_specs=[
            pl.BlockSpec((1, tm, d), lambda b, i: (b, i, 0)),
            pl.BlockSpec((1, N_MOD, d), _per_batch_or_shared(mod)),
            _resident((1, d)),
            _layer_resident(up, l),
            _layer_resident(down, l),
            _resident((1, d)),
        ],
        out_specs=pl.BlockSpec((1, tm, d), lambda b, i: (b, i, 0)),
        out_shape=jax.ShapeDtypeStruct(x.shape, F32),
        compiler_params=_params("parallel", "parallel"),
        name="ffn",
    )(x, mod, nw, up, down, fw)


_PROJ_WEIGHTS = ('w_k', 'w_v', 'w_a', 'w_q', 'w_r', 'w_u', 'w_gp', 'w_gg', 'w_gate')


def _proj_kernel(x_ref, mod_ref, nw_ref, wk_ref, wv_ref, wa_ref, wq_ref, wr_ref, wu_ref, wgp_ref, wgg_ref,
                 wgate_ref, bgate_ref,
                 qf_ref, kef_ref, decf_ref, qb_ref, keb_ref, decb_ref, vst_ref, oi_ref,
                 r_ref, u_ref, gp_ref, gg_ref):
    x = x_ref[0]
    mod = mod_ref[0]
    h = _modulated_norm(x, nw_ref[...], mod[3:4], mod[4:5]).astype(BF16)
    k_all = _dot(h, wk_ref[0])
    q_all = _dot(h, wq_ref[0]) * (GLA_DK ** -0.5)
    v_all = _dot(h, wv_ref[0])
    a_low = _dot(h, wa_ref[0]).astype(BF16)
    z = _dot(a_low, wgate_ref[0]) + bgate_ref[...]
    g_all = (jnp.minimum(z, 0.0) - jnp.log1p(jnp.exp(-jnp.abs(z)))) / GATE_TAU
    r_ref[0] = _dot(h, wr_ref[0])
    u_ref[0] = _dot(h, wu_ref[0])
    gp_ref[0] = _dot(h, wgp_ref[0])
    gg_ref[0] = _dot(h, wgg_ref[0])

    row = lax.broadcasted_iota(jnp.int32, (CHUNK, CHUNK), 0)
    col = lax.broadcasted_iota(jnp.int32, (CHUNK, CHUNK), 1)
    tri_f = jnp.where(col <= row, 1.0, 0.0).astype(BF16)
    tri_b = jnp.where(col >= row, 1.0, 0.0).astype(BF16)
    arow = lax.broadcasted_iota(jnp.int32, (CHUNK, GLA_KW), 0)
    acol = lax.broadcasted_iota(jnp.int32, (CHUNK, GLA_KW), 1) % CHUNK
    causal = acol <= arow
    v_head = lax.broadcasted_iota(jnp.int32, (CHUNK, GLA_VW), 1) // GLA_DV

    def cumulative(tri, g):
        g_hi = g.astype(BF16)
        g_lo = (g - g_hi.astype(F32)).astype(BF16)
        return _dot(tri, g_hi) + _dot(tri, g_lo)

    for c in range(x.shape[0] // CHUNK):
        rows = slice(c * CHUNK, (c + 1) * CHUNK)
        q, k, v, g = q_all[rows], k_all[rows], v_all[rows], g_all[rows]
        b_f = cumulative(tri_f, g[:, :GLA_KW])
        b_b = cumulative(tri_b, g[:, GLA_KW:])
        edge_f = b_f[CHUNK - 1:CHUNK]
        edge_b = b_b[0:1]
        q_f = (q * jnp.exp(b_f)).astype(BF16)
        q_b = (q * jnp.exp(b_b)).astype(BF16)
        k_f = (k * jnp.exp(-b_f)).astype(BF16)
        k_b = (k * jnp.exp(-b_b)).astype(BF16)
        qf_ref[0, rows, :] = q_f
        qb_ref[0, rows, :] = q_b
        kef_ref[0, rows, :] = (k * jnp.exp(edge_f - b_f)).astype(BF16)
        keb_ref[0, rows, :] = (k * jnp.exp(edge_b - b_b)).astype(BF16)
        decf_ref[0, c:c + 1, :] = jnp.exp(edge_f)
        decb_ref[0, c:c + 1, :] = jnp.exp(edge_b)
        att = jnp.where(causal, _dot_nt(q_f, _head_blocks(k_f)), _dot_nt(q_b, _head_blocks(k_b)))
        v_bf = v.astype(BF16)
        zero = jnp.zeros((), BF16)
        v_blk = jnp.concatenate([jnp.where(v_head == hd, v_bf, zero) for hd in range(GLA_HEADS)], axis=0)
        oi_ref[0, rows, :] = _dot(att.astype(BF16), v_blk)
        v_stack = jnp.concatenate([v[:, hd * GLA_DV:(hd + 1) * GLA_DV] for hd in range(GLA_HEADS)], axis=0)
        vst_ref[0, c] = v_stack.T.astype(BF16)


def _proj(x, mod, nw, lw, l):
    b, t, d = x.shape
    tm = _token_tile(t)
    nc, tc = t // CHUNK, tm // CHUNK
    tok = lambda w: pl.BlockSpec((1, tm, w), lambda b, i: (b, i, 0))
    dec = pl.BlockSpec((1, tc, GLA_KW), lambda b, i: (b, i, 0))
    outs = [
        ('qf', (b, t, GLA_KW), BF16, tok(GLA_KW)), ('kef', (b, t, GLA_KW), BF16, tok(GLA_KW)),
        ('decf', (b, nc, GLA_KW), F32, dec),
        ('qb', (b, t, GLA_KW), BF16, tok(GLA_KW)), ('keb', (b, t, GLA_KW), BF16, tok(GLA_KW)),
        ('decb', (b, nc, GLA_KW), F32, dec),
        ('vst', (b, nc, GLA_DV, GLA_KW), BF16, pl.BlockSpec((1, tc, GLA_DV, GLA_KW), lambda b, i: (b, i, 0, 0))),
        ('oi', (b, t, GLA_VW), F32, tok(GLA_VW)),
        ('r', (b, t, GLA_VW), F32, tok(GLA_VW)), ('u', (b, t, POOL_WIDTH), F32, tok(POOL_WIDTH)),
        ('gp', (b, t, d), F32, tok(d)), ('gg', (b, t, d), F32, tok(d)),
    ]
    weights = [lw[n] for n in _PROJ_WEIGHTS]
    res = pl.pallas_call(
        _proj_kernel,
        grid=(b, t // tm),
        in_specs=[tok(d), pl.BlockSpec((1, N_MOD, d), _per_batch_or_shared(mod)), _resident((1, d))]
        + [_layer_resident(w, l) for w in weights] + [_resident(lw['b_gate'].shape)],
        out_specs=[o[3] for o in outs],
        out_shape=[jax.ShapeDtypeStruct(o[1], o[2]) for o in outs],
        compiler_params=_params("parallel", "parallel"),
        name="mixer_proj",
    )(x, mod, nw, *weights, lw['b_gate'])
    return {o[0]: a for o, a in zip(outs, res)}


def _gla_kernel(qf_ref, kef_ref, decf_ref, qb_ref, keb_ref, decb_ref, vst_ref, oi_ref, r_ref,
                sf0_ref, sb0_ref, nw_ref, og_ref, sf_ref, sb_ref, st_ref, ob_ref, *, n_blocks, n_chunks):
    step = pl.program_id(1)
    tb = n_chunks * CHUNK

    def chunk(q_ref, ke_ref, dec_ref, c):
        rows = pl.ds(c * CHUNK, CHUNK)
        st = st_ref[...]
        o_stack = _dot_nt(_head_blocks(q_ref[0, rows, :]), st.astype(BF16))
        st_ref[...] = st * dec_ref[0, c:c + 1, :] + _dot(vst_ref[0, c], _head_blocks(ke_ref[0, rows, :]))
        return jnp.concatenate([o_stack[h * CHUNK:(h + 1) * CHUNK] for h in range(GLA_HEADS)], axis=1)

    @pl.when(step == 0)
    def _():
        st_ref[...] = sb0_ref[0]

    @pl.when(step < n_blocks)
    def _():
        base = (n_blocks - 1 - step) * tb
        for c in reversed(range(n_chunks)):
            o = chunk(qb_ref, keb_ref, decb_ref, c)
            ob_ref[pl.ds(pl.multiple_of(base + c * CHUNK, CHUNK), CHUNK), :] = o

    @pl.when(step == n_blocks - 1)
    def _():
        sb_ref[0] = st_ref[...]

    @pl.when(step == n_blocks)
    def _():
        st_ref[...] = sf0_ref[0]

    @pl.when(step >= n_blocks)
    def _():
        base = (step - n_blocks) * tb
        nw = nw_ref[...]
        for c in range(n_chunks):
            rows = pl.ds(c * CHUNK, CHUNK)
            o = chunk(qf_ref, kef_ref, decf_ref, c)
            o = o + ob_ref[pl.ds(pl.multiple_of(base + c * CHUNK, CHUNK), CHUNK), :] + oi_ref[0, rows, :]
            o = jnp.concatenate(
                [_rms(o[:, h * GLA_DV:(h + 1) * GLA_DV]) * nw for h in range(GLA_HEADS)], axis=1)
            og_ref[0, rows, :] = (o * _silu(r_ref[0, rows, :])).astype(og_ref.dtype)

    @pl.when(step == 2 * n_blocks - 1)
    def _():
        sf_ref[0] = st_ref[...]


def _gla(p, r, s_f0, s_b0, norm_w):
    b, t, _ = p['qf'].shape
    tb = _token_tile(t)
    nb = t // tb
    tc = tb // CHUNK
    fwd = lambda b, s: (b, jnp.maximum(s - nb, 0), 0)
    rev = lambda b, s: (b, nb - 1 - jnp.minimum(s, nb - 1), 0)
    both = lambda b, s: (b, jnp.where(s < nb, nb - 1 - s, s - nb), 0, 0)
    tok = lambda w, m: pl.BlockSpec((1, tb, w), m)
    dec = lambda m: pl.BlockSpec((1, tc, GLA_KW), m)
    state = pl.BlockSpec((1, GLA_DV, GLA_KW), lambda b, s: (b, 0, 0))
    kern = functools.partial(_gla_kernel, n_blocks=nb, n_chunks=tc)
    return pl.pallas_call(
        kern,
        grid=(b, 2 * nb),
        in_specs=[tok(GLA_KW, fwd), tok(GLA_KW, fwd), dec(fwd), tok(GLA_KW, rev), tok(GLA_KW, rev), dec(rev),
                  pl.BlockSpec((1, tc, GLA_DV, GLA_KW), both), tok(GLA_VW, fwd), tok(GLA_VW, fwd),
                  state, state, _resident((1, GLA_DV))],
        out_specs=[tok(GLA_VW, fwd), state, state],
        out_shape=[jax.ShapeDtypeStruct((b, t, GLA_VW), BF16),
                   jax.ShapeDtypeStruct((b, GLA_DV, GLA_KW), F32),
                   jax.ShapeDtypeStruct((b, GLA_DV, GLA_KW), F32)],
        scratch_shapes=[pltpu.VMEM((GLA_DV, GLA_KW), F32), pltpu.VMEM((t, GLA_VW), F32)],
        compiler_params=_params("arbitrary", "arbitrary"),
        name="gla_scan",
    )(p['qf'], p['kef'], p['decf'], p['qb'], p['keb'], p['decb'], p['vst'], p['oi'], r, s_f0, s_b0, norm_w)


def _box_mean(u, idx, seg, unit, w):
    n = u.shape[0]
    half = w // 2

    def ahead(x, s):
        return jnp.where(idx + s < seg, pltpu.roll(x, (n - s * unit) % n, axis=0), 0.0)

    def behind(x, s):
        return jnp.where(idx - s >= 0, pltpu.roll(x, (s * unit) % n, axis=0), 0.0)

    fwd = u
    span = 1
    while span < half:
        fwd = fwd + ahead(fwd, span)
        span *= 2
    bwd = behind(u, 1)
    span = 1
    while span < half:
        bwd = bwd + behind(bwd, span)
        span *= 2
    cnt = (jnp.minimum(idx + half, seg) - jnp.maximum(idx - half, 0)).astype(F32)
    return (fwd + bwd) / cnt


def _pool_kernel(u_ref, w_ref, scale_ref, o_ref, *, n_rows, n_cols):
    group = pl.program_id(1)
    n = n_rows * n_cols
    tok = lax.broadcasted_iota(jnp.int32, (n, POOL_GROUP), 0)
    for gi, w in enumerate(POOL_WINDOWS):
        @pl.when(group == gi)
        def _(w=w):
            u = u_ref[0]
            m = u
            if n_rows > 1:
                m = _box_mean(m, tok // n_cols, n_rows, n_cols, w)
            m = _box_mean(m, tok % n_cols, n_cols, 1, w)
            y = _dot((m - u).astype(BF16), w_ref[0, 0]) * scale_ref[0]
            o_ref[0] = y.astype(o_ref.dtype)


def _pool(u, pool_w, pool_scale, l, *, n_rows, n_cols):
    b, t, _ = u.shape
    kern = functools.partial(_pool_kernel, n_rows=n_rows, n_cols=n_cols)
    return pl.pallas_call(
        kern,
        grid=(b, N_POOL_GROUPS),
        in_specs=[
            pl.BlockSpec((1, t, POOL_GROUP), lambda b, g: (b, 0, g)),
            pl.BlockSpec((1, 1, POOL_GROUP, POOL_GROUP), lambda b, g: (l, g, 0, 0)),
            pl.BlockSpec((1, 1, POOL_GROUP), lambda b, g: (g, 0, 0)),
        ],
        out_specs=pl.BlockSpec((1, t, POOL_GROUP), lambda b, g: (b, 0, g)),
        out_shape=jax.ShapeDtypeStruct((b, t, POOL_WIDTH), BF16),
        compiler_params=_params("parallel", "parallel"),
        name="pool_mix",
    )(u, pool_w, pool_scale)


def _merge_kernel(x_ref, mod_ref, og_ref, yp_ref, gp_ref, gg_ref, wbg_ref, wbp_ref, wo_ref, o_ref):
    y_gla = _dot(og_ref[0], wbg_ref[0])
    y_pool = _dot(yp_ref[0], wbp_ref[0])
    merged = _sigmoid(gp_ref[0]) * y_pool + _sigmoid(gg_ref[0]) * y_gla
    gate = mod_ref[0][5:6]
    o_ref[0] = x_ref[0] + gate * _dot(merged.astype(BF16), wo_ref[0])


def _merge(x, mod, og, yp, gp, gg, lw, l):
    b, t, d = x.shape
    tm = _token_tile(t)
    tok = lambda w: pl.BlockSpec((1, tm, w), lambda b, i: (b, i, 0))
    weights = [lw['w_br_gla'], lw['w_br_pool'], lw['w_out']]
    return pl.pallas_call(
        _merge_kernel,
        grid=(b, t // tm),
        in_specs=[tok(d), pl.BlockSpec((1, N_MOD, d), _per_batch_or_shared(mod)),
                  tok(GLA_VW), tok(POOL_WIDTH), tok(d), tok(d)] + [_layer_resident(w, l) for w in weights],
        out_specs=tok(d),
        out_shape=jax.ShapeDtypeStruct(x.shape, F32),
        compiler_params=_params("parallel", "parallel"),
        name="mixer_merge",
    )(x, mod, og, yp, gp, gg, *weights)


def _mixer_weights(w_in, w_af_up, w_ab_up, pool_w, w_br_pool, w_br_gla, w_out):
    cols = lambda lo, hi: w_in[:, :, lo:hi].astype(BF16)
    zeros = jnp.zeros((DEPTH, GATE_RANK, GLA_KW), F32)
    w_gate = jnp.concatenate([jnp.concatenate([w_af_up, zeros], axis=2),
                              jnp.concatenate([zeros, w_ab_up], axis=2)], axis=1)
    return {
        'w_k': cols(OFF_K, OFF_V), 'w_v': cols(OFF_V, OFF_AF), 'w_a': cols(OFF_AF, OFF_Q),
        'w_q': cols(OFF_Q, OFF_R), 'w_r': cols(OFF_R, OFF_POOL), 'w_u': cols(OFF_POOL, OFF_GP),
        'w_gp': cols(OFF_GP, OFF_GG), 'w_gg': cols(OFF_GG, IN_COLS),
        'w_gate': w_gate.astype(BF16),
        'pool_w': pool_w.astype(BF16),
        'w_br_pool': w_br_pool.astype(BF16), 'w_br_gla': w_br_gla.astype(BF16), 'w_out': w_out.astype(BF16),
    }


def _mixer(x, mod, nw, lw, l, s_f0, s_b0, gla_norm_w, *, n_rows, n_cols, states_only=False):
    p = _proj(x, mod, nw, lw, l)
    og, s_f, s_b = _gla(p, p['r'], s_f0, s_b0, gla_norm_w)
    if states_only:
        return None, s_f, s_b
    yp = _pool(p['u'], lw['pool_w'], lw['pool_scale'], l, n_rows=n_rows, n_cols=n_cols)
    return _merge(x, mod, og, yp, p['gp'], p['gg'], lw, l), s_f, s_b


def kernel(x, c, ctx, c_ctx, w_mod, b_mod, norm_w, ffn1_up, ffn1_down, w_in, w_af_up, b_af, w_ab_up, b_ab,
           gla_norm_w, pool_w, pool_scale, w_br_pool, w_br_gla, w_out, ffn2_up, ffn2_down, final_norm_w):
    batch, seq, d = x.shape
    ctx_len = ctx.shape[1]
    cond = jnp.concatenate([c, c_ctx[None, :], jnp.zeros((SUBLANES - batch - 1, d), F32)], axis=0)
    mod_all = _modulation(cond, w_mod, b_mod)
    zero_state = jnp.zeros((batch, GLA_DV, GLA_KW), F32)
    fw = final_norm_w[None, :]
    up1, down1 = ffn1_up.astype(BF16), ffn1_down.astype(BF16)
    up2, down2 = ffn2_up.astype(BF16), ffn2_down.astype(BF16)
    lw = _mixer_weights(w_in, w_af_up, w_ab_up, pool_w, w_br_pool, w_br_gla, w_out)
    for l in range(DEPTH):
        last = l == DEPTH - 1
        mod_x = mod_all[l, :batch].reshape(batch, N_MOD, d)
        mod_c = mod_all[l, batch:batch + 1].reshape(1, N_MOD, d)
        nw = norm_w[l][:, None, :]
        lw['b_gate'] = jnp.concatenate([b_af[l], b_ab[l]])[None, :]
        lw['pool_scale'] = pool_scale[l].reshape(N_POOL_GROUPS, 1, POOL_GROUP)
        gnw = gla_norm_w[l][None, :]

        x = _ffn(x, mod_x, nw[0], up1, down1, fw, l, mod_row=0, final_norm=False)
        ctx = _ffn(ctx, mod_c, nw[0], up1, down1, fw, l, mod_row=0, final_norm=False)

        ctx_mixed, s_f, s_b = _mixer(ctx, mod_c, nw[1], lw, l, zero_state, zero_state, gnw,
                                     n_rows=1, n_cols=ctx_len, states_only=last)
        if not last:
            ctx = _ffn(ctx_mixed, mod_c, nw[2], up2, down2, fw, l, mod_row=6, final_norm=False)

        x, _, _ = _mixer(x, mod_x, nw[1], lw, l, s_f, s_b, gnw, n_rows=seq // GRID_W, n_cols=GRID_W)
        x = _ffn(x, mod_x, nw[2], up2, down2, fw, l, mod_row=6, final_norm=last)
    return x
```

```python
import functools

import jax
import jax.numpy as jnp
from jax import lax
from jax.experimental import pallas as pl
from jax.experimental.pallas import tpu as pltpu

F32 = jnp.float32
BF16 = jnp.bfloat16

D_MODEL = 1024
DEPTH = 2
GRID_W = 64
N_POOL_GROUPS = 4
POOL_GROUP = 128
POOL_WIDTH = N_POOL_GROUPS * POOL_GROUP
POOL_WINDOWS = (2, 4, 8, 16)
GLA_HEADS = 4
GLA_DK = 64
GLA_DV = 128
GLA_KW = GLA_HEADS * GLA_DK
GLA_VW = GLA_HEADS * GLA_DV
GATE_RANK = 16
GATE_TAU = 16.0
CHUNK = 64
D_FF = 2816
N_MOD = 9
EPS = 1e-6

OFF_K = 0
OFF_V = OFF_K + GLA_KW
OFF_AF = OFF_V + GLA_VW
OFF_AB = OFF_AF + GATE_RANK
OFF_Q = OFF_AB + GATE_RANK
OFF_R = OFF_Q + GLA_KW
OFF_POOL = OFF_R + GLA_VW
OFF_GP = OFF_POOL + POOL_WIDTH
OFF_GG = OFF_GP + D_MODEL
IN_COLS = OFF_GG + D_MODEL

V7X_VMEM_BYTES = 64 * 1024 * 1024
VMEM_LIMIT_BYTES = 56 * 1024 * 1024
SUBLANES = 8
TOKEN_TILE = 512
FF_CHUNK = 256
MOD_COL_TILE = 1152


def _resident(shape):
    return pl.BlockSpec(shape, lambda *_: (0,) * len(shape), pipeline_mode=pl.Buffered(1))


def _layer_resident(arr, l):
    tail = (0,) * (arr.ndim - 1)
    return pl.BlockSpec((1,) + arr.shape[1:], lambda *_: (l,) + tail, pipeline_mode=pl.Buffered(1))


def _params(*semantics):
    return pltpu.CompilerParams(dimension_semantics=semantics, vmem_limit_bytes=VMEM_LIMIT_BYTES)


def _per_batch_or_shared(arr):
    zeros = (0,) * (arr.ndim - 1)
    if arr.shape[0] == 1:
        return lambda b, i: (0,) + zeros
    return lambda b, i: (b,) + zeros


def _token_tile(t):
    return min(TOKEN_TILE, t)


def _rms(x):
    return x * lax.rsqrt(jnp.mean(x * x, axis=-1, keepdims=True) + EPS)


def _modulated_norm(x, w, shift, scale):
    return (_rms(x) * w) * (1.0 + scale) + shift


def _silu(a):
    return a / (1.0 + jnp.exp(-a))


def _sigmoid(a):
    return 1.0 / (1.0 + jnp.exp(-a))


def _dot(a, b):
    return jnp.dot(a, b, preferred_element_type=F32)


def _dot_nt(a, b):
    return lax.dot_general(a, b, (((1,), (1,)), ((), ())), preferred_element_type=F32)


def _head_blocks(x):
    head = lax.broadcasted_iota(jnp.int32, x.shape, 1) // GLA_DK
    zero = jnp.zeros((), x.dtype)
    return jnp.concatenate([jnp.where(head == h, x, zero) for h in range(GLA_HEADS)], axis=0)


def _chunk_cumsum(g, *, reverse):
    n = g.shape[0]
    pos = lax.broadcasted_iota(jnp.int32, g.shape, 0) % CHUNK
    acc = g
    span = 1
    while span < CHUNK:
        if reverse:
            acc = acc + jnp.where(pos + span < CHUNK, pltpu.roll(acc, n - span, axis=0), 0.0)
        else:
            acc = acc + jnp.where(pos >= span, pltpu.roll(acc, span, axis=0), 0.0)
        span *= 2
    return acc


def _mod_kernel(cond_ref, w_ref, b_ref, o_ref):
    cond = cond_ref[...]
    o_ref[0] = _dot(_silu(cond).astype(BF16), w_ref[0].astype(BF16)) + b_ref[0]


def _modulation(cond, w_mod, b_mod):
    n_cols = N_MOD * D_MODEL
    rows = cond.shape[0]
    return pl.pallas_call(
        _mod_kernel,
        grid=(DEPTH, n_cols // MOD_COL_TILE),
        in_specs=[
            pl.BlockSpec((rows, D_MODEL), lambda l, j: (0, 0)),
            pl.BlockSpec((1, D_MODEL, MOD_COL_TILE), lambda l, j: (l, 0, j)),
            pl.BlockSpec((1, 1, MOD_COL_TILE), lambda l, j: (l, 0, j)),
        ],
        out_specs=pl.BlockSpec((1, rows, MOD_COL_TILE), lambda l, j: (l, 0, j)),
        out_shape=jax.ShapeDtypeStruct((DEPTH, rows, n_cols), F32),
        compiler_params=_params("arbitrary", "arbitrary"),
        name="modulation",
    )(cond, w_mod, b_mod.reshape(DEPTH, 1, n_cols))


def _ffn_kernel(x_ref, mod_ref, nw_ref, up_ref, down_ref, fw_ref, o_ref, *, mod_row, final_norm):
    x = x_ref[0]
    mod = mod_ref[0]
    shift = mod[mod_row:mod_row + 1]
    scale = mod[mod_row + 1:mod_row + 2]
    gate = mod[mod_row + 2:mod_row + 3]
    h = _modulated_norm(x, nw_ref[...], shift, scale).astype(BF16)
    acc = jnp.zeros(x.shape, F32)
    for j in range(D_FF // FF_CHUNK):
        lo = j * FF_CHUNK
        a = _dot(h, up_ref[0, :, lo:lo + FF_CHUNK])
        b = _dot(h, up_ref[0, :, D_FF + lo:D_FF + lo + FF_CHUNK])
        act = (_silu(a) * b).astype(BF16)
        acc = acc + _dot(act, down_ref[0, lo:lo + FF_CHUNK, :])
    y = x + (0.5 * gate) * acc
    if final_norm:
        y = _rms(y) * fw_ref[...]
    o_ref[0] = y


def _ffn(x, mod, nw, up, down, fw, l, *, mod_row, final_norm):
    b, t, d = x.shape
    tm = _token_tile(t)
    kern = functools.partial(_ffn_kernel, mod_row=mod_row, final_norm=final_norm)
    return pl.pallas_call(
        kern,
        grid=(b, t // tm),
        in_specs=[
            pl.BlockSpec((1, tm, d), lambda b, i: (b, i, 0)),
            pl.BlockSpec((1, N_MOD, d), _per_batch_or_shared(mod)),
            _resident((1, d)),
            _layer_resident(up, l),
            _layer_resident(down, l),
            _resident((1, d)),
        ],
        out_specs=pl.BlockSpec((1, tm, d), lambda b, i: (b, i, 0)),
        out_shape=jax.ShapeDtypeStruct(x.shape, F32),
        compiler_params=_params("parallel", "parallel"),
        name="ffn",
    )(x, mod, nw, up, down, fw)


_PROJ_WEIGHTS = ('w_k', 'w_v', 'w_a', 'w_q', 'w_r', 'w_u', 'w_gp', 'w_gg', 'w_gate')


def _proj_kernel(x_ref, mod_ref, nw_ref, wk_ref, wv_ref, wa_ref, wq_ref, wr_ref, wu_ref, wgp_ref, wgg_ref,
                 wgate_ref, bgate_ref,
                 qf_ref, kef_ref, decf_ref, qb_ref, keb_ref, decb_ref, vst_ref, oi_ref,
                 r_ref, u_ref, gp_ref, gg_ref):
    x = x_ref[0]
    mod = mod_ref[0]
    tm = x.shape[0]
    chunks = [slice(c * CHUNK, (c + 1) * CHUNK) for c in range(tm // CHUNK)]
    h = _modulated_norm(x, nw_ref[...], mod[3:4], mod[4:5]).astype(BF16)
    k_all = _dot(h, wk_ref[0])
    q_all = _dot(h, wq_ref[0]) * (GLA_DK ** -0.5)
    v_all = _dot(h, wv_ref[0])
    a_low = _dot(h, wa_ref[0]).astype(BF16)
    z = _dot(a_low, wgate_ref[0]) + bgate_ref[...]
    g_all = (jnp.minimum(z, 0.0) - jnp.log1p(jnp.exp(-jnp.abs(z)))) / GATE_TAU

    b_f = _chunk_cumsum(g_all[:, :GLA_KW], reverse=False)
    b_b = _chunk_cumsum(g_all[:, GLA_KW:], reverse=True)
    r_ref[0] = _silu(_dot(h, wr_ref[0])).astype(BF16)
    u_ref[0] = _dot(h, wu_ref[0])

    q_f, q_b, k_f, k_b = [], [], [], []
    for c, rows in enumerate(chunks):
        q, k, bf, bb = q_all[rows], k_all[rows], b_f[rows], b_b[rows]
        edge_f = bf[CHUNK - 1:CHUNK]
        edge_b = bb[0:1]
        q_f.append((q * jnp.exp(bf)).astype(BF16))
        q_b.append((q * jnp.exp(bb)).astype(BF16))
        k_f.append((k * jnp.exp(-bf)).astype(BF16))
        k_b.append((k * jnp.exp(-bb)).astype(BF16))
        qf_ref[0, rows, :] = q_f[c]
        qb_ref[0, rows, :] = q_b[c]
        kef_ref[0, rows, :] = (k * jnp.exp(edge_f - bf)).astype(BF16)
        keb_ref[0, rows, :] = (k * jnp.exp(edge_b - bb)).astype(BF16)
        decf_ref[0, c:c + 1, :] = jnp.exp(edge_f)
        decb_ref[0, c:c + 1, :] = jnp.exp(edge_b)

    arow = lax.broadcasted_iota(jnp.int32, (CHUNK, GLA_KW), 0)
    acol = lax.broadcasted_iota(jnp.int32, (CHUNK, GLA_KW), 1) % CHUNK
    causal = acol <= arow
    att = [jnp.where(causal, _dot_nt(q_f[c], _head_blocks(k_f[c])), _dot_nt(q_b[c], _head_blocks(k_b[c])))
           .astype(BF16) for c in range(len(chunks))]
    gp_ref[0] = _sigmoid(_dot(h, wgp_ref[0])).astype(BF16)

    v_head = lax.broadcasted_iota(jnp.int32, (CHUNK, GLA_VW), 1) // GLA_DV
    zero = jnp.zeros((), BF16)
    for c, rows in enumerate(chunks):
        v_bf = v_all[rows].astype(BF16)
        v_blk = jnp.concatenate([jnp.where(v_head == hd, v_bf, zero) for hd in range(GLA_HEADS)], axis=0)
        oi_ref[0, rows, :] = _dot(att[c], v_blk)
    gg_ref[0] = _sigmoid(_dot(h, wgg_ref[0])).astype(BF16)

    for c, rows in enumerate(chunks):
        v = v_all[rows]
        v_stack = jnp.concatenate([v[:, hd * GLA_DV:(hd + 1) * GLA_DV] for hd in range(GLA_HEADS)], axis=0)
        vst_ref[0, c] = v_stack.T.astype(BF16)


def _proj(x, mod, nw, lw, l):
    b, t, d = x.shape
    tm = _token_tile(t)
    nc, tc = t // CHUNK, tm // CHUNK
    tok = lambda w: pl.BlockSpec((1, tm, w), lambda b, i: (b, i, 0))
    dec = pl.BlockSpec((1, tc, GLA_KW), lambda b, i: (b, i, 0))
    outs = [
        ('qf', (b, t, GLA_KW), BF16, tok(GLA_KW)), ('kef', (b, t, GLA_KW), BF16, tok(GLA_KW)),
        ('decf', (b, nc, GLA_KW), F32, dec),
        ('qb', (b, t, GLA_KW), BF16, tok(GLA_KW)), ('keb', (b, t, GLA_KW), BF16, tok(GLA_KW)),
        ('decb', (b, nc, GLA_KW), F32, dec),
        ('vst', (b, nc, GLA_DV, GLA_KW), BF16, pl.BlockSpec((1, tc, GLA_DV, GLA_KW), lambda b, i: (b, i, 0, 0))),
        ('oi', (b, t, GLA_VW), F32, tok(GLA_VW)),
        ('r', (b, t, GLA_VW), BF16, tok(GLA_VW)), ('u', (b, t, POOL_WIDTH), F32, tok(POOL_WIDTH)),
        ('gp', (b, t, d), BF16, tok(d)), ('gg', (b, t, d), BF16, tok(d)),
    ]
    weights = [lw[n] for n in _PROJ_WEIGHTS]
    res = pl.pallas_call(
        _proj_kernel,
        grid=(b, t // tm),
        in_specs=[tok(d), pl.BlockSpec((1, N_MOD, d), _per_batch_or_shared(mod)), _resident((1, d))]
        + [_layer_resident(w, l) for w in weights] + [_resident(lw['b_gate'].shape)],
        out_specs=[o[3] for o in outs],
        out_shape=[jax.ShapeDtypeStruct(o[1], o[2]) for o in outs],
        compiler_params=_params("parallel", "parallel"),
        name="mixer_proj",
    )(x, mod, nw, *weights, lw['b_gate'])
    return {o[0]: a for o, a in zip(outs, res)}


def _gla_kernel(qf_ref, kef_ref, decf_ref, qb_ref, keb_ref, decb_ref, vst_ref, oi_ref, r_ref,
                sf0_ref, sb0_ref, nw_ref, og_ref, sf_ref, sb_ref, st_ref, ob_ref, *, n_blocks, n_chunks):
    step = pl.program_id(1)
    tb = n_chunks * CHUNK

    def chunk(q_ref, ke_ref, dec_ref, c):
        rows = pl.ds(c * CHUNK, CHUNK)
        st = st_ref[...]
        o_stack = _dot_nt(_head_blocks(q_ref[0, rows, :]), st.astype(BF16))
        st_ref[...] = st * dec_ref[0, c:c + 1, :] + _dot(vst_ref[0, c], _head_blocks(ke_ref[0, rows, :]))
        return jnp.concatenate([o_stack[h * CHUNK:(h + 1) * CHUNK] for h in range(GLA_HEADS)], axis=1)

    @pl.when(step == 0)
    def _():
        st_ref[...] = sb0_ref[0]

    @pl.when(step < n_blocks)
    def _():
        base = (n_blocks - 1 - step) * tb
        for c in reversed(range(n_chunks)):
            o = chunk(qb_ref, keb_ref, decb_ref, c)
            ob_ref[pl.ds(pl.multiple_of(base + c * CHUNK, CHUNK), CHUNK), :] = o

    @pl.when(step == n_blocks - 1)
    def _():
        sb_ref[0] = st_ref[...]

    @pl.when(step == n_blocks)
    def _():
        st_ref[...] = sf0_ref[0]

    @pl.when(step >= n_blocks)
    def _():
        base = (step - n_blocks) * tb
        nw = nw_ref[...]
        for c in range(n_chunks):
            rows = pl.ds(c * CHUNK, CHUNK)
            o = chunk(qf_ref, kef_ref, decf_ref, c)
            o = o + ob_ref[pl.ds(pl.multiple_of(base + c * CHUNK, CHUNK), CHUNK), :] + oi_ref[0, rows, :]
            o = jnp.concatenate(
                [_rms(o[:, h * GLA_DV:(h + 1) * GLA_DV]) * nw for h in range(GLA_HEADS)], axis=1)
            og_ref[0, rows, :] = (o * r_ref[0, rows, :].astype(F32)).astype(og_ref.dtype)

    @pl.when(step == 2 * n_blocks - 1)
    def _():
        sf_ref[0] = st_ref[...]


def _gla(p, r, s_f0, s_b0, norm_w):
    b, t, _ = p['qf'].shape
    tb = _token_tile(t)
    nb = t // tb
    tc = tb // CHUNK
    fwd = lambda b, s: (b, jnp.maximum(s - nb, 0), 0)
    rev = lambda b, s: (b, nb - 1 - jnp.minimum(s, nb - 1), 0)
    both = lambda b, s: (b, jnp.where(s < nb, nb - 1 - s, s - nb), 0, 0)
    tok = lambda w, m: pl.BlockSpec((1, tb, w), m)
    dec = lambda m: pl.BlockSpec((1, tc, GLA_KW), m)
    state = pl.BlockSpec((1, GLA_DV, GLA_KW), lambda b, s: (b, 0, 0))
    kern = functools.partial(_gla_kernel, n_blocks=nb, n_chunks=tc)
    return pl.pallas_call(
        kern,
        grid=(b, 2 * nb),
        in_specs=[tok(GLA_KW, fwd), tok(GLA_KW, fwd), dec(fwd), tok(GLA_KW, rev), tok(GLA_KW, rev), dec(rev),
                  pl.BlockSpec((1, tc, GLA_DV, GLA_KW), both), tok(GLA_VW, fwd), tok(GLA_VW, fwd),
                  state, state, _resident((1, GLA_DV))],
        out_specs=[tok(GLA_VW, fwd), state, state],
        out_shape=[jax.ShapeDtypeStruct((b, t, GLA_VW), BF16),
                   jax.ShapeDtypeStruct((b, GLA_DV, GLA_KW), F32),
                   jax.ShapeDtypeStruct((b, GLA_DV, GLA_KW), F32)],
        scratch_shapes=[pltpu.VMEM((GLA_DV, GLA_KW), F32), pltpu.VMEM((t, GLA_VW), F32)],
        compiler_params=_params("arbitrary", "arbitrary"),
        name="gla_scan",
    )(p['qf'], p['kef'], p['decf'], p['qb'], p['keb'], p['decb'], p['vst'], p['oi'], r, s_f0, s_b0, norm_w)


def _box_mean(u, idx, seg, unit, w):
    n = u.shape[0]
    half = w // 2

    def ahead(x, s):
        return jnp.where(idx + s < seg, pltpu.roll(x, (n - s * unit) % n, axis=0), 0.0)

    def behind(x, s):
        return jnp.where(idx - s >= 0, pltpu.roll(x, (s * unit) % n, axis=0), 0.0)

    fwd = u
    span = 1
    while span < half:
        fwd = fwd + ahead(fwd, span)
        span *= 2
    bwd = behind(u, 1)
    span = 1
    while span < half:
        bwd = bwd + behind(bwd, span)
        span *= 2
    cnt = (jnp.minimum(idx + half, seg) - jnp.maximum(idx - half, 0)).astype(F32)
    return (fwd + bwd) / cnt


def _pool_kernel(u_ref, w_ref, scale_ref, o_ref, *, n_rows, n_cols):
    group = pl.program_id(1)
    n = n_rows * n_cols
    tok = lax.broadcasted_iota(jnp.int32, (n, POOL_GROUP), 0)
    for gi, w in enumerate(POOL_WINDOWS):
        @pl.when(group == gi)
        def _(w=w):
            u = u_ref[0]
            m = u
            if n_rows > 1:
                m = _box_mean(m, tok // n_cols, n_rows, n_cols, w)
            m = _box_mean(m, tok % n_cols, n_cols, 1, w)
            y = _dot((m - u).astype(BF16), w_ref[0, 0]) * scale_ref[0]
            o_ref[0] = y.astype(o_ref.dtype)


def _pool(u, pool_w, pool_scale, l, *, n_rows, n_cols):
    b, t, _ = u.shape
    kern = functools.partial(_pool_kernel, n_rows=n_rows, n_cols=n_cols)
    return pl.pallas_call(
        kern,
        grid=(b, N_POOL_GROUPS),
        in_specs=[
            pl.BlockSpec((1, t, POOL_GROUP), lambda b, g: (b, 0, g)),
            pl.BlockSpec((1, 1, POOL_GROUP, POOL_GROUP), lambda b, g: (l, g, 0, 0)),
            pl.BlockSpec((1, 1, POOL_GROUP), lambda b, g: (g, 0, 0)),
        ],
        out_specs=pl.BlockSpec((1, t, POOL_GROUP), lambda b, g: (b, 0, g)),
        out_shape=jax.ShapeDtypeStruct((b, t, POOL_WIDTH), BF16),
        compiler_params=_params("parallel", "parallel"),
        name="pool_mix",
    )(u, pool_w, pool_scale)


def _merge_kernel(x_ref, mod_ref, og_ref, yp_ref, gp_ref, gg_ref, wbg_ref, wbp_ref, wo_ref, o_ref):
    y_gla = _dot(og_ref[0], wbg_ref[0])
    y_pool = _dot(yp_ref[0], wbp_ref[0])
    merged = gp_ref[0].astype(F32) * y_pool + gg_ref[0].astype(F32) * y_gla
    gate = mod_ref[0][5:6]
    o_ref[0] = x_ref[0] + gate * _dot(merged.astype(BF16), wo_ref[0])


def _merge(x, mod, og, yp, gp, gg, lw, l):
    b, t, d = x.shape
    tm = _token_tile(t)
    tok = lambda w: pl.BlockSpec((1, tm, w), lambda b, i: (b, i, 0))
    weights = [lw['w_br_gla'], lw['w_br_pool'], lw['w_out']]
    return pl.pallas_call(
        _merge_kernel,
        grid=(b, t // tm),
        in_specs=[tok(d), pl.BlockSpec((1, N_MOD, d), _per_batch_or_shared(mod)),
                  tok(GLA_VW), tok(POOL_WIDTH), tok(d), tok(d)] + [_layer_resident(w, l) for w in weights],
        out_specs=tok(d),
        out_shape=jax.ShapeDtypeStruct(x.shape, F32),
        compiler_params=_params("parallel", "parallel"),
        name="mixer_merge",
    )(x, mod, og, yp, gp, gg, *weights)


def _mixer_weights(w_in, w_af_up, w_ab_up, pool_w, w_br_pool, w_br_gla, w_out):
    cols = lambda lo, hi: w_in[:, :, lo:hi].astype(BF16)
    zeros = jnp.zeros((DEPTH, GATE_RANK, GLA_KW), F32)
    w_gate = jnp.concatenate([jnp.concatenate([w_af_up, zeros], axis=2),
                              jnp.concatenate([zeros, w_ab_up], axis=2)], axis=1)
    return {
        'w_k': cols(OFF_K, OFF_V), 'w_v': cols(OFF_V, OFF_AF), 'w_a': cols(OFF_AF, OFF_Q),
        'w_q': cols(OFF_Q, OFF_R), 'w_r': cols(OFF_R, OFF_POOL), 'w_u': cols(OFF_POOL, OFF_GP),
        'w_gp': cols(OFF_GP, OFF_GG), 'w_gg': cols(OFF_GG, IN_COLS),
        'w_gate': w_gate.astype(BF16),
        'pool_w': pool_w.astype(BF16),
        'w_br_pool': w_br_pool.astype(BF16), 'w_br_gla': w_br_gla.astype(BF16), 'w_out': w_out.astype(BF16),
    }


def _mixer(x, mod, nw, lw, l, s_f0, s_b0, gla_norm_w, *, n_rows, n_cols, states_only=False):
    p = _proj(x, mod, nw, lw, l)
    og, s_f, s_b = _gla(p, p['r'], s_f0, s_b0, gla_norm_w)
    if states_only:
        return None, s_f, s_b
    yp = _pool(p['u'], lw['pool_w'], lw['pool_scale'], l, n_rows=n_rows, n_cols=n_cols)
    return _merge(x, mod, og, yp, p['gp'], p['gg'], lw, l), s_f, s_b


def kernel(x, c, ctx, c_ctx, w_mod, b_mod, norm_w, ffn1_up, ffn1_down, w_in, w_af_up, b_af, w_ab_up, b_ab,
           gla_norm_w, pool_w, pool_scale, w_br_pool, w_br_gla, w_out, ffn2_up, ffn2_down, final_norm_w):
    batch, seq, d = x.shape
    ctx_len = ctx.shape[1]
    cond = jnp.concatenate([c, c_ctx[None, :], jnp.zeros((SUBLANES - batch - 1, d), F32)], axis=0)
    mod_all = _modulation(cond, w_mod, b_mod)
    zero_state = jnp.zeros((batch, GLA_DV, GLA_KW), F32)
    fw = final_norm_w[None, :]
    up1, down1 = ffn1_up.astype(BF16), ffn1_down.astype(BF16)
    up2, down2 = ffn2_up.astype(BF16), ffn2_down.astype(BF16)
    lw = _mixer_weights(w_in, w_af_up, w_ab_up, pool_w, w_br_pool, w_br_gla, w_out)
    for l in range(DEPTH):
        last = l == DEPTH - 1
        mod_x = mod_all[l, :batch].reshape(batch, N_MOD, d)
        mod_c = mod_all[l, batch:batch + 1].reshape(1, N_MOD, d)
        nw = norm_w[l][:, None, :]
        lw['b_gate'] = jnp.concatenate([b_af[l], b_ab[l]])[None, :]
        lw['pool_scale'] = pool_scale[l].reshape(N_POOL_GROUPS, 1, POOL_GROUP)
        gnw = gla_norm_w[l][None, :]

        x = _ffn(x, mod_x, nw[0], up1, down1, fw, l, mod_row=0, final_norm=False)
        ctx = _ffn(ctx, mod_c, nw[0], up1, down1, fw, l, mod_row=0, final_norm=False)

        ctx_mixed, s_f, s_b = _mixer(ctx, mod_c, nw[1], lw, l, zero_state, zero_state, gnw,
                                     n_rows=1, n_cols=ctx_len, states_only=last)
        if not last:
            ctx = _ffn(ctx_mixed, mod_c, nw[2], up2, down2, fw, l, mod_row=6, final_norm=False)

        x, _, _ = _mixer(x, mod_x, nw[1], lw, l, s_f, s_b, gnw, n_rows=seq // GRID_W, n_cols=GRID_W)
        x = _ffn(x, mod_x, nw[2], up2, down2, fw, l, mod_row=6, final_norm=last)
    return x
```

```python
import functools

import jax
import jax.numpy as jnp
from jax import lax
from jax.experimental import pallas as pl
from jax.experimental.pallas import tpu as pltpu

F32 = jnp.float32
BF16 = jnp.bfloat16

D_MODEL = 1024
DEPTH = 2
GRID_W = 64
N_POOL_GROUPS = 4
POOL_GROUP = 128
POOL_WIDTH = N_POOL_GROUPS * POOL_GROUP
POOL_WINDOWS = (2, 4, 8, 16)
GLA_HEADS = 4
GLA_DK = 64
GLA_DV = 128
GLA_KW = GLA_HEADS * GLA_DK
GLA_VW = GLA_HEADS * GLA_DV
GATE_RANK = 16
GATE_TAU = 16.0
CHUNK = 64
D_FF = 2816
N_MOD = 9
EPS = 1e-6

OFF_K = 0
OFF_V = OFF_K + GLA_KW
OFF_AF = OFF_V + GLA_VW
OFF_AB = OFF_AF + GATE_RANK
OFF_Q = OFF_AB + GATE_RANK
OFF_R = OFF_Q + GLA_KW
OFF_POOL = OFF_R + GLA_VW
OFF_GP = OFF_POOL + POOL_WIDTH
OFF_GG = OFF_GP + D_MODEL
IN_COLS = OFF_GG + D_MODEL

V7X_VMEM_BYTES = 64 * 1024 * 1024
VMEM_LIMIT_BYTES = 56 * 1024 * 1024
SUBLANES = 8
TOKEN_TILE = 512
FFN_TOKEN_TILE = 1024
SCAN_TOKEN_TILE = 1024
FF_CHUNK = 256
MOD_COL_TILE = 1152


def _resident(shape):
    return pl.BlockSpec(shape, lambda *_: (0,) * len(shape), pipeline_mode=pl.Buffered(1))


def _layer_resident(arr, l):
    tail = (0,) * (arr.ndim - 1)
    return pl.BlockSpec((1,) + arr.shape[1:], lambda *_: (l,) + tail, pipeline_mode=pl.Buffered(1))


def _params(*semantics):
    return pltpu.CompilerParams(dimension_semantics=semantics, vmem_limit_bytes=VMEM_LIMIT_BYTES)


def _per_batch_or_shared(arr):
    zeros = (0,) * (arr.ndim - 1)
    if arr.shape[0] == 1:
        return lambda b, i: (0,) + zeros
    return lambda b, i: (b,) + zeros


def _token_tile(t, tile=TOKEN_TILE):
    return min(tile, t)


def _rms(x):
    return x * lax.rsqrt(jnp.mean(x * x, axis=-1, keepdims=True) + EPS)


def _modulated_norm(x, w, shift, scale):
    return (_rms(x) * w) * (1.0 + scale) + shift


def _silu(a):
    return a / (1.0 + jnp.exp(-a))


def _sigmoid(a):
    return 1.0 / (1.0 + jnp.exp(-a))


def _dot(a, b):
    return jnp.dot(a, b, preferred_element_type=F32)


def _dot_nt(a, b):
    return lax.dot_general(a, b, (((1,), (1,)), ((), ())), preferred_element_type=F32)


def _head_blocks(x):
    head = lax.broadcasted_iota(jnp.int32, x.shape, 1) // GLA_DK
    zero = jnp.zeros((), x.dtype)
    return jnp.concatenate([jnp.where(head == h, x, zero) for h in range(GLA_HEADS)], axis=0)


def _chunk_cumsum(g, *, reverse):
    n = g.shape[0]
    pos = lax.broadcasted_iota(jnp.int32, g.shape, 0) % CHUNK
    acc = g
    span = 1
    while span < CHUNK:
        if reverse:
            acc = acc + jnp.where(pos + span < CHUNK, pltpu.roll(acc, n - span, axis=0), 0.0)
        else:
            acc = acc + jnp.where(pos >= span, pltpu.roll(acc, span, axis=0), 0.0)
        span *= 2
    return acc


def _mod_kernel(cond_ref, w_ref, b_ref, o_ref):
    cond = cond_ref[...]
    o_ref[0] = _dot(_silu(cond).astype(BF16), w_ref[0].astype(BF16)) + b_ref[0]


def _modulation(cond, w_mod, b_mod):
    n_cols = N_MOD * D_MODEL
    rows = cond.shape[0]
    return pl.pallas_call(
        _mod_kernel,
        grid=(DEPTH, n_cols // MOD_COL_TILE),
        in_specs=[
            pl.BlockSpec((rows, D_MODEL), lambda l, j: (0, 0)),
            pl.BlockSpec((1, D_MODEL, MOD_COL_TILE), lambda l, j: (l, 0, j)),
            pl.BlockSpec((1, 1, MOD_COL_TILE), lambda l, j: (l, 0, j)),
        ],
        out_specs=pl.BlockSpec((1, rows, MOD_COL_TILE), lambda l, j: (l, 0, j)),
        out_shape=jax.ShapeDtypeStruct((DEPTH, rows, n_cols), F32),
        compiler_params=_params("arbitrary", "arbitrary"),
        name="modulation",
    )(cond, w_mod, b_mod.reshape(DEPTH, 1, n_cols))


def _ffn_kernel(x_ref, mod_ref, nw_ref, up_ref, down_ref, fw_ref, o_ref, *, mod_row, final_norm):
    x = x_ref[0]
    mod = mod_ref[0]
    shift = mod[mod_row:mod_row + 1]
    scale = mod[mod_row + 1:mod_row + 2]
    gate = mod[mod_row + 2:mod_row + 3]
    h = _modulated_norm(x, nw_ref[...], shift, scale).astype(BF16)
    acc = jnp.zeros(x.shape, F32)
    for j in range(D_FF // FF_CHUNK):
        lo = j * FF_CHUNK
        a = _dot(h, up_ref[0, :, lo:lo + FF_CHUNK])
        b = _dot(h, up_ref[0, :, D_FF + lo:D_FF + lo + FF_CHUNK])
        act = (_silu(a) * b).astype(BF16)
        acc = acc + _dot(act, down_ref[0, lo:lo + FF_CHUNK, :])
    y = x + (0.5 * gate) * acc
    if final_norm:
        y = _rms(y) * fw_ref[...]
    o_ref[0] = y


def _ffn(x, mod, nw, up, down, fw, l, *, mod_row, final_norm):
    b, t, d = x.shape
    tm = _token_tile(t, FFN_TOKEN_TILE)
    kern = functools.partial(_ffn_kernel, mod_row=mod_row, final_norm=final_norm)
    return pl.pallas_call(
        kern,
        grid=(b, t // tm),
        in_specs=[
            pl.BlockSpec((1, tm, d), lambda b, i: (b, i, 0)),
            pl.BlockSpec((1, N_MOD, d), _per_batch_or_shared(mod)),
            _resident((1, d)),
            _layer_resident(up, l),
            _layer_resident(down, l),
            _resident((1, d)),
        ],
        out_specs=pl.BlockSpec((1, tm, d), lambda b, i: (b, i, 0)),
        out_shape=jax.ShapeDtypeStruct(x.shape, F32),
        compiler_params=_params("parallel", "parallel"),
        name="ffn",
    )(x, mod, nw, up, down, fw)


_PROJ_WEIGHTS = ('w_k', 'w_v', 'w_a', 'w_q', 'w_r', 'w_u', 'w_gp', 'w_gg', 'w_gate')


def _proj_kernel(x_ref, mod_ref, nw_ref, wk_ref, wv_ref, wa_ref, wq_ref, wr_ref, wu_ref, wgp_ref, wgg_ref,
                 wgate_ref, bgate_ref,
                 qf_ref, kef_ref, decf_ref, qb_ref, keb_ref, decb_ref, vst_ref, oi_ref,
                 r_ref, u_ref, gp_ref, gg_ref):
    x = x_ref[0]
    mod = mod_ref[0]
    tm = x.shape[0]
    chunks = [slice(c * CHUNK, (c + 1) * CHUNK) for c in range(tm // CHUNK)]
    h = _modulated_norm(x, nw_ref[...], mod[3:4], mod[4:5]).astype(BF16)
    k_all = _dot(h, wk_ref[0])
    q_all = _dot(h, wq_ref[0]) * (GLA_DK ** -0.5)
    v_all = _dot(h, wv_ref[0])
    a_low = _dot(h, wa_ref[0]).astype(BF16)
    z = _dot(a_low, wgate_ref[0]) + bgate_ref[...]
    u_ref[0] = _dot(h, wu_ref[0])
    gp_raw = _dot(h, wgp_ref[0])
    g_all = (jnp.minimum(z, 0.0) - jnp.log(1.0 + jnp.exp(-jnp.abs(z)))) / GATE_TAU

    gg_raw = _dot(h, wgg_ref[0])
    b_f = _chunk_cumsum(g_all[:, :GLA_KW], reverse=False)
    b_b = _chunk_cumsum(g_all[:, GLA_KW:], reverse=True)
    r_raw = _dot(h, wr_ref[0])

    q_f, q_b, k_f, k_b = [], [], [], []
    for c, rows in enumerate(chunks):
        q, k, bf, bb = q_all[rows], k_all[rows], b_f[rows], b_b[rows]
        edge_f = bf[CHUNK - 1:CHUNK]
        edge_b = bb[0:1]
        q_f.append((q * jnp.exp(bf)).astype(BF16))
        q_b.append((q * jnp.exp(bb)).astype(BF16))
        k_f.append((k * jnp.exp(-bf)).astype(BF16))
        k_b.append((k * jnp.exp(-bb)).astype(BF16))
        qf_ref[0, rows, :] = q_f[c]
        qb_ref[0, rows, :] = q_b[c]
        kef_ref[0, rows, :] = (k * jnp.exp(edge_f - bf)).astype(BF16)
        keb_ref[0, rows, :] = (k * jnp.exp(edge_b - bb)).astype(BF16)
        decf_ref[0, c:c + 1, :] = jnp.exp(edge_f)
        decb_ref[0, c:c + 1, :] = jnp.exp(edge_b)

    gp_ref[0] = _sigmoid(gp_raw).astype(BF16)
    arow = lax.broadcasted_iota(jnp.int32, (CHUNK, GLA_KW), 0)
    acol = lax.broadcasted_iota(jnp.int32, (CHUNK, GLA_KW), 1) % CHUNK
    causal = acol <= arow
    att = [jnp.where(causal, _dot_nt(q_f[c], _head_blocks(k_f[c])), _dot_nt(q_b[c], _head_blocks(k_b[c])))
           .astype(BF16) for c in range(len(chunks))]
    gg_ref[0] = _sigmoid(gg_raw).astype(BF16)

    v_head = lax.broadcasted_iota(jnp.int32, (CHUNK, GLA_VW), 1) // GLA_DV
    zero = jnp.zeros((), BF16)
    for c, rows in enumerate(chunks):
        v_bf = v_all[rows].astype(BF16)
        v_blk = jnp.concatenate([jnp.where(v_head == hd, v_bf, zero) for hd in range(GLA_HEADS)], axis=0)
        oi_ref[0, rows, :] = _dot(att[c], v_blk)
    r_ref[0] = _silu(r_raw).astype(BF16)

    for c, rows in enumerate(chunks):
        v = v_all[rows]
        v_stack = jnp.concatenate([v[:, hd * GLA_DV:(hd + 1) * GLA_DV] for hd in range(GLA_HEADS)], axis=0)
        vst_ref[0, c] = v_stack.T.astype(BF16)


def _proj(x, mod, nw, lw, l):
    b, t, d = x.shape
    tm = _token_tile(t)
    nc, tc = t // CHUNK, tm // CHUNK
    tok = lambda w: pl.BlockSpec((1, tm, w), lambda b, i: (b, i, 0))
    dec = pl.BlockSpec((1, tc, GLA_KW), lambda b, i: (b, i, 0))
    outs = [
        ('qf', (b, t, GLA_KW), BF16, tok(GLA_KW)), ('kef', (b, t, GLA_KW), BF16, tok(GLA_KW)),
        ('decf', (b, nc, GLA_KW), F32, dec),
        ('qb', (b, t, GLA_KW), BF16, tok(GLA_KW)), ('keb', (b, t, GLA_KW), BF16, tok(GLA_KW)),
        ('decb', (b, nc, GLA_KW), F32, dec),
        ('vst', (b, nc, GLA_DV, GLA_KW), BF16, pl.BlockSpec((1, tc, GLA_DV, GLA_KW), lambda b, i: (b, i, 0, 0))),
        ('oi', (b, t, GLA_VW), F32, tok(GLA_VW)),
        ('r', (b, t, GLA_VW), BF16, tok(GLA_VW)), ('u', (b, t, POOL_WIDTH), F32, tok(POOL_WIDTH)),
        ('gp', (b, t, d), BF16, tok(d)), ('gg', (b, t, d), BF16, tok(d)),
    ]
    weights = [lw[n] for n in _PROJ_WEIGHTS]
    res = pl.pallas_call(
        _proj_kernel,
        grid=(b, t // tm),
        in_specs=[tok(d), pl.BlockSpec((1, N_MOD, d), _per_batch_or_shared(mod)), _resident((1, d))]
        + [_layer_resident(w, l) for w in weights] + [_resident(lw['b_gate'].shape)],
        out_specs=[o[3] for o in outs],
        out_shape=[jax.ShapeDtypeStruct(o[1], o[2]) for o in outs],
        compiler_params=_params("parallel", "parallel"),
        name="mixer_proj",
    )(x, mod, nw, *weights, lw['b_gate'])
    return {o[0]: a for o, a in zip(outs, res)}


def _gla_kernel(qf_ref, kef_ref, decf_ref, qb_ref, keb_ref, decb_ref, vst_ref, oi_ref, r_ref,
                sf0_ref, sb0_ref, nw_ref, og_ref, sf_ref, sb_ref, st_ref, ob_ref, *, n_blocks, n_chunks):
    step = pl.program_id(1)
    tb = n_chunks * CHUNK

    def chunk(q_ref, ke_ref, dec_ref, c):
        rows = pl.ds(c * CHUNK, CHUNK)
        st = st_ref[...]
        o_stack = _dot_nt(_head_blocks(q_ref[0, rows, :]), st.astype(BF16))
        st_ref[...] = st * dec_ref[0, c:c + 1, :] + _dot(vst_ref[0, c], _head_blocks(ke_ref[0, rows, :]))
        return jnp.concatenate([o_stack[h * CHUNK:(h + 1) * CHUNK] for h in range(GLA_HEADS)], axis=1)

    @pl.when(step == 0)
    def _():
        st_ref[...] = sb0_ref[0]

    @pl.when(step < n_blocks)
    def _():
        base = (n_blocks - 1 - step) * tb
        for c in reversed(range(n_chunks)):
            o = chunk(qb_ref, keb_ref, decb_ref, c)
            ob_ref[pl.ds(pl.multiple_of(base + c * CHUNK, CHUNK), CHUNK), :] = o

    @pl.when(step == n_blocks - 1)
    def _():
        sb_ref[0] = st_ref[...]

    @pl.when(step == n_blocks)
    def _():
        st_ref[...] = sf0_ref[0]

    @pl.when(step >= n_blocks)
    def _():
        base = (step - n_blocks) * tb
        nw = nw_ref[...]
        for c in range(n_chunks):
            rows = pl.ds(c * CHUNK, CHUNK)
            o = chunk(qf_ref, kef_ref, decf_ref, c)
            o = o + ob_ref[pl.ds(pl.multiple_of(base + c * CHUNK, CHUNK), CHUNK), :] + oi_ref[0, rows, :]
            o = jnp.concatenate(
                [_rms(o[:, h * GLA_DV:(h + 1) * GLA_DV]) * nw for h in range(GLA_HEADS)], axis=1)
            og_ref[0, rows, :] = (o * r_ref[0, rows, :].astype(F32)).astype(og_ref.dtype)

    @pl.when(step == 2 * n_blocks - 1)
    def _():
        sf_ref[0] = st_ref[...]


def _gla(p, r, s_f0, s_b0, norm_w):
    b, t, _ = p['qf'].shape
    tb = _token_tile(t, SCAN_TOKEN_TILE)
    nb = t // tb
    tc = tb // CHUNK
    fwd = lambda b, s: (b, jnp.maximum(s - nb, 0), 0)
    rev = lambda b, s: (b, nb - 1 - jnp.minimum(s, nb - 1), 0)
    both = lambda b, s: (b, jnp.where(s < nb, nb - 1 - s, s - nb), 0, 0)
    tok = lambda w, m: pl.BlockSpec((1, tb, w), m)
    dec = lambda m: pl.BlockSpec((1, tc, GLA_KW), m)
    state = pl.BlockSpec((1, GLA_DV, GLA_KW), lambda b, s: (b, 0, 0))
    kern = functools.partial(_gla_kernel, n_blocks=nb, n_chunks=tc)
    return pl.pallas_call(
        kern,
        grid=(b, 2 * nb),
        in_specs=[tok(GLA_KW, fwd), tok(GLA_KW, fwd), dec(fwd), tok(GLA_KW, rev), tok(GLA_KW, rev), dec(rev),
                  pl.BlockSpec((1, tc, GLA_DV, GLA_KW), both), tok(GLA_VW, fwd), tok(GLA_VW, fwd),
                  state, state, _resident((1, GLA_DV))],
        out_specs=[tok(GLA_VW, fwd), state, state],
        out_shape=[jax.ShapeDtypeStruct((b, t, GLA_VW), BF16),
                   jax.ShapeDtypeStruct((b, GLA_DV, GLA_KW), F32),
                   jax.ShapeDtypeStruct((b, GLA_DV, GLA_KW), F32)],
        scratch_shapes=[pltpu.VMEM((GLA_DV, GLA_KW), F32), pltpu.VMEM((t, GLA_VW), F32)],
        compiler_params=_params("arbitrary", "arbitrary"),
        name="gla_scan",
    )(p['qf'], p['kef'], p['decf'], p['qb'], p['keb'], p['decb'], p['vst'], p['oi'], r, s_f0, s_b0, norm_w)


def _box_mean(u, idx, seg, unit, w):
    n = u.shape[0]
    half = w // 2

    def ahead(x, s):
        return jnp.where(idx + s < seg, pltpu.roll(x, (n - s * unit) % n, axis=0), 0.0)

    def behind(x, s):
        return jnp.where(idx - s >= 0, pltpu.roll(x, (s * unit) % n, axis=0), 0.0)

    fwd = u
    span = 1
    while span < half:
        fwd = fwd + ahead(fwd, span)
        span *= 2
    bwd = behind(u, 1)
    span = 1
    while span < half:
        bwd = bwd + behind(bwd, span)
        span *= 2
    cnt = (jnp.minimum(idx + half, seg) - jnp.maximum(idx - half, 0)).astype(F32)
    return (fwd + bwd) / cnt


def _pool_kernel(u_ref, w_ref, scale_ref, o_ref, *, n_rows, n_cols):
    group = pl.program_id(1)
    n = n_rows * n_cols
    tok = lax.broadcasted_iota(jnp.int32, (n, POOL_GROUP), 0)
    for gi, w in enumerate(POOL_WINDOWS):
        @pl.when(group == gi)
        def _(w=w):
            u = u_ref[0]
            m = u
            if n_rows > 1:
                m = _box_mean(m, tok // n_cols, n_rows, n_cols, w)
            m = _box_mean(m, tok % n_cols, n_cols, 1, w)
            y = _dot((m - u).astype(BF16), w_ref[0, 0]) * scale_ref[0]
            o_ref[0] = y.astype(o_ref.dtype)


def _pool(u, pool_w, pool_scale, l, *, n_rows, n_cols):
    b, t, _ = u.shape
    kern = functools.partial(_pool_kernel, n_rows=n_rows, n_cols=n_cols)
    return pl.pallas_call(
        kern,
        grid=(b, N_POOL_GROUPS),
        in_specs=[
            pl.BlockSpec((1, t, POOL_GROUP), lambda b, g: (b, 0, g)),
            pl.BlockSpec((1, 1, POOL_GROUP, POOL_GROUP), lambda b, g: (l, g, 0, 0)),
            pl.BlockSpec((1, 1, POOL_GROUP), lambda b, g: (g, 0, 0)),
        ],
        out_specs=pl.BlockSpec((1, t, POOL_GROUP), lambda b, g: (b, 0, g)),
        out_shape=jax.ShapeDtypeStruct((b, t, POOL_WIDTH), BF16),
        compiler_params=_params("parallel", "parallel"),
        name="pool_mix",
    )(u, pool_w, pool_scale)


def _merge_kernel(x_ref, mod_ref, og_ref, yp_ref, gp_ref, gg_ref, wbg_ref, wbp_ref, wo_ref, o_ref):
    y_gla = _dot(og_ref[0], wbg_ref[0])
    y_pool = _dot(yp_ref[0], wbp_ref[0])
    merged = gp_ref[0].astype(F32) * y_pool + gg_ref[0].astype(F32) * y_gla
    gate = mod_ref[0][5:6]
    o_ref[0] = x_ref[0] + gate * _dot(merged.astype(BF16), wo_ref[0])


def _merge(x, mod, og, yp, gp, gg, lw, l):
    b, t, d = x.shape
    tm = _token_tile(t)
    tok = lambda w: pl.BlockSpec((1, tm, w), lambda b, i: (b, i, 0))
    weights = [lw['w_br_gla'], lw['w_br_pool'], lw['w_out']]
    return pl.pallas_call(
        _merge_kernel,
        grid=(b, t // tm),
        in_specs=[tok(d), pl.BlockSpec((1, N_MOD, d), _per_batch_or_shared(mod)),
                  tok(GLA_VW), tok(POOL_WIDTH), tok(d), tok(d)] + [_layer_resident(w, l) for w in weights],
        out_specs=tok(d),
        out_shape=jax.ShapeDtypeStruct(x.shape, F32),
        compiler_params=_params("parallel", "parallel"),
        name="mixer_merge",
    )(x, mod, og, yp, gp, gg, *weights)


def _mixer_weights(w_in, w_af_up, w_ab_up, pool_w, w_br_pool, w_br_gla, w_out):
    cols = lambda lo, hi: w_in[:, :, lo:hi].astype(BF16)
    zeros = jnp.zeros((DEPTH, GATE_RANK, GLA_KW), F32)
    w_gate = jnp.concatenate([jnp.concatenate([w_af_up, zeros], axis=2),
                              jnp.concatenate([zeros, w_ab_up], axis=2)], axis=1)
    return {
        'w_k': cols(OFF_K, OFF_V), 'w_v': cols(OFF_V, OFF_AF), 'w_a': cols(OFF_AF, OFF_Q),
        'w_q': cols(OFF_Q, OFF_R), 'w_r': cols(OFF_R, OFF_POOL), 'w_u': cols(OFF_POOL, OFF_GP),
        'w_gp': cols(OFF_GP, OFF_GG), 'w_gg': cols(OFF_GG, IN_COLS),
        'w_gate': w_gate.astype(BF16),
        'pool_w': pool_w.astype(BF16),
        'w_br_pool': w_br_pool.astype(BF16), 'w_br_gla': w_br_gla.astype(BF16), 'w_out': w_out.astype(BF16),
    }


def _mixer(x, mod, nw, lw, l, s_f0, s_b0, gla_norm_w, *, n_rows, n_cols, states_only=False):
    p = _proj(x, mod, nw, lw, l)
    og, s_f, s_b = _gla(p, p['r'], s_f0, s_b0, gla_norm_w)
    if states_only:
        return None, s_f, s_b
    yp = _pool(p['u'], lw['pool_w'], lw['pool_scale'], l, n_rows=n_rows, n_cols=n_cols)
    return _merge(x, mod, og, yp, p['gp'], p['gg'], lw, l), s_f, s_b


def kernel(x, c, ctx, c_ctx, w_mod, b_mod, norm_w, ffn1_up, ffn1_down, w_in, w_af_up, b_af, w_ab_up, b_ab,
           gla_norm_w, pool_w, pool_scale, w_br_pool, w_br_gla, w_out, ffn2_up, ffn2_down, final_norm_w):
    batch, seq, d = x.shape
    ctx_len = ctx.shape[1]
    cond = jnp.concatenate([c, c_ctx[None, :], jnp.zeros((SUBLANES - batch - 1, d), F32)], axis=0)
    mod_all = _modulation(cond, w_mod, b_mod)
    zero_state = jnp.zeros((batch, GLA_DV, GLA_KW), F32)
    fw = final_norm_w[None, :]
    up1, down1 = ffn1_up.astype(BF16), ffn1_down.astype(BF16)
    up2, down2 = ffn2_up.astype(BF16), ffn2_down.astype(BF16)
    lw = _mixer_weights(w_in, w_af_up, w_ab_up, pool_w, w_br_pool, w_br_gla, w_out)
    for l in range(DEPTH):
        last = l == DEPTH - 1
        mod_x = mod_all[l, :batch].reshape(batch, N_MOD, d)
        mod_c = mod_all[l, batch:batch + 1].reshape(1, N_MOD, d)
        nw = norm_w[l][:, None, :]
        lw['b_gate'] = jnp.concatenate([b_af[l], b_ab[l]])[None, :]
        lw['pool_scale'] = pool_scale[l].reshape(N_POOL_GROUPS, 1, POOL_GROUP)
        gnw = gla_norm_w[l][None, :]

        x = _ffn(x, mod_x, nw[0], up1, down1, fw, l, mod_row=0, final_norm=False)
        ctx = _ffn(ctx, mod_c, nw[0], up1, down1, fw, l, mod_row=0, final_norm=False)

        ctx_mixed, s_f, s_b = _mixer(ctx, mod_c, nw[1], lw, l, zero_state, zero_state, gnw,
                                     n_rows=1, n_cols=ctx_len, states_only=last)
        if not last:
            ctx = _ffn(ctx_mixed, mod_c, nw[2], up2, down2, fw, l, mod_row=6, final_norm=False)

        x, _, _ = _mixer(x, mod_x, nw[1], lw, l, s_f, s_b, gnw, n_rows=seq // GRID_W, n_cols=GRID_W)
        x = _ffn(x, mod_x, nw[2], up2, down2, fw, l, mod_row=6, final_norm=last)
    return x
```

```python
import functools

import jax
import jax.numpy as jnp
from jax import lax
from jax.experimental import pallas as pl
from jax.experimental.pallas import tpu as pltpu

F32 = jnp.float32
BF16 = jnp.bfloat16

D_MODEL = 1024
DEPTH = 2
GRID_W = 64
N_POOL_GROUPS = 4
POOL_GROUP = 128
POOL_WIDTH = N_POOL_GROUPS * POOL_GROUP
POOL_WINDOWS = (2, 4, 8, 16)
GLA_HEADS = 4
GLA_DK = 64
GLA_DV = 128
GLA_KW = GLA_HEADS * GLA_DK
GLA_VW = GLA_HEADS * GLA_DV
GATE_RANK = 16
GATE_TAU = 16.0
CHUNK = 64
D_FF = 2816
N_MOD = 9
EPS = 1e-6

OFF_K = 0
OFF_V = OFF_K + GLA_KW
OFF_AF = OFF_V + GLA_VW
OFF_AB = OFF_AF + GATE_RANK
OFF_Q = OFF_AB + GATE_RANK
OFF_R = OFF_Q + GLA_KW
OFF_POOL = OFF_R + GLA_VW
OFF_GP = OFF_POOL + POOL_WIDTH
OFF_GG = OFF_GP + D_MODEL
IN_COLS = OFF_GG + D_MODEL

V7X_VMEM_BYTES = 64 * 1024 * 1024
VMEM_LIMIT_BYTES = 56 * 1024 * 1024
SUBLANES = 8
TOKEN_TILE = 512
FFN_TOKEN_TILE = 1024
SCAN_TOKEN_TILE = 1024
FF_CHUNK = 256
MOD_COL_TILE = 1152


def _resident(shape):
    return pl.BlockSpec(shape, lambda *_: (0,) * len(shape), pipeline_mode=pl.Buffered(1))


def _layer_resident(arr, l):
    tail = (0,) * (arr.ndim - 1)
    return pl.BlockSpec((1,) + arr.shape[1:], lambda *_: (l,) + tail, pipeline_mode=pl.Buffered(1))


def _params(*semantics):
    return pltpu.CompilerParams(dimension_semantics=semantics, vmem_limit_bytes=VMEM_LIMIT_BYTES)


def _per_batch_or_shared(arr):
    zeros = (0,) * (arr.ndim - 1)
    if arr.shape[0] == 1:
        return lambda b, i: (0,) + zeros
    return lambda b, i: (b,) + zeros


def _token_tile(t, tile=TOKEN_TILE):
    return min(tile, t)


def _rms(x):
    return x * lax.rsqrt(jnp.mean(x * x, axis=-1, keepdims=True) + EPS)


def _modulated_norm(x, w, shift, scale):
    return (_rms(x) * w) * (1.0 + scale) + shift


def _silu(a):
    return a / (1.0 + jnp.exp(-a))


def _sigmoid(a):
    return 1.0 / (1.0 + jnp.exp(-a))


def _dot(a, b):
    return jnp.dot(a, b, preferred_element_type=F32)


def _dot_nt(a, b):
    return lax.dot_general(a, b, (((1,), (1,)), ((), ())), preferred_element_type=F32)


def _head_blocks(x):
    head = lax.broadcasted_iota(jnp.int32, x.shape, 1) // GLA_DK
    zero = jnp.zeros((), x.dtype)
    return jnp.concatenate([jnp.where(head == h, x, zero) for h in range(GLA_HEADS)], axis=0)


def _chunk_cumsum(g, *, reverse):
    n = g.shape[0]
    pos = lax.broadcasted_iota(jnp.int32, g.shape, 0) % CHUNK
    acc = g
    span = 1
    while span < CHUNK:
        if reverse:
            acc = acc + jnp.where(pos + span < CHUNK, pltpu.roll(acc, n - span, axis=0), 0.0)
        else:
            acc = acc + jnp.where(pos >= span, pltpu.roll(acc, span, axis=0), 0.0)
        span *= 2
    return acc


def _mod_kernel(cond_ref, w_ref, b_ref, o_ref):
    cond = cond_ref[...]
    o_ref[0] = _dot(_silu(cond).astype(BF16), w_ref[0].astype(BF16)) + b_ref[0]


def _modulation(cond, w_mod, b_mod):
    n_cols = N_MOD * D_MODEL
    rows = cond.shape[0]
    return pl.pallas_call(
        _mod_kernel,
        grid=(DEPTH, n_cols // MOD_COL_TILE),
        in_specs=[
            pl.BlockSpec((rows, D_MODEL), lambda l, j: (0, 0)),
            pl.BlockSpec((1, D_MODEL, MOD_COL_TILE), lambda l, j: (l, 0, j)),
            pl.BlockSpec((1, 1, MOD_COL_TILE), lambda l, j: (l, 0, j)),
        ],
        out_specs=pl.BlockSpec((1, rows, MOD_COL_TILE), lambda l, j: (l, 0, j)),
        out_shape=jax.ShapeDtypeStruct((DEPTH, rows, n_cols), F32),
        compiler_params=_params("arbitrary", "arbitrary"),
        name="modulation",
    )(cond, w_mod, b_mod.reshape(DEPTH, 1, n_cols))


def _ffn_kernel(x_ref, mod_ref, nw_ref, up_ref, down_ref, fw_ref, o_ref, *, mod_row, final_norm):
    x = x_ref[0]
    mod = mod_ref[0]
    shift = mod[mod_row:mod_row + 1]
    scale = mod[mod_row + 1:mod_row + 2]
    gate = mod[mod_row + 2:mod_row + 3]
    h = _modulated_norm(x, nw_ref[...], shift, scale).astype(BF16)
    acc = jnp.zeros(x.shape, F32)
    for j in range(D_FF // FF_CHUNK):
        lo = j * FF_CHUNK
        a = _dot(h, up_ref[0, :, lo:lo + FF_CHUNK])
        b = _dot(h, up_ref[0, :, D_FF + lo:D_FF + lo + FF_CHUNK])
        act = (_silu(a) * b).astype(BF16)
        acc = acc + _dot(act, down_ref[0, lo:lo + FF_CHUNK, :])
    y = x + (0.5 * gate) * acc
    if final_norm:
        y = _rms(y) * fw_ref[...]
    o_ref[0] = y


def _ffn(x, mod, nw, up, down, fw, l, *, mod_row, final_norm):
    b, t, d = x.shape
    tm = _token_tile(t, FFN_TOKEN_TILE)
    kern = functools.partial(_ffn_kernel, mod_row=mod_row, final_norm=final_norm)
    return pl.pallas_call(
        kern,
        grid=(b, t // tm),
        in_specs=[
            pl.BlockSpec((1, tm, d), lambda b, i: (b, i, 0)),
            pl.BlockSpec((1, N_MOD, d), _per_batch_or_shared(mod)),
            _resident((1, d)),
            _layer_resident(up, l),
            _layer_resident(down, l),
            _resident((1, d)),
        ],
        out_specs=pl.BlockSpec((1, tm, d), lambda b, i: (b, i, 0)),
        out_shape=jax.ShapeDtypeStruct(x.shape, F32),
        compiler_params=_params("parallel", "parallel"),
        name="ffn",
    )(x, mod, nw, up, down, fw)


_PROJ_WEIGHTS = ('w_k', 'w_v', 'w_a', 'w_q', 'w_r', 'w_u', 'w_gp', 'w_gg', 'w_gate')


def _proj_kernel(x_ref, mod_ref, nw_ref, wk_ref, wv_ref, wa_ref, wq_ref, wr_ref, wu_ref, wgp_ref, wgg_ref,
                 wgate_ref, bgate_ref,
                 qf_ref, kef_ref, decf_ref, qb_ref, keb_ref, decb_ref, vst_ref, oi_ref,
                 r_ref, u_ref, gp_ref, gg_ref):
    x = x_ref[0]
    mod = mod_ref[0]
    tm = x.shape[0]
    chunks = [slice(c * CHUNK, (c + 1) * CHUNK) for c in range(tm // CHUNK)]
    h = _modulated_norm(x, nw_ref[...], mod[3:4], mod[4:5]).astype(BF16)
    k_all = _dot(h, wk_ref[0])
    q_all = _dot(h, wq_ref[0]) * (GLA_DK ** -0.5)
    v_all = _dot(h, wv_ref[0])
    a_low = _dot(h, wa_ref[0]).astype(BF16)
    z = _dot(a_low, wgate_ref[0]) + bgate_ref[...]
    u_ref[0] = _dot(h, wu_ref[0])
    gp_raw = _dot(h, wgp_ref[0])
    g_all = (jnp.minimum(z, 0.0) - jnp.log(1.0 + jnp.exp(-jnp.abs(z)))) / GATE_TAU

    gg_raw = _dot(h, wgg_ref[0])
    b_f = _chunk_cumsum(g_all[:, :GLA_KW], reverse=False)
    b_b = _chunk_cumsum(g_all[:, GLA_KW:], reverse=True)
    r_raw = _dot(h, wr_ref[0])

    q_f, q_b, k_f, k_b = [], [], [], []
    for c, rows in enumerate(chunks):
        q, k, bf, bb = q_all[rows], k_all[rows], b_f[rows], b_b[rows]
        edge_f = bf[CHUNK - 1:CHUNK]
        edge_b = bb[0:1]
        q_f.append((q * jnp.exp(bf)).astype(BF16))
        q_b.append((q * jnp.exp(bb)).astype(BF16))
        k_f.append((k * jnp.exp(-bf)).astype(BF16))
        k_b.append((k * jnp.exp(-bb)).astype(BF16))
        qf_ref[0, rows, :] = q_f[c]
        qb_ref[0, rows, :] = q_b[c]
        kef_ref[0, rows, :] = (k * jnp.exp(edge_f - bf)).astype(BF16)
        keb_ref[0, rows, :] = (k * jnp.exp(edge_b - bb)).astype(BF16)
        decf_ref[0, c:c + 1, :] = jnp.exp(edge_f)
        decb_ref[0, c:c + 1, :] = jnp.exp(edge_b)

    gp_ref[0] = _sigmoid(gp_raw).astype(BF16)
    arow = lax.broadcasted_iota(jnp.int32, (CHUNK, GLA_KW), 0)
    acol = lax.broadcasted_iota(jnp.int32, (CHUNK, GLA_KW), 1) % CHUNK
    causal = acol <= arow
    att = [jnp.where(causal, _dot_nt(q_f[c], _head_blocks(k_f[c])), _dot_nt(q_b[c], _head_blocks(k_b[c])))
           .astype(BF16) for c in range(len(chunks))]
    gg_ref[0] = _sigmoid(gg_raw).astype(BF16)

    v_head = lax.broadcasted_iota(jnp.int32, (CHUNK, GLA_VW), 1) // GLA_DV
    zero = jnp.zeros((), BF16)
    for c, rows in enumerate(chunks):
        v_bf = v_all[rows].astype(BF16)
        v_blk = jnp.concatenate([jnp.where(v_head == hd, v_bf, zero) for hd in range(GLA_HEADS)], axis=0)
        oi_ref[0, rows, :] = _dot(att[c], v_blk)
    r_ref[0] = _silu(r_raw).astype(BF16)

    for c, rows in enumerate(chunks):
        v = v_all[rows]
        v_stack = jnp.concatenate([v[:, hd * GLA_DV:(hd + 1) * GLA_DV] for hd in range(GLA_HEADS)], axis=0)
        vst_ref[0, c] = v_stack.T.astype(BF16)


def _proj(x, mod, nw, lw, l):
    b, t, d = x.shape
    tm = _token_tile(t)
    nc, tc = t // CHUNK, tm // CHUNK
    tok = lambda w: pl.BlockSpec((1, tm, w), lambda b, i: (b, i, 0))
    dec = pl.BlockSpec((1, tc, GLA_KW), lambda b, i: (b, i, 0))
    outs = [
        ('qf', (b, t, GLA_KW), BF16, tok(GLA_KW)), ('kef', (b, t, GLA_KW), BF16, tok(GLA_KW)),
        ('decf', (b, nc, GLA_KW), F32, dec),
        ('qb', (b, t, GLA_KW), BF16, tok(GLA_KW)), ('keb', (b, t, GLA_KW), BF16, tok(GLA_KW)),
        ('decb', (b, nc, GLA_KW), F32, dec),
        ('vst', (b, nc, GLA_DV, GLA_KW), BF16, pl.BlockSpec((1, tc, GLA_DV, GLA_KW), lambda b, i: (b, i, 0, 0))),
        ('oi', (b, t, GLA_VW), F32, tok(GLA_VW)),
        ('r', (b, t, GLA_VW), BF16, tok(GLA_VW)), ('u', (b, t, POOL_WIDTH), F32, tok(POOL_WIDTH)),
        ('gp', (b, t, d), BF16, tok(d)), ('gg', (b, t, d), BF16, tok(d)),
    ]
    weights = [lw[n] for n in _PROJ_WEIGHTS]
    res = pl.pallas_call(
        _proj_kernel,
        grid=(b, t // tm),
        in_specs=[tok(d), pl.BlockSpec((1, N_MOD, d), _per_batch_or_shared(mod)), _resident((1, d))]
        + [_layer_resident(w, l) for w in weights] + [_resident(lw['b_gate'].shape)],
        out_specs=[o[3] for o in outs],
        out_shape=[jax.ShapeDtypeStruct(o[1], o[2]) for o in outs],
        compiler_params=_params("parallel", "parallel"),
        name="mixer_proj",
    )(x, mod, nw, *weights, lw['b_gate'])
    return {o[0]: a for o, a in zip(outs, res)}


def _gla_kernel(qf_ref, kef_ref, decf_ref, qb_ref, keb_ref, decb_ref, vst_ref, oi_ref, r_ref,
                sf0_ref, sb0_ref, nw_ref, og_ref, sf_ref, sb_ref, st_ref, ob_ref, *, n_blocks, n_chunks):
    step = pl.program_id(1)
    tb = n_chunks * CHUNK

    def chunk(q_ref, ke_ref, dec_ref, c):
        rows = pl.ds(c * CHUNK, CHUNK)
        st = st_ref[...]
        o_stack = _dot_nt(_head_blocks(q_ref[0, rows, :]), st.astype(BF16))
        st_ref[...] = st * dec_ref[0, c:c + 1, :] + _dot(vst_ref[0, c], _head_blocks(ke_ref[0, rows, :]))
        return jnp.concatenate([o_stack[h * CHUNK:(h + 1) * CHUNK] for h in range(GLA_HEADS)], axis=1)

    @pl.when(step == 0)
    def _():
        st_ref[...] = sb0_ref[0]

    @pl.when(step < n_blocks)
    def _():
        base = (n_blocks - 1 - step) * tb
        for c in reversed(range(n_chunks)):
            o = chunk(qb_ref, keb_ref, decb_ref, c)
            ob_ref[pl.ds(pl.multiple_of(base + c * CHUNK, CHUNK), CHUNK), :] = o

    @pl.when(step == n_blocks - 1)
    def _():
        sb_ref[0] = st_ref[...]

    @pl.when(step == n_blocks)
    def _():
        st_ref[...] = sf0_ref[0]

    @pl.when(step >= n_blocks)
    def _():
        base = (step - n_blocks) * tb
        nw = nw_ref[...]
        for c in range(n_chunks):
            rows = pl.ds(c * CHUNK, CHUNK)
            o = chunk(qf_ref, kef_ref, decf_ref, c)
            o = o + ob_ref[pl.ds(pl.multiple_of(base + c * CHUNK, CHUNK), CHUNK), :] + oi_ref[0, rows, :]
            o = jnp.concatenate(
                [_rms(o[:, h * GLA_DV:(h + 1) * GLA_DV]) * nw for h in range(GLA_HEADS)], axis=1)
            og_ref[0, rows, :] = (o * r_ref[0, rows, :].astype(F32)).astype(og_ref.dtype)

    @pl.when(step == 2 * n_blocks - 1)
    def _():
        sf_ref[0] = st_ref[...]


def _gla(p, r, s_f0, s_b0, norm_w):
    b, t, _ = p['qf'].shape
    tb = _token_tile(t, SCAN_TOKEN_TILE)
    nb = t // tb
    tc = tb // CHUNK
    fwd = lambda b, s: (b, jnp.maximum(s - nb, 0), 0)
    rev = lambda b, s: (b, nb - 1 - jnp.minimum(s, nb - 1), 0)
    both = lambda b, s: (b, jnp.where(s < nb, nb - 1 - s, s - nb), 0, 0)
    tok = lambda w, m: pl.BlockSpec((1, tb, w), m)
    dec = lambda m: pl.BlockSpec((1, tc, GLA_KW), m)
    state = pl.BlockSpec((1, GLA_DV, GLA_KW), lambda b, s: (b, 0, 0))
    kern = functools.partial(_gla_kernel, n_blocks=nb, n_chunks=tc)
    return pl.pallas_call(
        kern,
        grid=(b, 2 * nb),
        in_specs=[tok(GLA_KW, fwd), tok(GLA_KW, fwd), dec(fwd), tok(GLA_KW, rev), tok(GLA_KW, rev), dec(rev),
                  pl.BlockSpec((1, tc, GLA_DV, GLA_KW), both), tok(GLA_VW, fwd), tok(GLA_VW, fwd),
                  state, state, _resident((1, GLA_DV))],
        out_specs=[tok(GLA_VW, fwd), state, state],
        out_shape=[jax.ShapeDtypeStruct((b, t, GLA_VW), BF16),
                   jax.ShapeDtypeStruct((b, GLA_DV, GLA_KW), F32),
                   jax.ShapeDtypeStruct((b, GLA_DV, GLA_KW), F32)],
        scratch_shapes=[pltpu.VMEM((GLA_DV, GLA_KW), F32), pltpu.VMEM((t, GLA_VW), F32)],
        compiler_params=_params("arbitrary", "arbitrary"),
        name="gla_scan",
    )(p['qf'], p['kef'], p['decf'], p['qb'], p['keb'], p['decb'], p['vst'], p['oi'], r, s_f0, s_b0, norm_w)


def _col_box_sum(x, n_cols, w):
    half = w // 2
    c_out = lax.broadcasted_iota(jnp.int32, (n_cols, 2 * n_cols), 0)
    c_in = lax.broadcasted_iota(jnp.int32, (n_cols, 2 * n_cols), 1) % n_cols
    band = jnp.where((c_in >= c_out - half) & (c_in < c_out + half), 1.0, 0.0).astype(BF16)
    hi = x.astype(BF16)
    lo = (x - hi.astype(F32)).astype(BF16)
    out = []
    for r in range(x.shape[0] // n_cols):
        rows = slice(r * n_cols, (r + 1) * n_cols)
        out.append(_dot(band, jnp.concatenate([hi[rows], lo[rows]], axis=0)))
    return jnp.concatenate(out, axis=0)


def _row_box_sum(u_ref, lanes, tile, n_tiles, tm, n_cols, w):
    half = w // 2
    halo = half * n_cols
    assert halo <= tm and tm % n_cols == 0
    cur = u_ref[0, pl.ds(pl.multiple_of(tile * tm, tm), tm), lanes]
    before = jnp.maximum(tile - 1, 0) * tm + (tm - halo)
    after = jnp.minimum(tile + 1, n_tiles - 1) * tm
    prev = u_ref[0, pl.ds(pl.multiple_of(before, n_cols), halo), lanes]
    nxt = u_ref[0, pl.ds(pl.multiple_of(after, n_cols), halo), lanes]
    win = jnp.concatenate([jnp.where(tile > 0, prev, 0.0), cur, jnp.where(tile < n_tiles - 1, nxt, 0.0)],
                          axis=0)
    span = 1
    while span < w:
        win = win[:-span * n_cols] + win[span * n_cols:]
        span *= 2
    return win[:tm]


def _merge_kernel(x_ref, mod_ref, og_ref, u_ref, gp_ref, gg_ref, wbg_ref, wbp_ref, wo_ref, pw_ref, ps_ref,
                  o_ref, *, n_rows, n_cols, n_tiles):
    tile = pl.program_id(1)
    tm = x_ref.shape[1]
    y_gla = _dot(og_ref[0], wbg_ref[0])
    tok = lax.broadcasted_iota(jnp.int32, (tm, POOL_GROUP), 0)
    col = (tok % n_cols).astype(F32)
    row = (tile * (tm // n_cols) + tok // n_cols).astype(F32)
    mixed = []
    for g, w in enumerate(POOL_WINDOWS):
        half = float(w // 2)
        lanes = slice(g * POOL_GROUP, (g + 1) * POOL_GROUP)
        u = u_ref[0, pl.ds(pl.multiple_of(tile * tm, tm), tm), lanes]
        cnt = jnp.minimum(col, half) + jnp.minimum(n_cols - col, half)
        m = u
        if n_rows > 1:
            cnt = cnt * (jnp.minimum(row, half) + jnp.minimum(n_rows - row, half))
            m = _row_box_sum(u_ref, lanes, tile, n_tiles, tm, n_cols, w)
        m = _col_box_sum(m, n_cols, w) / cnt
        mixed.append((_dot((m - u).astype(BF16), pw_ref[0, g]) * ps_ref[:, lanes]).astype(BF16))
    y_pool = _dot(jnp.concatenate(mixed, axis=1), wbp_ref[0])
    merged = gp_ref[0].astype(F32) * y_pool + gg_ref[0].astype(F32) * y_gla
    gate = mod_ref[0][5:6]
    o_ref[0] = x_ref[0] + gate * _dot(merged.astype(BF16), wo_ref[0])


def _merge(x, mod, og, u, gp, gg, lw, l, *, n_rows, n_cols):
    b, t, d = x.shape
    tm = _token_tile(t)
    tok = lambda w: pl.BlockSpec((1, tm, w), lambda b, i: (b, i, 0))
    whole = pl.BlockSpec((1, t, POOL_WIDTH), lambda b, i: (b, 0, 0), pipeline_mode=pl.Buffered(1))
    weights = [lw['w_br_gla'], lw['w_br_pool'], lw['w_out'], lw['pool_w']]
    kern = functools.partial(_merge_kernel, n_rows=n_rows, n_cols=n_cols, n_tiles=t // tm)
    return pl.pallas_call(
        kern,
        grid=(b, t // tm),
        in_specs=[tok(d), pl.BlockSpec((1, N_MOD, d), _per_batch_or_shared(mod)),
                  tok(GLA_VW), whole, tok(d), tok(d)] + [_layer_resident(w, l) for w in weights]
        + [_resident(lw['pool_scale'].shape)],
        out_specs=tok(d),
        out_shape=jax.ShapeDtypeStruct(x.shape, F32),
        compiler_params=_params("parallel", "parallel"),
        name="mixer_merge",
    )(x, mod, og, u, gp, gg, *weights, lw['pool_scale'])


def _mixer_weights(w_in, w_af_up, w_ab_up, pool_w, w_br_pool, w_br_gla, w_out):
    cols = lambda lo, hi: w_in[:, :, lo:hi].astype(BF16)
    zeros = jnp.zeros((DEPTH, GATE_RANK, GLA_KW), F32)
    w_gate = jnp.concatenate([jnp.concatenate([w_af_up, zeros], axis=2),
                              jnp.concatenate([zeros, w_ab_up], axis=2)], axis=1)
    return {
        'w_k': cols(OFF_K, OFF_V), 'w_v': cols(OFF_V, OFF_AF), 'w_a': cols(OFF_AF, OFF_Q),
        'w_q': cols(OFF_Q, OFF_R), 'w_r': cols(OFF_R, OFF_POOL), 'w_u': cols(OFF_POOL, OFF_GP),
        'w_gp': cols(OFF_GP, OFF_GG), 'w_gg': cols(OFF_GG, IN_COLS),
        'w_gate': w_gate.astype(BF16),
        'pool_w': pool_w.astype(BF16),
        'w_br_pool': w_br_pool.astype(BF16), 'w_br_gla': w_br_gla.astype(BF16), 'w_out': w_out.astype(BF16),
    }


def _mixer(x, mod, nw, lw, l, s_f0, s_b0, gla_norm_w, *, n_rows, n_cols, states_only=False):
    p = _proj(x, mod, nw, lw, l)
    og, s_f, s_b = _gla(p, p['r'], s_f0, s_b0, gla_norm_w)
    if states_only:
        return None, s_f, s_b
    return _merge(x, mod, og, p['u'], p['gp'], p['gg'], lw, l, n_rows=n_rows, n_cols=n_cols), s_f, s_b


def kernel(x, c, ctx, c_ctx, w_mod, b_mod, norm_w, ffn1_up, ffn1_down, w_in, w_af_up, b_af, w_ab_up, b_ab,
           gla_norm_w, pool_w, pool_scale, w_br_pool, w_br_gla, w_out, ffn2_up, ffn2_down, final_norm_w):
    batch, seq, d = x.shape
    ctx_len = ctx.shape[1]
    cond = jnp.concatenate([c, c_ctx[None, :], jnp.zeros((SUBLANES - batch - 1, d), F32)], axis=0)
    mod_all = _modulation(cond, w_mod, b_mod)
    zero_state = jnp.zeros((batch, GLA_DV, GLA_KW), F32)
    fw = final_norm_w[None, :]
    up1, down1 = ffn1_up.astype(BF16), ffn1_down.astype(BF16)
    up2, down2 = ffn2_up.astype(BF16), ffn2_down.astype(BF16)
    lw = _mixer_weights(w_in, w_af_up, w_ab_up, pool_w, w_br_pool, w_br_gla, w_out)
    for l in range(DEPTH):
        last = l == DEPTH - 1
        mod_x = mod_all[l, :batch].reshape(batch, N_MOD, d)
        mod_c = mod_all[l, batch:batch + 1].reshape(1, N_MOD, d)
        nw = norm_w[l][:, None, :]
        lw['b_gate'] = jnp.concatenate([b_af[l], b_ab[l]])[None, :]
        lw['pool_scale'] = pool_scale[l][None, :]
        gnw = gla_norm_w[l][None, :]

        x = _ffn(x, mod_x, nw[0], up1, down1, fw, l, mod_row=0, final_norm=False)
        ctx = _ffn(ctx, mod_c, nw[0], up1, down1, fw, l, mod_row=0, final_norm=False)

        ctx_mixed, s_f, s_b = _mixer(ctx, mod_c, nw[1], lw, l, zero_state, zero_state, gnw,
                                     n_rows=1, n_cols=ctx_len, states_only=last)
        if not last:
            ctx = _ffn(ctx_mixed, mod_c, nw[2], up2, down2, fw, l, mod_row=6, final_norm=False)

        x, _, _ = _mixer(x, mod_x, nw[1], lw, l, s_f, s_b, gnw, n_rows=seq // GRID_W, n_cols=GRID_W)
        x = _ffn(x, mod_x, nw[2], up2, down2, fw, l, mod_row=6, final_norm=last)
    return x
```

```python
import functools

import jax
import jax.numpy as jnp
from jax import lax
from jax.experimental import pallas as pl
from jax.experimental.pallas import tpu as pltpu

F32 = jnp.float32
BF16 = jnp.bfloat16

D_MODEL = 1024
DEPTH = 2
GRID_W = 64
N_POOL_GROUPS = 4
POOL_GROUP = 128
POOL_WIDTH = N_POOL_GROUPS * POOL_GROUP
POOL_WINDOWS = (2, 4, 8, 16)
GLA_HEADS = 4
GLA_DK = 64
GLA_DV = 128
GLA_KW = GLA_HEADS * GLA_DK
GLA_VW = GLA_HEADS * GLA_DV
GATE_RANK = 16
GATE_TAU = 16.0
CHUNK = 64
D_FF = 2816
N_MOD = 9
EPS = 1e-6

OFF_K = 0
OFF_V = OFF_K + GLA_KW
OFF_AF = OFF_V + GLA_VW
OFF_AB = OFF_AF + GATE_RANK
OFF_Q = OFF_AB + GATE_RANK
OFF_R = OFF_Q + GLA_KW
OFF_POOL = OFF_R + GLA_VW
OFF_GP = OFF_POOL + POOL_WIDTH
OFF_GG = OFF_GP + D_MODEL
IN_COLS = OFF_GG + D_MODEL

V7X_VMEM_BYTES = 64 * 1024 * 1024
VMEM_LIMIT_BYTES = 56 * 1024 * 1024
SUBLANES = 8
TOKEN_TILE = 512
FFN_TOKEN_TILE = 1024
SCAN_TOKEN_TILE = 1024
FF_CHUNK = 256
MOD_COL_TILE = 1152


def _resident(shape):
    return pl.BlockSpec(shape, lambda *_: (0,) * len(shape), pipeline_mode=pl.Buffered(1))


def _layer_resident(arr, l):
    tail = (0,) * (arr.ndim - 1)
    return pl.BlockSpec((1,) + arr.shape[1:], lambda *_: (l,) + tail, pipeline_mode=pl.Buffered(1))


def _params(*semantics):
    return pltpu.CompilerParams(dimension_semantics=semantics, vmem_limit_bytes=VMEM_LIMIT_BYTES)


def _per_batch_or_shared(arr):
    zeros = (0,) * (arr.ndim - 1)
    if arr.shape[0] == 1:
        return lambda b, i: (0,) + zeros
    return lambda b, i: (b,) + zeros


def _token_tile(t, tile=TOKEN_TILE):
    return min(tile, t)


def _rms(x):
    return x * lax.rsqrt(jnp.mean(x * x, axis=-1, keepdims=True) + EPS)


def _modulated_norm(x, w, shift, scale):
    return (_rms(x) * w) * (1.0 + scale) + shift


def _silu(a):
    return a / (1.0 + jnp.exp(-a))


def _sigmoid(a):
    return 1.0 / (1.0 + jnp.exp(-a))


def _dot(a, b):
    return jnp.dot(a, b, preferred_element_type=F32)


def _dot_nt(a, b):
    return lax.dot_general(a, b, (((1,), (1,)), ((), ())), preferred_element_type=F32)


def _head_blocks(x):
    head = lax.broadcasted_iota(jnp.int32, x.shape, 1) // GLA_DK
    zero = jnp.zeros((), x.dtype)
    return jnp.concatenate([jnp.where(head == h, x, zero) for h in range(GLA_HEADS)], axis=0)


def _chunk_cumsum(g, *, reverse):
    n = g.shape[0]
    pos = lax.broadcasted_iota(jnp.int32, g.shape, 0) % CHUNK
    acc = g
    span = 1
    while span < CHUNK:
        if reverse:
            acc = acc + jnp.where(pos + span < CHUNK, pltpu.roll(acc, n - span, axis=0), 0.0)
        else:
            acc = acc + jnp.where(pos >= span, pltpu.roll(acc, span, axis=0), 0.0)
        span *= 2
    return acc


def _mod_kernel(cond_ref, w_ref, b_ref, o_ref):
    cond = cond_ref[...]
    o_ref[0] = _dot(_silu(cond).astype(BF16), w_ref[0].astype(BF16)) + b_ref[0]


def _modulation(cond, w_mod, b_mod):
    n_cols = N_MOD * D_MODEL
    rows = cond.shape[0]
    return pl.pallas_call(
        _mod_kernel,
        grid=(DEPTH, n_cols // MOD_COL_TILE),
        in_specs=[
            pl.BlockSpec((rows, D_MODEL), lambda l, j: (0, 0)),
            pl.BlockSpec((1, D_MODEL, MOD_COL_TILE), lambda l, j: (l, 0, j)),
            pl.BlockSpec((1, 1, MOD_COL_TILE), lambda l, j: (l, 0, j)),
        ],
        out_specs=pl.BlockSpec((1, rows, MOD_COL_TILE), lambda l, j: (l, 0, j)),
        out_shape=jax.ShapeDtypeStruct((DEPTH, rows, n_cols), F32),
        compiler_params=_params("arbitrary", "arbitrary"),
        name="modulation",
    )(cond, w_mod, b_mod.reshape(DEPTH, 1, n_cols))


def _ffn_kernel(x_ref, mod_ref, nw_ref, up_ref, down_ref, fw_ref, o_ref, act_ref, *, mod_row, final_norm):
    x = x_ref[0]
    mod = mod_ref[0]
    shift = mod[mod_row:mod_row + 1]
    scale = mod[mod_row + 1:mod_row + 2]
    gate = mod[mod_row + 2:mod_row + 3]
    h = _modulated_norm(x, nw_ref[...], shift, scale).astype(BF16)
    for j in range(D_FF // FF_CHUNK):
        lo = j * FF_CHUNK
        a = _dot(h, up_ref[0, :, lo:lo + FF_CHUNK])
        b = _dot(h, up_ref[0, :, D_FF + lo:D_FF + lo + FF_CHUNK])
        act_ref[:, lo:lo + FF_CHUNK] = (_silu(a) * b).astype(BF16)
    y = x + (0.5 * gate) * _dot(act_ref[...], down_ref[0])
    if final_norm:
        y = _rms(y) * fw_ref[...]
    o_ref[0] = y


def _ffn(x, mod, nw, up, down, fw, l, *, mod_row, final_norm):
    b, t, d = x.shape
    tm = _token_tile(t, FFN_TOKEN_TILE)
    kern = functools.partial(_ffn_kernel, mod_row=mod_row, final_norm=final_norm)
    return pl.pallas_call(
        kern,
        grid=(b, t // tm),
        in_specs=[
            pl.BlockSpec((1, tm, d), lambda b, i: (b, i, 0)),
            pl.BlockSpec((1, N_MOD, d), _per_batch_or_shared(mod)),
            _resident((1, d)),
            _layer_resident(up, l),
            _layer_resident(down, l),
            _resident((1, d)),
        ],
        out_specs=pl.BlockSpec((1, tm, d), lambda b, i: (b, i, 0)),
        out_shape=jax.ShapeDtypeStruct(x.shape, F32),
        scratch_shapes=[pltpu.VMEM((tm, D_FF), BF16)],
        compiler_params=_params("parallel", "parallel"),
        name="ffn",
    )(x, mod, nw, up, down, fw)


_PROJ_WEIGHTS = ('w_k', 'w_v', 'w_a', 'w_q', 'w_r', 'w_u', 'w_gp', 'w_gg', 'w_gate')


def _proj_kernel(x_ref, mod_ref, nw_ref, wk_ref, wv_ref, wa_ref, wq_ref, wr_ref, wu_ref, wgp_ref, wgg_ref,
                 wgate_ref, bgate_ref,
                 qf_ref, kef_ref, decf_ref, qb_ref, keb_ref, decb_ref, vst_ref, oi_ref,
                 r_ref, u_ref, gp_ref, gg_ref):
    x = x_ref[0]
    mod = mod_ref[0]
    tm = x.shape[0]
    chunks = [slice(c * CHUNK, (c + 1) * CHUNK) for c in range(tm // CHUNK)]
    h = _modulated_norm(x, nw_ref[...], mod[3:4], mod[4:5]).astype(BF16)
    k_all = _dot(h, wk_ref[0])
    q_all = _dot(h, wq_ref[0]) * (GLA_DK ** -0.5)
    v_all = _dot(h, wv_ref[0])
    a_low = _dot(h, wa_ref[0]).astype(BF16)
    z = _dot(a_low, wgate_ref[0]) + bgate_ref[...]
    u_ref[0] = _dot(h, wu_ref[0])
    gp_raw = _dot(h, wgp_ref[0])
    g_all = (jnp.minimum(z, 0.0) - jnp.log(1.0 + jnp.exp(-jnp.abs(z)))) / GATE_TAU

    gg_raw = _dot(h, wgg_ref[0])
    b_f = _chunk_cumsum(g_all[:, :GLA_KW], reverse=False)
    b_b = _chunk_cumsum(g_all[:, GLA_KW:], reverse=True)
    r_raw = _dot(h, wr_ref[0])

    q_f, q_b, k_f, k_b = [], [], [], []
    for c, rows in enumerate(chunks):
        q, k, bf, bb = q_all[rows], k_all[rows], b_f[rows], b_b[rows]
        edge_f = bf[CHUNK - 1:CHUNK]
        edge_b = bb[0:1]
        q_f.append((q * jnp.exp(bf)).astype(BF16))
        q_b.append((q * jnp.exp(bb)).astype(BF16))
        k_f.append((k * jnp.exp(-bf)).astype(BF16))
        k_b.append((k * jnp.exp(-bb)).astype(BF16))
        qf_ref[0, rows, :] = q_f[c]
        qb_ref[0, rows, :] = q_b[c]
        kef_ref[0, rows, :] = (k * jnp.exp(edge_f - bf)).astype(BF16)
        keb_ref[0, rows, :] = (k * jnp.exp(edge_b - bb)).astype(BF16)
        decf_ref[0, c:c + 1, :] = jnp.exp(edge_f)
        decb_ref[0, c:c + 1, :] = jnp.exp(edge_b)

    gp_ref[0] = _sigmoid(gp_raw).astype(BF16)
    arow = lax.broadcasted_iota(jnp.int32, (CHUNK, GLA_KW), 0)
    acol = lax.broadcasted_iota(jnp.int32, (CHUNK, GLA_KW), 1) % CHUNK
    causal = acol <= arow
    att = [jnp.where(causal, _dot_nt(q_f[c], _head_blocks(k_f[c])), _dot_nt(q_b[c], _head_blocks(k_b[c])))
           .astype(BF16) for c in range(len(chunks))]
    gg_ref[0] = _sigmoid(gg_raw).astype(BF16)

    v_head = lax.broadcasted_iota(jnp.int32, (CHUNK, GLA_VW), 1) // GLA_DV
    zero = jnp.zeros((), BF16)
    for c, rows in enumerate(chunks):
        v_bf = v_all[rows].astype(BF16)
        v_blk = jnp.concatenate([jnp.where(v_head == hd, v_bf, zero) for hd in range(GLA_HEADS)], axis=0)
        oi_ref[0, rows, :] = _dot(att[c], v_blk)
    r_ref[0] = _silu(r_raw).astype(BF16)

    for c, rows in enumerate(chunks):
        v = v_all[rows]
        v_stack = jnp.concatenate([v[:, hd * GLA_DV:(hd + 1) * GLA_DV] for hd in range(GLA_HEADS)], axis=0)
        vst_ref[0, c] = v_stack.T.astype(BF16)


def _proj(x, mod, nw, lw, l):
    b, t, d = x.shape
    tm = _token_tile(t)
    nc, tc = t // CHUNK, tm // CHUNK
    tok = lambda w: pl.BlockSpec((1, tm, w), lambda b, i: (b, i, 0))
    dec = pl.BlockSpec((1, tc, GLA_KW), lambda b, i: (b, i, 0))
    outs = [
        ('qf', (b, t, GLA_KW), BF16, tok(GLA_KW)), ('kef', (b, t, GLA_KW), BF16, tok(GLA_KW)),
        ('decf', (b, nc, GLA_KW), F32, dec),
        ('qb', (b, t, GLA_KW), BF16, tok(GLA_KW)), ('keb', (b, t, GLA_KW), BF16, tok(GLA_KW)),
        ('decb', (b, nc, GLA_KW), F32, dec),
        ('vst', (b, nc, GLA_DV, GLA_KW), BF16, pl.BlockSpec((1, tc, GLA_DV, GLA_KW), lambda b, i: (b, i, 0, 0))),
        ('oi', (b, t, GLA_VW), F32, tok(GLA_VW)),
        ('r', (b, t, GLA_VW), BF16, tok(GLA_VW)), ('u', (b, t, POOL_WIDTH), F32, tok(POOL_WIDTH)),
        ('gp', (b, t, d), BF16, tok(d)), ('gg', (b, t, d), BF16, tok(d)),
    ]
    weights = [lw[n] for n in _PROJ_WEIGHTS]
    res = pl.pallas_call(
        _proj_kernel,
        grid=(b, t // tm),
        in_specs=[tok(d), pl.BlockSpec((1, N_MOD, d), _per_batch_or_shared(mod)), _resident((1, d))]
        + [_layer_resident(w, l) for w in weights] + [_resident(lw['b_gate'].shape)],
        out_specs=[o[3] for o in outs],
        out_shape=[jax.ShapeDtypeStruct(o[1], o[2]) for o in outs],
        compiler_params=_params("parallel", "parallel"),
        name="mixer_proj",
    )(x, mod, nw, *weights, lw['b_gate'])
    return {o[0]: a for o, a in zip(outs, res)}


def _gla_kernel(qf_ref, kef_ref, decf_ref, qb_ref, keb_ref, decb_ref, vst_ref, oi_ref, r_ref,
                sf0_ref, sb0_ref, nw_ref, og_ref, sf_ref, sb_ref, st_ref, ob_ref, *, n_blocks, n_chunks):
    step = pl.program_id(1)
    tb = n_chunks * CHUNK

    def chunk(q_ref, ke_ref, dec_ref, c):
        rows = pl.ds(c * CHUNK, CHUNK)
        st = st_ref[...]
        o_stack = _dot_nt(_head_blocks(q_ref[0, rows, :]), st.astype(BF16))
        st_ref[...] = st * dec_ref[0, c:c + 1, :] + _dot(vst_ref[0, c], _head_blocks(ke_ref[0, rows, :]))
        return jnp.concatenate([o_stack[h * CHUNK:(h + 1) * CHUNK] for h in range(GLA_HEADS)], axis=1)

    @pl.when(step == 0)
    def _():
        st_ref[...] = sb0_ref[0]

    @pl.when(step < n_blocks)
    def _():
        base = (n_blocks - 1 - step) * tb
        for c in reversed(range(n_chunks)):
            o = chunk(qb_ref, keb_ref, decb_ref, c)
            ob_ref[pl.ds(pl.multiple_of(base + c * CHUNK, CHUNK), CHUNK), :] = o

    @pl.when(step == n_blocks - 1)
    def _():
        sb_ref[0] = st_ref[...]

    @pl.when(step == n_blocks)
    def _():
        st_ref[...] = sf0_ref[0]

    @pl.when(step >= n_blocks)
    def _():
        base = (step - n_blocks) * tb
        nw = nw_ref[...]
        for c in range(n_chunks):
            rows = pl.ds(c * CHUNK, CHUNK)
            o = chunk(qf_ref, kef_ref, decf_ref, c)
            o = o + ob_ref[pl.ds(pl.multiple_of(base + c * CHUNK, CHUNK), CHUNK), :] + oi_ref[0, rows, :]
            o = jnp.concatenate(
                [_rms(o[:, h * GLA_DV:(h + 1) * GLA_DV]) * nw for h in range(GLA_HEADS)], axis=1)
            og_ref[0, rows, :] = (o * r_ref[0, rows, :].astype(F32)).astype(og_ref.dtype)

    @pl.when(step == 2 * n_blocks - 1)
    def _():
        sf_ref[0] = st_ref[...]


def _gla(p, r, s_f0, s_b0, norm_w):
    b, t, _ = p['qf'].shape
    tb = _token_tile(t, SCAN_TOKEN_TILE)
    nb = t // tb
    tc = tb // CHUNK
    fwd = lambda b, s: (b, jnp.maximum(s - nb, 0), 0)
    rev = lambda b, s: (b, nb - 1 - jnp.minimum(s, nb - 1), 0)
    both = lambda b, s: (b, jnp.where(s < nb, nb - 1 - s, s - nb), 0, 0)
    tok = lambda w, m: pl.BlockSpec((1, tb, w), m)
    dec = lambda m: pl.BlockSpec((1, tc, GLA_KW), m)
    state = pl.BlockSpec((1, GLA_DV, GLA_KW), lambda b, s: (b, 0, 0))
    kern = functools.partial(_gla_kernel, n_blocks=nb, n_chunks=tc)
    return pl.pallas_call(
        kern,
        grid=(b, 2 * nb),
        in_specs=[tok(GLA_KW, fwd), tok(GLA_KW, fwd), dec(fwd), tok(GLA_KW, rev), tok(GLA_KW, rev), dec(rev),
                  pl.BlockSpec((1, tc, GLA_DV, GLA_KW), both), tok(GLA_VW, fwd), tok(GLA_VW, fwd),
                  state, state, _resident((1, GLA_DV))],
        out_specs=[tok(GLA_VW, fwd), state, state],
        out_shape=[jax.ShapeDtypeStruct((b, t, GLA_VW), BF16),
                   jax.ShapeDtypeStruct((b, GLA_DV, GLA_KW), F32),
                   jax.ShapeDtypeStruct((b, GLA_DV, GLA_KW), F32)],
        scratch_shapes=[pltpu.VMEM((GLA_DV, GLA_KW), F32), pltpu.VMEM((t, GLA_VW), F32)],
        compiler_params=_params("arbitrary", "arbitrary"),
        name="gla_scan",
    )(p['qf'], p['kef'], p['decf'], p['qb'], p['keb'], p['decb'], p['vst'], p['oi'], r, s_f0, s_b0, norm_w)


def _col_box_sum(x, n_cols, w):
    half = w // 2
    c_out = lax.broadcasted_iota(jnp.int32, (n_cols, 2 * n_cols), 0)
    c_in = lax.broadcasted_iota(jnp.int32, (n_cols, 2 * n_cols), 1) % n_cols
    band = jnp.where((c_in >= c_out - half) & (c_in < c_out + half), 1.0, 0.0).astype(BF16)
    hi = x.astype(BF16)
    lo = (x - hi.astype(F32)).astype(BF16)
    out = []
    for r in range(x.shape[0] // n_cols):
        rows = slice(r * n_cols, (r + 1) * n_cols)
        out.append(_dot(band, jnp.concatenate([hi[rows], lo[rows]], axis=0)))
    return jnp.concatenate(out, axis=0)


def _row_box_sum(u_ref, lanes, tile, n_tiles, tm, n_cols, w):
    half = w // 2
    halo = half * n_cols
    assert halo <= tm and tm % n_cols == 0
    cur = u_ref[0, pl.ds(pl.multiple_of(tile * tm, tm), tm), lanes]
    before = jnp.maximum(tile - 1, 0) * tm + (tm - halo)
    after = jnp.minimum(tile + 1, n_tiles - 1) * tm
    prev = u_ref[0, pl.ds(pl.multiple_of(before, n_cols), halo), lanes]
    nxt = u_ref[0, pl.ds(pl.multiple_of(after, n_cols), halo), lanes]
    win = jnp.concatenate([jnp.where(tile > 0, prev, 0.0), cur, jnp.where(tile < n_tiles - 1, nxt, 0.0)],
                          axis=0)
    span = 1
    while span < w:
        win = win[:-span * n_cols] + win[span * n_cols:]
        span *= 2
    return win[:tm]


def _merge_kernel(x_ref, mod_ref, og_ref, u_ref, gp_ref, gg_ref, wbg_ref, wbp_ref, wo_ref, pw_ref, ps_ref,
                  o_ref, *, n_rows, n_cols, n_tiles):
    tile = pl.program_id(1)
    tm = x_ref.shape[1]
    y_gla = _dot(og_ref[0], wbg_ref[0])
    tok = lax.broadcasted_iota(jnp.int32, (tm, POOL_GROUP), 0)
    col = (tok % n_cols).astype(F32)
    row = (tile * (tm // n_cols) + tok // n_cols).astype(F32)
    mixed = []
    for g, w in enumerate(POOL_WINDOWS):
        half = float(w // 2)
        lanes = slice(g * POOL_GROUP, (g + 1) * POOL_GROUP)
        u = u_ref[0, pl.ds(pl.multiple_of(tile * tm, tm), tm), lanes]
        cnt = jnp.minimum(col, half) + jnp.minimum(n_cols - col, half)
        m = u
        if n_rows > 1:
            cnt = cnt * (jnp.minimum(row, half) + jnp.minimum(n_rows - row, half))
            m = _row_box_sum(u_ref, lanes, tile, n_tiles, tm, n_cols, w)
        m = _col_box_sum(m, n_cols, w) / cnt
        mixed.append((_dot((m - u).astype(BF16), pw_ref[0, g]) * ps_ref[:, lanes]).astype(BF16))
    y_pool = _dot(jnp.concatenate(mixed, axis=1), wbp_ref[0])
    merged = gp_ref[0].astype(F32) * y_pool + gg_ref[0].astype(F32) * y_gla
    gate = mod_ref[0][5:6]
    o_ref[0] = x_ref[0] + gate * _dot(merged.astype(BF16), wo_ref[0])


def _merge(x, mod, og, u, gp, gg, lw, l, *, n_rows, n_cols):
    b, t, d = x.shape
    tm = _token_tile(t)
    tok = lambda w: pl.BlockSpec((1, tm, w), lambda b, i: (b, i, 0))
    whole = pl.BlockSpec((1, t, POOL_WIDTH), lambda b, i: (b, 0, 0), pipeline_mode=pl.Buffered(1))
    weights = [lw['w_br_gla'], lw['w_br_pool'], lw['w_out'], lw['pool_w']]
    kern = functools.partial(_merge_kernel, n_rows=n_rows, n_cols=n_cols, n_tiles=t // tm)
    return pl.pallas_call(
        kern,
        grid=(b, t // tm),
        in_specs=[tok(d), pl.BlockSpec((1, N_MOD, d), _per_batch_or_shared(mod)),
                  tok(GLA_VW), whole, tok(d), tok(d)] + [_layer_resident(w, l) for w in weights]
        + [_resident(lw['pool_scale'].shape)],
        out_specs=tok(d),
        out_shape=jax.ShapeDtypeStruct(x.shape, F32),
        compiler_params=_params("parallel", "parallel"),
        name="mixer_merge",
    )(x, mod, og, u, gp, gg, *weights, lw['pool_scale'])


def _mixer_weights(w_in, w_af_up, w_ab_up, pool_w, w_br_pool, w_br_gla, w_out):
    cols = lambda lo, hi: w_in[:, :, lo:hi].astype(BF16)
    zeros = jnp.zeros((DEPTH, GATE_RANK, GLA_KW), F32)
    w_gate = jnp.concatenate([jnp.concatenate([w_af_up, zeros], axis=2),
                              jnp.concatenate([zeros, w_ab_up], axis=2)], axis=1)
    return {
        'w_k': cols(OFF_K, OFF_V), 'w_v': cols(OFF_V, OFF_AF), 'w_a': cols(OFF_AF, OFF_Q),
        'w_q': cols(OFF_Q, OFF_R), 'w_r': cols(OFF_R, OFF_POOL), 'w_u': cols(OFF_POOL, OFF_GP),
        'w_gp': cols(OFF_GP, OFF_GG), 'w_gg': cols(OFF_GG, IN_COLS),
        'w_gate': w_gate.astype(BF16),
        'pool_w': pool_w.astype(BF16),
        'w_br_pool': w_br_pool.astype(BF16), 'w_br_gla': w_br_gla.astype(BF16), 'w_out': w_out.astype(BF16),
    }


def _mixer(x, mod, nw, lw, l, s_f0, s_b0, gla_norm_w, *, n_rows, n_cols, states_only=False):
    p = _proj(x, mod, nw, lw, l)
    og, s_f, s_b = _gla(p, p['r'], s_f0, s_b0, gla_norm_w)
    if states_only:
        return None, s_f, s_b
    return _merge(x, mod, og, p['u'], p['gp'], p['gg'], lw, l, n_rows=n_rows, n_cols=n_cols), s_f, s_b


def kernel(x, c, ctx, c_ctx, w_mod, b_mod, norm_w, ffn1_up, ffn1_down, w_in, w_af_up, b_af, w_ab_up, b_ab,
           gla_norm_w, pool_w, pool_scale, w_br_pool, w_br_gla, w_out, ffn2_up, ffn2_down, final_norm_w):
    batch, seq, d = x.shape
    ctx_len = ctx.shape[1]
    cond = jnp.concatenate([c, c_ctx[None, :], jnp.zeros((SUBLANES - batch - 1, d), F32)], axis=0)
    mod_all = _modulation(cond, w_mod, b_mod)
    zero_state = jnp.zeros((batch, GLA_DV, GLA_KW), F32)
    fw = final_norm_w[None, :]
    up1, down1 = ffn1_up.astype(BF16), ffn1_down.astype(BF16)
    up2, down2 = ffn2_up.astype(BF16), ffn2_down.astype(BF16)
    lw = _mixer_weights(w_in, w_af_up, w_ab_up, pool_w, w_br_pool, w_br_gla, w_out)
    for l in range(DEPTH):
        last = l == DEPTH - 1
        mod_x = mod_all[l, :batch].reshape(batch, N_MOD, d)
        mod_c = mod_all[l, batch:batch + 1].reshape(1, N_MOD, d)
        nw = norm_w[l][:, None, :]
        lw['b_gate'] = jnp.concatenate([b_af[l], b_ab[l]])[None, :]
        lw['pool_scale'] = pool_scale[l][None, :]
        gnw = gla_norm_w[l][None, :]

        x = _ffn(x, mod_x, nw[0], up1, down1, fw, l, mod_row=0, final_norm=False)
        ctx = _ffn(ctx, mod_c, nw[0], up1, down1, fw, l, mod_row=0, final_norm=False)

        ctx_mixed, s_f, s_b = _mixer(ctx, mod_c, nw[1], lw, l, zero_state, zero_state, gnw,
                                     n_rows=1, n_cols=ctx_len, states_only=last)
        if not last:
            ctx = _ffn(ctx_mixed, mod_c, nw[2], up2, down2, fw, l, mod_row=6, final_norm=False)

        x, _, _ = _mixer(x, mod_x, nw[1], lw, l, s_f, s_b, gnw, n_rows=seq // GRID_W, n_cols=GRID_W)
        x = _ffn(x, mod_x, nw[2], up2, down2, fw, l, mod_row=6, final_norm=last)
    return x
```

```python
import functools

import jax
import jax.numpy as jnp
from jax import lax
from jax.experimental import pallas as pl
from jax.experimental.pallas import tpu as pltpu

F32 = jnp.float32
BF16 = jnp.bfloat16

D_MODEL = 1024
DEPTH = 2
GRID_W = 64
N_POOL_GROUPS = 4
POOL_GROUP = 128
POOL_WIDTH = N_POOL_GROUPS * POOL_GROUP
POOL_WINDOWS = (2, 4, 8, 16)
GLA_HEADS = 4
GLA_DK = 64
GLA_DV = 128
GLA_KW = GLA_HEADS * GLA_DK
GLA_VW = GLA_HEADS * GLA_DV
GATE_RANK = 16
GATE_TAU = 16.0
CHUNK = 64
D_FF = 2816
N_MOD = 9
EPS = 1e-6

OFF_K = 0
OFF_V = OFF_K + GLA_KW
OFF_AF = OFF_V + GLA_VW
OFF_AB = OFF_AF + GATE_RANK
OFF_Q = OFF_AB + GATE_RANK
OFF_R = OFF_Q + GLA_KW
OFF_POOL = OFF_R + GLA_VW
OFF_GP = OFF_POOL + POOL_WIDTH
OFF_GG = OFF_GP + D_MODEL
IN_COLS = OFF_GG + D_MODEL

V7X_VMEM_BYTES = 64 * 1024 * 1024
VMEM_LIMIT_BYTES = 56 * 1024 * 1024
SUBLANES = 8
TOKEN_TILE = 512
FFN_TOKEN_TILE = 1024
SCAN_TOKEN_TILE = 1024
FF_CHUNK = 256
MOD_COL_TILE = 1152


def _resident(shape):
    return pl.BlockSpec(shape, lambda *_: (0,) * len(shape), pipeline_mode=pl.Buffered(1))


def _params(*semantics):
    return pltpu.CompilerParams(dimension_semantics=semantics, vmem_limit_bytes=VMEM_LIMIT_BYTES)


def _per_batch_or_shared(arr):
    zeros = (0,) * (arr.ndim - 1)
    if arr.shape[0] == 1:
        return lambda b, i: (0,) + zeros
    return lambda b, i: (b,) + zeros


def _token_tile(t, tile=TOKEN_TILE):
    return min(tile, t)


def _rms(x):
    return x * lax.rsqrt(jnp.mean(x * x, axis=-1, keepdims=True) + EPS)


def _modulated_norm(x, w, shift, scale):
    return (_rms(x) * w) * (1.0 + scale) + shift


def _silu(a):
    return a / (1.0 + jnp.exp(-a))


def _sigmoid(a):
    return 1.0 / (1.0 + jnp.exp(-a))


def _dot(a, b):
    return jnp.dot(a, b, preferred_element_type=F32)


def _dot_nt(a, b):
    return lax.dot_general(a, b, (((1,), (1,)), ((), ())), preferred_element_type=F32)


def _head_blocks(x):
    head = lax.broadcasted_iota(jnp.int32, x.shape, 1) // GLA_DK
    zero = jnp.zeros((), x.dtype)
    return jnp.concatenate([jnp.where(head == h, x, zero) for h in range(GLA_HEADS)], axis=0)


def _chunk_cumsum(g, *, reverse):
    n = g.shape[0]
    pos = lax.broadcasted_iota(jnp.int32, g.shape, 0) % CHUNK
    acc = g
    span = 1
    while span < CHUNK:
        if reverse:
            acc = acc + jnp.where(pos + span < CHUNK, pltpu.roll(acc, n - span, axis=0), 0.0)
        else:
            acc = acc + jnp.where(pos >= span, pltpu.roll(acc, span, axis=0), 0.0)
        span *= 2
    return acc


def _mod_kernel(cond_ref, w_ref, b_ref, o_ref):
    cond = cond_ref[...]
    o_ref[0] = _dot(_silu(cond).astype(BF16), w_ref[0].astype(BF16)) + b_ref[0]


def _modulation(cond, w_mod, b_mod):
    n_cols = N_MOD * D_MODEL
    rows = cond.shape[0]
    return pl.pallas_call(
        _mod_kernel,
        grid=(DEPTH, n_cols // MOD_COL_TILE),
        in_specs=[
            pl.BlockSpec((rows, D_MODEL), lambda l, j: (0, 0)),
            pl.BlockSpec((1, D_MODEL, MOD_COL_TILE), lambda l, j: (l, 0, j)),
            pl.BlockSpec((1, 1, MOD_COL_TILE), lambda l, j: (l, 0, j)),
        ],
        out_specs=pl.BlockSpec((1, rows, MOD_COL_TILE), lambda l, j: (l, 0, j)),
        out_shape=jax.ShapeDtypeStruct((DEPTH, rows, n_cols), F32),
        compiler_params=_params("arbitrary", "arbitrary"),
        name="modulation",
    )(cond, w_mod, b_mod.reshape(DEPTH, 1, n_cols))


def _ffn_kernel(*refs, mod_row, final_norm, cast_cols):
    n_in = len(cast_cols)
    n_out = sum(len(cols) for cols in cast_cols)
    x_ref, mod_ref, nw_ref, up_ref, down_ref, fw_ref = refs[:6]
    cast_in = refs[6:6 + n_in]
    o_ref = refs[6 + n_in]
    cast_out = refs[7 + n_in:7 + n_in + n_out]
    act_ref = refs[7 + n_in + n_out]
    x = x_ref[0]
    mod = mod_ref[0]
    shift = mod[mod_row:mod_row + 1]
    scale = mod[mod_row + 1:mod_row + 2]
    gate = mod[mod_row + 2:mod_row + 3]
    h = _modulated_norm(x, nw_ref[...], shift, scale).astype(BF16)
    for j in range(D_FF // FF_CHUNK):
        lo = j * FF_CHUNK
        a = _dot(h, up_ref[:, lo:lo + FF_CHUNK])
        b = _dot(h, up_ref[:, D_FF + lo:D_FF + lo + FF_CHUNK])
        act_ref[:, lo:lo + FF_CHUNK] = (_silu(a) * b).astype(BF16)
    y = x + (0.5 * gate) * _dot(act_ref[...], down_ref[...])
    if final_norm:
        y = _rms(y) * fw_ref[...]
    o_ref[0] = y
    k = 0
    for src_ref, cols in zip(cast_in, cast_cols):
        w = src_ref[0]
        for lo, hi in cols:
            cast_out[k][...] = w[:, lo:hi].astype(BF16)
            k += 1


def _ffn(x, mod, nw, up, down, fw, *, mod_row, final_norm, cast=()):
    b, t, d = x.shape
    tm = _token_tile(t, FFN_TOKEN_TILE)
    nt = t // tm
    steps = b * nt
    cast = [(w, l, cols or ((0, w.shape[2]),)) for w, l, cols in cast]
    cast_specs, out_specs, out_shapes = [], [], []
    for w, l, cols in cast:
        rows = w.shape[1] // steps
        assert rows * steps == w.shape[1] and rows % (2 * SUBLANES) == 0, w.shape
        cast_specs.append(pl.BlockSpec((1, rows, w.shape[2]), lambda b, i, l=l: (l, b * nt + i, 0)))
        for lo, hi in cols:
            out_specs.append(pl.BlockSpec((rows, hi - lo), lambda b, i: (b * nt + i, 0)))
            out_shapes.append(jax.ShapeDtypeStruct((w.shape[1], hi - lo), BF16))
    kern = functools.partial(_ffn_kernel, mod_row=mod_row, final_norm=final_norm,
                             cast_cols=tuple(cols for _, _, cols in cast))
    res = pl.pallas_call(
        kern,
        grid=(b, nt),
        in_specs=[
            pl.BlockSpec((1, tm, d), lambda b, i: (b, i, 0)),
            pl.BlockSpec((1, N_MOD, d), _per_batch_or_shared(mod)),
            _resident((1, d)),
            _resident(up.shape),
            _resident(down.shape),
            _resident((1, d)),
        ] + cast_specs,
        out_specs=[pl.BlockSpec((1, tm, d), lambda b, i: (b, i, 0))] + out_specs,
        out_shape=[jax.ShapeDtypeStruct(x.shape, F32)] + out_shapes,
        scratch_shapes=[pltpu.VMEM((tm, D_FF), BF16)],
        compiler_params=_params("parallel", "parallel"),
        name="ffn",
    )(x, mod, nw, up, down, fw, *[w for w, _, _ in cast])
    return res[0], list(res[1:])


_PROJ_WEIGHTS = ('w_k', 'w_v', 'w_a', 'w_q', 'w_r', 'w_u', 'w_gp', 'w_gg', 'w_gate')


def _proj_kernel(x_ref, mod_ref, nw_ref, wk_ref, wv_ref, wa_ref, wq_ref, wr_ref, wu_ref, wgp_ref, wgg_ref,
                 wgate_ref, bgate_ref,
                 qf_ref, kef_ref, decf_ref, qb_ref, keb_ref, decb_ref, vst_ref, oi_ref,
                 r_ref, u_ref, gp_ref, gg_ref):
    x = x_ref[0]
    mod = mod_ref[0]
    tm = x.shape[0]
    chunks = [slice(c * CHUNK, (c + 1) * CHUNK) for c in range(tm // CHUNK)]
    h = _modulated_norm(x, nw_ref[...], mod[3:4], mod[4:5]).astype(BF16)
    k_all = _dot(h, wk_ref[...])
    q_all = _dot(h, wq_ref[...]) * (GLA_DK ** -0.5)
    v_all = _dot(h, wv_ref[...])
    a_low = _dot(h, wa_ref[...]).astype(BF16)
    z = _dot(a_low, wgate_ref[...]) + bgate_ref[...]
    u_ref[0] = _dot(h, wu_ref[...])
    gp_raw = _dot(h, wgp_ref[...])
    g_all = (jnp.minimum(z, 0.0) - jnp.log(1.0 + jnp.exp(-jnp.abs(z)))) / GATE_TAU

    gg_raw = _dot(h, wgg_ref[...])
    b_f = _chunk_cumsum(g_all[:, :GLA_KW], reverse=False)
    b_b = _chunk_cumsum(g_all[:, GLA_KW:], reverse=True)
    r_raw = _dot(h, wr_ref[...])

    q_f, q_b, k_f, k_b = [], [], [], []
    for c, rows in enumerate(chunks):
        q, k, bf, bb = q_all[rows], k_all[rows], b_f[rows], b_b[rows]
        edge_f = bf[CHUNK - 1:CHUNK]
        edge_b = bb[0:1]
        q_f.append((q * jnp.exp(bf)).astype(BF16))
        q_b.append((q * jnp.exp(bb)).astype(BF16))
        k_f.append((k * jnp.exp(-bf)).astype(BF16))
        k_b.append((k * jnp.exp(-bb)).astype(BF16))
        qf_ref[0, rows, :] = q_f[c]
        qb_ref[0, rows, :] = q_b[c]
        kef_ref[0, rows, :] = (k * jnp.exp(edge_f - bf)).astype(BF16)
        keb_ref[0, rows, :] = (k * jnp.exp(edge_b - bb)).astype(BF16)
        decf_ref[0, c:c + 1, :] = jnp.exp(edge_f)
        decb_ref[0, c:c + 1, :] = jnp.exp(edge_b)

    gp_ref[0] = _sigmoid(gp_raw).astype(BF16)
    arow = lax.broadcasted_iota(jnp.int32, (CHUNK, GLA_KW), 0)
    acol = lax.broadcasted_iota(jnp.int32, (CHUNK, GLA_KW), 1) % CHUNK
    causal = acol <= arow
    att = [jnp.where(causal, _dot_nt(q_f[c], _head_blocks(k_f[c])), _dot_nt(q_b[c], _head_blocks(k_b[c])))
           .astype(BF16) for c in range(len(chunks))]
    gg_ref[0] = _sigmoid(gg_raw).astype(BF16)

    v_head = lax.broadcasted_iota(jnp.int32, (CHUNK, GLA_VW), 1) // GLA_DV
    zero = jnp.zeros((), BF16)
    for c, rows in enumerate(chunks):
        v_bf = v_all[rows].astype(BF16)
        v_blk = jnp.concatenate([jnp.where(v_head == hd, v_bf, zero) for hd in range(GLA_HEADS)], axis=0)
        oi_ref[0, rows, :] = _dot(att[c], v_blk)
    r_ref[0] = _silu(r_raw).astype(BF16)

    for c, rows in enumerate(chunks):
        v = v_all[rows]
        v_stack = jnp.concatenate([v[:, hd * GLA_DV:(hd + 1) * GLA_DV] for hd in range(GLA_HEADS)], axis=0)
        vst_ref[0, c] = v_stack.T.astype(BF16)


def _proj(x, mod, nw, lw):
    b, t, d = x.shape
    tm = _token_tile(t)
    nc, tc = t // CHUNK, tm // CHUNK
    tok = lambda w: pl.BlockSpec((1, tm, w), lambda b, i: (b, i, 0))
    dec = pl.BlockSpec((1, tc, GLA_KW), lambda b, i: (b, i, 0))
    outs = [
        ('qf', (b, t, GLA_KW), BF16, tok(GLA_KW)), ('kef', (b, t, GLA_KW), BF16, tok(GLA_KW)),
        ('decf', (b, nc, GLA_KW), F32, dec),
        ('qb', (b, t, GLA_KW), BF16, tok(GLA_KW)), ('keb', (b, t, GLA_KW), BF16, tok(GLA_KW)),
        ('decb', (b, nc, GLA_KW), F32, dec),
        ('vst', (b, nc, GLA_DV, GLA_KW), BF16, pl.BlockSpec((1, tc, GLA_DV, GLA_KW), lambda b, i: (b, i, 0, 0))),
        ('oi', (b, t, GLA_VW), F32, tok(GLA_VW)),
        ('r', (b, t, GLA_VW), BF16, tok(GLA_VW)), ('u', (b, t, POOL_WIDTH), F32, tok(POOL_WIDTH)),
        ('gp', (b, t, d), BF16, tok(d)), ('gg', (b, t, d), BF16, tok(d)),
    ]
    weights = [lw[n] for n in _PROJ_WEIGHTS]
    res = pl.pallas_call(
        _proj_kernel,
        grid=(b, t // tm),
        in_specs=[tok(d), pl.BlockSpec((1, N_MOD, d), _per_batch_or_shared(mod)), _resident((1, d))]
        + [_resident(w.shape) for w in weights] + [_resident(lw['b_gate'].shape)],
        out_specs=[o[3] for o in outs],
        out_shape=[jax.ShapeDtypeStruct(o[1], o[2]) for o in outs],
        compiler_params=_params("parallel", "parallel"),
        name="mixer_proj",
    )(x, mod, nw, *weights, lw['b_gate'])
    return {o[0]: a for o, a in zip(outs, res)}


def _gla_kernel(qf_ref, kef_ref, decf_ref, qb_ref, keb_ref, decb_ref, vst_ref, oi_ref, r_ref,
                sf0_ref, sb0_ref, nw_ref, og_ref, sf_ref, sb_ref, st_ref, ob_ref, *, n_blocks, n_chunks):
    step = pl.program_id(1)
    tb = n_chunks * CHUNK

    def chunk(q_ref, ke_ref, dec_ref, c):
        rows = pl.ds(c * CHUNK, CHUNK)
        st = st_ref[...]
        o_stack = _dot_nt(_head_blocks(q_ref[0, rows, :]), st.astype(BF16))
        st_ref[...] = st * dec_ref[0, c:c + 1, :] + _dot(vst_ref[0, c], _head_blocks(ke_ref[0, rows, :]))
        return jnp.concatenate([o_stack[h * CHUNK:(h + 1) * CHUNK] for h in range(GLA_HEADS)], axis=1)

    @pl.when(step == 0)
    def _():
        st_ref[...] = sb0_ref[0]

    @pl.when(step < n_blocks)
    def _():
        base = (n_blocks - 1 - step) * tb
        for c in reversed(range(n_chunks)):
            o = chunk(qb_ref, keb_ref, decb_ref, c)
            ob_ref[pl.ds(pl.multiple_of(base + c * CHUNK, CHUNK), CHUNK), :] = o

    @pl.when(step == n_blocks - 1)
    def _():
        sb_ref[0] = st_ref[...]

    @pl.when(step == n_blocks)
    def _():
        st_ref[...] = sf0_ref[0]

    @pl.when(step >= n_blocks)
    def _():
        base = (step - n_blocks) * tb
        nw = nw_ref[...]
        for c in range(n_chunks):
            rows = pl.ds(c * CHUNK, CHUNK)
            o = chunk(qf_ref, kef_ref, decf_ref, c)
            o = o + ob_ref[pl.ds(pl.multiple_of(base + c * CHUNK, CHUNK), CHUNK), :] + oi_ref[0, rows, :]
            o = jnp.concatenate(
                [_rms(o[:, h * GLA_DV:(h + 1) * GLA_DV]) * nw for h in range(GLA_HEADS)], axis=1)
            og_ref[0, rows, :] = (o * r_ref[0, rows, :].astype(F32)).astype(og_ref.dtype)

    @pl.when(step == 2 * n_blocks - 1)
    def _():
        sf_ref[0] = st_ref[...]


def _gla(p, r, s_f0, s_b0, norm_w):
    b, t, _ = p['qf'].shape
    tb = _token_tile(t, SCAN_TOKEN_TILE)
    nb = t // tb
    tc = tb // CHUNK
    fwd = lambda b, s: (b, jnp.maximum(s - nb, 0), 0)
    rev = lambda b, s: (b, nb - 1 - jnp.minimum(s, nb - 1), 0)
    both = lambda b, s: (b, jnp.where(s < nb, nb - 1 - s, s - nb), 0, 0)
    tok = lambda w, m: pl.BlockSpec((1, tb, w), m)
    dec = lambda m: pl.BlockSpec((1, tc, GLA_KW), m)
    state = pl.BlockSpec((1, GLA_DV, GLA_KW), lambda b, s: (b, 0, 0))
    kern = functools.partial(_gla_kernel, n_blocks=nb, n_chunks=tc)
    return pl.pallas_call(
        kern,
        grid=(b, 2 * nb),
        in_specs=[tok(GLA_KW, fwd), tok(GLA_KW, fwd), dec(fwd), tok(GLA_KW, rev), tok(GLA_KW, rev), dec(rev),
                  pl.BlockSpec((1, tc, GLA_DV, GLA_KW), both), tok(GLA_VW, fwd), tok(GLA_VW, fwd),
                  state, state, _resident((1, GLA_DV))],
        out_specs=[tok(GLA_VW, fwd), state, state],
        out_shape=[jax.ShapeDtypeStruct((b, t, GLA_VW), BF16),
                   jax.ShapeDtypeStruct((b, GLA_DV, GLA_KW), F32),
                   jax.ShapeDtypeStruct((b, GLA_DV, GLA_KW), F32)],
        scratch_shapes=[pltpu.VMEM((GLA_DV, GLA_KW), F32), pltpu.VMEM((t, GLA_VW), F32)],
        compiler_params=_params("arbitrary", "arbitrary"),
        name="gla_scan",
    )(p['qf'], p['kef'], p['decf'], p['qb'], p['keb'], p['decb'], p['vst'], p['oi'], r, s_f0, s_b0, norm_w)


def _col_box_sum(x, n_cols, w):
    half = w // 2
    c_out = lax.broadcasted_iota(jnp.int32, (n_cols, 2 * n_cols), 0)
    c_in = lax.broadcasted_iota(jnp.int32, (n_cols, 2 * n_cols), 1) % n_cols
    band = jnp.where((c_in >= c_out - half) & (c_in < c_out + half), 1.0, 0.0).astype(BF16)
    hi = x.astype(BF16)
    lo = (x - hi.astype(F32)).astype(BF16)
    out = []
    for r in range(x.shape[0] // n_cols):
        rows = slice(r * n_cols, (r + 1) * n_cols)
        out.append(_dot(band, jnp.concatenate([hi[rows], lo[rows]], axis=0)))
    return jnp.concatenate(out, axis=0)


def _row_box_sum(u_ref, lanes, tile, n_tiles, tm, n_cols, w):
    half = w // 2
    halo = half * n_cols
    assert halo <= tm and tm % n_cols == 0
    cur = u_ref[0, pl.ds(pl.multiple_of(tile * tm, tm), tm), lanes]
    before = jnp.maximum(tile - 1, 0) * tm + (tm - halo)
    after = jnp.minimum(tile + 1, n_tiles - 1) * tm
    prev = u_ref[0, pl.ds(pl.multiple_of(before, n_cols), halo), lanes]
    nxt = u_ref[0, pl.ds(pl.multiple_of(after, n_cols), halo), lanes]
    win = jnp.concatenate([jnp.where(tile > 0, prev, 0.0), cur, jnp.where(tile < n_tiles - 1, nxt, 0.0)],
                          axis=0)
    span = 1
    while span < w:
        win = win[:-span * n_cols] + win[span * n_cols:]
        span *= 2
    return win[:tm]


def _merge_kernel(x_ref, mod_ref, og_ref, u_ref, gp_ref, gg_ref, wbg_ref, wbp_ref, wo_ref, pw_ref, ps_ref,
                  o_ref, *, n_rows, n_cols, n_tiles):
    tile = pl.program_id(1)
    tm = x_ref.shape[1]
    y_gla = _dot(og_ref[0], wbg_ref[...])
    tok = lax.broadcasted_iota(jnp.int32, (tm, POOL_GROUP), 0)
    col = (tok % n_cols).astype(F32)
    row = (tile * (tm // n_cols) + tok // n_cols).astype(F32)
    mixed = []
    for g, w in enumerate(POOL_WINDOWS):
        half = float(w // 2)
        lanes = slice(g * POOL_GROUP, (g + 1) * POOL_GROUP)
        u = u_ref[0, pl.ds(pl.multiple_of(tile * tm, tm), tm), lanes]
        cnt = jnp.minimum(col, half) + jnp.minimum(n_cols - col, half)
        m = u
        if n_rows > 1:
            cnt = cnt * (jnp.minimum(row, half) + jnp.minimum(n_rows - row, half))
            m = _row_box_sum(u_ref, lanes, tile, n_tiles, tm, n_cols, w)
        m = _col_box_sum(m, n_cols, w) / cnt
        mixed.append((_dot((m - u).astype(BF16), pw_ref[lanes, :]) * ps_ref[:, lanes]).astype(BF16))
    y_pool = _dot(jnp.concatenate(mixed, axis=1), wbp_ref[...])
    merged = gp_ref[0].astype(F32) * y_pool + gg_ref[0].astype(F32) * y_gla
    gate = mod_ref[0][5:6]
    o_ref[0] = x_ref[0] + gate * _dot(merged.astype(BF16), wo_ref[...])


def _merge(x, mod, og, u, gp, gg, lw, *, n_rows, n_cols):
    b, t, d = x.shape
    tm = _token_tile(t)
    tok = lambda w: pl.BlockSpec((1, tm, w), lambda b, i: (b, i, 0))
    whole = pl.BlockSpec((1, t, POOL_WIDTH), lambda b, i: (b, 0, 0), pipeline_mode=pl.Buffered(1))
    weights = [lw['w_br_gla'], lw['w_br_pool'], lw['w_out'], lw['pool_w']]
    kern = functools.partial(_merge_kernel, n_rows=n_rows, n_cols=n_cols, n_tiles=t // tm)
    return pl.pallas_call(
        kern,
        grid=(b, t // tm),
        in_specs=[tok(d), pl.BlockSpec((1, N_MOD, d), _per_batch_or_shared(mod)),
                  tok(GLA_VW), whole, tok(d), tok(d)] + [_resident(w.shape) for w in weights]
        + [_resident(lw['pool_scale'].shape)],
        out_specs=tok(d),
        out_shape=jax.ShapeDtypeStruct(x.shape, F32),
        compiler_params=_params("parallel", "parallel"),
        name="mixer_merge",
    )(x, mod, og, u, gp, gg, *weights, lw['pool_scale'])


_W_IN_PARTS = (('w_k', OFF_K, OFF_V), ('w_v', OFF_V, OFF_AF), ('w_a', OFF_AF, OFF_Q), ('w_q', OFF_Q, OFF_R),
               ('w_r', OFF_R, OFF_POOL), ('w_u', OFF_POOL, OFF_GP), ('w_gp', OFF_GP, OFF_GG),
               ('w_gg', OFF_GG, IN_COLS))
_MIXER_MATS = ('w_br_gla', 'w_br_pool', 'w_out', 'pool_w')


def _mixer_cast_jobs(l, w_in, w_br_gla, w_br_pool, w_out, pool_w):
    return [(w_in, l, tuple((lo, hi) for _, lo, hi in _W_IN_PARTS)), (w_br_gla, l, None), (w_br_pool, l, None),
            (w_out, l, None), (pool_w.reshape(DEPTH, POOL_WIDTH, POOL_GROUP), l, None)]


def _mixer(x, mod, nw, lw, s_f0, s_b0, gla_norm_w, *, n_rows, n_cols, states_only=False):
    p = _proj(x, mod, nw, lw)
    og, s_f, s_b = _gla(p, p['r'], s_f0, s_b0, gla_norm_w)
    if states_only:
        return None, s_f, s_b
    return _merge(x, mod, og, p['u'], p['gp'], p['gg'], lw, n_rows=n_rows, n_cols=n_cols), s_f, s_b


def kernel(x, c, ctx, c_ctx, w_mod, b_mod, norm_w, ffn1_up, ffn1_down, w_in, w_af_up, b_af, w_ab_up, b_ab,
           gla_norm_w, pool_w, pool_scale, w_br_pool, w_br_gla, w_out, ffn2_up, ffn2_down, final_norm_w):
    batch, seq, d = x.shape
    ctx_len = ctx.shape[1]
    cond = jnp.concatenate([c, c_ctx[None, :], jnp.zeros((SUBLANES - batch - 1, d), F32)], axis=0)
    mod_all = _modulation(cond, w_mod, b_mod)
    zero_state = jnp.zeros((batch, GLA_DV, GLA_KW), F32)
    fw = final_norm_w[None, :]
    zeros = jnp.zeros((GATE_RANK, GLA_KW), F32)
    up1, down1 = ffn1_up[0].astype(BF16), ffn1_down[0].astype(BF16)
    for l in range(DEPTH):
        last = l == DEPTH - 1
        mod_x = mod_all[l, :batch].reshape(batch, N_MOD, d)
        mod_c = mod_all[l, batch:batch + 1].reshape(1, N_MOD, d)
        nw = norm_w[l][:, None, :]
        gnw = gla_norm_w[l][None, :]
        lw = {
            'w_gate': jnp.concatenate([jnp.concatenate([w_af_up[l], zeros], axis=1),
                                       jnp.concatenate([zeros, w_ab_up[l]], axis=1)], axis=0).astype(BF16),
            'b_gate': jnp.concatenate([b_af[l], b_ab[l]])[None, :],
            'pool_scale': pool_scale[l][None, :],
        }

        x, cast = _ffn(x, mod_x, nw[0], up1, down1, fw, mod_row=0, final_norm=False,
                       cast=[(ffn2_up, l, None), (ffn2_down, l, None)]
                       + _mixer_cast_jobs(l, w_in, w_br_gla, w_br_pool, w_out, pool_w))
        up2, down2 = cast[:2]
        lw.update(zip([n for n, _, _ in _W_IN_PARTS] + list(_MIXER_MATS), cast[2:]))
        ctx, _ = _ffn(ctx, mod_c, nw[0], up1, down1, fw, mod_row=0, final_norm=False)

        ctx_mixed, s_f, s_b = _mixer(ctx, mod_c, nw[1], lw, zero_state, zero_state, gnw,
                                     n_rows=1, n_cols=ctx_len, states_only=last)
        x, _, _ = _mixer(x, mod_x, nw[1], lw, s_f, s_b, gnw, n_rows=seq // GRID_W, n_cols=GRID_W)
        x, cast = _ffn(x, mod_x, nw[2], up2, down2, fw, mod_row=6, final_norm=last,
                       cast=[] if last else [(ffn1_up, l + 1, None), (ffn1_down, l + 1, None)])
        if not last:
            ctx, _ = _ffn(ctx_mixed, mod_c, nw[2], up2, down2, fw, mod_row=6, final_norm=False)
            up1, down1 = cast
    return x
```

```python
import functools

import jax
import jax.numpy as jnp
from jax import lax
from jax.experimental import pallas as pl
from jax.experimental.pallas import tpu as pltpu

F32 = jnp.float32
BF16 = jnp.bfloat16

D_MODEL = 1024
DEPTH = 2
GRID_W = 64
N_POOL_GROUPS = 4
POOL_GROUP = 128
POOL_WIDTH = N_POOL_GROUPS * POOL_GROUP
POOL_WINDOWS = (2, 4, 8, 16)
GLA_HEADS = 4
GLA_DK = 64
GLA_DV = 128
GLA_KW = GLA_HEADS * GLA_DK
GLA_VW = GLA_HEADS * GLA_DV
GATE_RANK = 16
GATE_TAU = 16.0
CHUNK = 64
D_FF = 2816
N_MOD = 9
EPS = 1e-6

OFF_K = 0
OFF_V = OFF_K + GLA_KW
OFF_AF = OFF_V + GLA_VW
OFF_AB = OFF_AF + GATE_RANK
OFF_Q = OFF_AB + GATE_RANK
OFF_R = OFF_Q + GLA_KW
OFF_POOL = OFF_R + GLA_VW
OFF_GP = OFF_POOL + POOL_WIDTH
OFF_GG = OFF_GP + D_MODEL
IN_COLS = OFF_GG + D_MODEL

V7X_VMEM_BYTES = 64 * 1024 * 1024
VMEM_LIMIT_BYTES = 56 * 1024 * 1024
SUBLANES = 8
TOKEN_TILE = 512
FFN_TOKEN_TILE = 1024
SCAN_TOKEN_TILE = 1024
FF_CHUNK = 256
MOD_COL_TILE = 1152


def _resident(shape):
    return pl.BlockSpec(shape, lambda *_: (0,) * len(shape), pipeline_mode=pl.Buffered(1))


def _params(*semantics):
    return pltpu.CompilerParams(dimension_semantics=semantics, vmem_limit_bytes=VMEM_LIMIT_BYTES)


def _per_batch_or_shared(arr):
    zeros = (0,) * (arr.ndim - 1)
    if arr.shape[0] == 1:
        return lambda b, i: (0,) + zeros
    return lambda b, i: (b,) + zeros


def _token_tile(t, tile=TOKEN_TILE):
    return min(tile, t)


def _rms(x):
    return x * lax.rsqrt(jnp.mean(x * x, axis=-1, keepdims=True) + EPS)


def _modulated_norm(x, w, shift, scale):
    return (_rms(x) * w) * (1.0 + scale) + shift


def _silu(a):
    return a / (1.0 + jnp.exp(-a))


def _sigmoid(a):
    return 1.0 / (1.0 + jnp.exp(-a))


def _dot(a, b):
    return jnp.dot(a, b, preferred_element_type=F32)


def _dot_nt(a, b):
    return lax.dot_general(a, b, (((1,), (1,)), ((), ())), preferred_element_type=F32)


def _head_blocks(x):
    head = lax.broadcasted_iota(jnp.int32, x.shape, 1) // GLA_DK
    zero = jnp.zeros((), x.dtype)
    return jnp.concatenate([jnp.where(head == h, x, zero) for h in range(GLA_HEADS)], axis=0)


def _chunk_cumsum(g, *, reverse):
    n = g.shape[0]
    pos = lax.broadcasted_iota(jnp.int32, g.shape, 0) % CHUNK
    acc = g
    span = 1
    while span < CHUNK:
        if reverse:
            acc = acc + jnp.where(pos + span < CHUNK, pltpu.roll(acc, n - span, axis=0), 0.0)
        else:
            acc = acc + jnp.where(pos >= span, pltpu.roll(acc, span, axis=0), 0.0)
        span *= 2
    return acc


def _mod_kernel(cond_ref, w_ref, b_ref, o_ref):
    cond = cond_ref[...]
    o_ref[0] = _dot(_silu(cond).astype(BF16), w_ref[0].astype(BF16)) + b_ref[0]


def _modulation(cond, w_mod, b_mod):
    n_cols = N_MOD * D_MODEL
    rows = cond.shape[0]
    return pl.pallas_call(
        _mod_kernel,
        grid=(DEPTH, n_cols // MOD_COL_TILE),
        in_specs=[
            pl.BlockSpec((rows, D_MODEL), lambda l, j: (0, 0)),
            pl.BlockSpec((1, D_MODEL, MOD_COL_TILE), lambda l, j: (l, 0, j)),
            pl.BlockSpec((1, 1, MOD_COL_TILE), lambda l, j: (l, 0, j)),
        ],
        out_specs=pl.BlockSpec((1, rows, MOD_COL_TILE), lambda l, j: (l, 0, j)),
        out_shape=jax.ShapeDtypeStruct((DEPTH, rows, n_cols), F32),
        compiler_params=_params("arbitrary", "arbitrary"),
        name="modulation",
    )(cond, w_mod, b_mod.reshape(DEPTH, 1, n_cols))


def _ffn_kernel(*refs, mod_row, final_norm, n_cast):
    x_ref, mod_ref, nw_ref, up_ref, down_ref, fw_ref = refs[:6]
    cast_in = refs[6:6 + n_cast]
    o_ref = refs[6 + n_cast]
    cast_out = refs[7 + n_cast:7 + 2 * n_cast]
    act_ref = refs[7 + 2 * n_cast]
    x = x_ref[0]
    mod = mod_ref[0]
    shift = mod[mod_row:mod_row + 1]
    scale = mod[mod_row + 1:mod_row + 2]
    gate = mod[mod_row + 2:mod_row + 3]
    h = _modulated_norm(x, nw_ref[...], shift, scale).astype(BF16)
    for j in range(D_FF // FF_CHUNK):
        lo = j * FF_CHUNK
        a = _dot(h, up_ref[:, lo:lo + FF_CHUNK])
        b = _dot(h, up_ref[:, D_FF + lo:D_FF + lo + FF_CHUNK])
        act_ref[:, lo:lo + FF_CHUNK] = (_silu(a) * b).astype(BF16)
    y = x + (0.5 * gate) * _dot(act_ref[...], down_ref[...])
    if final_norm:
        y = _rms(y) * fw_ref[...]
    o_ref[0] = y
    for src_ref, dst_ref in zip(cast_in, cast_out):
        dst_ref[...] = src_ref[0].astype(BF16)


def _ffn(x, mod, nw, up, down, fw, *, mod_row, final_norm, cast=()):
    b, t, d = x.shape
    tm = _token_tile(t, FFN_TOKEN_TILE)
    nt = t // tm
    steps = b * nt
    cast_specs, out_specs, out_shapes = [], [], []
    for w, l, start, n_rows in cast:
        rows = n_rows // steps
        assert rows * steps == n_rows and rows % (2 * SUBLANES) == 0 and start % rows == 0, (w.shape, start)
        cast_specs.append(pl.BlockSpec((1, rows, w.shape[2]),
                                       lambda b, i, l=l, first=start // rows: (l, first + b * nt + i, 0)))
        out_specs.append(pl.BlockSpec((rows, w.shape[2]), lambda b, i: (b * nt + i, 0)))
        out_shapes.append(jax.ShapeDtypeStruct((n_rows, w.shape[2]), BF16))
    kern = functools.partial(_ffn_kernel, mod_row=mod_row, final_norm=final_norm, n_cast=len(cast))
    res = pl.pallas_call(
        kern,
        grid=(b, nt),
        in_specs=[
            pl.BlockSpec((1, tm, d), lambda b, i: (b, i, 0)),
            pl.BlockSpec((1, N_MOD, d), _per_batch_or_shared(mod)),
            _resident((1, d)),
            _resident(up.shape),
            _resident(down.shape),
            _resident((1, d)),
        ] + cast_specs,
        out_specs=[pl.BlockSpec((1, tm, d), lambda b, i: (b, i, 0))] + out_specs,
        out_shape=[jax.ShapeDtypeStruct(x.shape, F32)] + out_shapes,
        scratch_shapes=[pltpu.VMEM((tm, D_FF), BF16)],
        compiler_params=_params("parallel", "parallel"),
        name="ffn",
    )(x, mod, nw, up, down, fw, *[job[0] for job in cast])
    return res[0], list(res[1:])


_PROJ_WEIGHTS = ('w_k', 'w_v', 'w_a', 'w_q', 'w_r', 'w_u', 'w_gp0', 'w_gp1', 'w_gg0', 'w_gg1', 'w_gate')


def _proj_kernel(x_ref, mod_ref, nw_ref, wk_ref, wv_ref, wa_ref, wq_ref, wr_ref, wu_ref, wgp0_ref, wgp1_ref,
                 wgg0_ref, wgg1_ref, wgate_ref, bgate_ref,
                 qf_ref, kef_ref, decf_ref, qb_ref, keb_ref, decb_ref, vst_ref, oi_ref,
                 r_ref, u_ref, gp_ref, gg_ref):
    x = x_ref[0]
    mod = mod_ref[0]
    tm = x.shape[0]
    chunks = [slice(c * CHUNK, (c + 1) * CHUNK) for c in range(tm // CHUNK)]
    h = _modulated_norm(x, nw_ref[...], mod[3:4], mod[4:5]).astype(BF16)
    k_all = _dot_nt(h, wk_ref[...])
    q_all = _dot_nt(h, wq_ref[...]) * (GLA_DK ** -0.5)
    v_all = _dot_nt(h, wv_ref[...])
    a_low = _dot_nt(h, wa_ref[...]).astype(BF16)
    z = _dot(a_low, wgate_ref[...]) + bgate_ref[...]
    u_ref[0] = _dot_nt(h, wu_ref[...])
    gp_raw = jnp.concatenate([_dot_nt(h, wgp0_ref[...]), _dot_nt(h, wgp1_ref[...])], axis=1)
    g_all = (jnp.minimum(z, 0.0) - jnp.log(1.0 + jnp.exp(-jnp.abs(z)))) / GATE_TAU

    gg_raw = jnp.concatenate([_dot_nt(h, wgg0_ref[...]), _dot_nt(h, wgg1_ref[...])], axis=1)
    b_f = _chunk_cumsum(g_all[:, :GLA_KW], reverse=False)
    b_b = _chunk_cumsum(g_all[:, GLA_KW:], reverse=True)
    r_raw = _dot_nt(h, wr_ref[...])

    q_f, q_b, k_f, k_b = [], [], [], []
    for c, rows in enumerate(chunks):
        q, k, bf, bb = q_all[rows], k_all[rows], b_f[rows], b_b[rows]
        edge_f = bf[CHUNK - 1:CHUNK]
        edge_b = bb[0:1]
        q_f.append((q * jnp.exp(bf)).astype(BF16))
        q_b.append((q * jnp.exp(bb)).astype(BF16))
        k_f.append((k * jnp.exp(-bf)).astype(BF16))
        k_b.append((k * jnp.exp(-bb)).astype(BF16))
        qf_ref[0, rows, :] = q_f[c]
        qb_ref[0, rows, :] = q_b[c]
        kef_ref[0, rows, :] = (k * jnp.exp(edge_f - bf)).astype(BF16)
        keb_ref[0, rows, :] = (k * jnp.exp(edge_b - bb)).astype(BF16)
        decf_ref[0, c:c + 1, :] = jnp.exp(edge_f)
        decb_ref[0, c:c + 1, :] = jnp.exp(edge_b)

    gp_ref[0] = _sigmoid(gp_raw).astype(BF16)
    arow = lax.broadcasted_iota(jnp.int32, (CHUNK, GLA_KW), 0)
    acol = lax.broadcasted_iota(jnp.int32, (CHUNK, GLA_KW), 1) % CHUNK
    causal = acol <= arow
    att = [jnp.where(causal, _dot_nt(q_f[c], _head_blocks(k_f[c])), _dot_nt(q_b[c], _head_blocks(k_b[c])))
           .astype(BF16) for c in range(len(chunks))]
    gg_ref[0] = _sigmoid(gg_raw).astype(BF16)

    v_head = lax.broadcasted_iota(jnp.int32, (CHUNK, GLA_VW), 1) // GLA_DV
    zero = jnp.zeros((), BF16)
    for c, rows in enumerate(chunks):
        v_bf = v_all[rows].astype(BF16)
        v_blk = jnp.concatenate([jnp.where(v_head == hd, v_bf, zero) for hd in range(GLA_HEADS)], axis=0)
        oi_ref[0, rows, :] = _dot(att[c], v_blk)
    r_ref[0] = _silu(r_raw).astype(BF16)

    for c, rows in enumerate(chunks):
        v = v_all[rows]
        v_stack = jnp.concatenate([v[:, hd * GLA_DV:(hd + 1) * GLA_DV] for hd in range(GLA_HEADS)], axis=0)
        vst_ref[0, c] = v_stack.T.astype(BF16)


def _proj(x, mod, nw, lw):
    b, t, d = x.shape
    tm = _token_tile(t)
    nc, tc = t // CHUNK, tm // CHUNK
    tok = lambda w: pl.BlockSpec((1, tm, w), lambda b, i: (b, i, 0))
    dec = pl.BlockSpec((1, tc, GLA_KW), lambda b, i: (b, i, 0))
    outs = [
        ('qf', (b, t, GLA_KW), BF16, tok(GLA_KW)), ('kef', (b, t, GLA_KW), BF16, tok(GLA_KW)),
        ('decf', (b, nc, GLA_KW), F32, dec),
        ('qb', (b, t, GLA_KW), BF16, tok(GLA_KW)), ('keb', (b, t, GLA_KW), BF16, tok(GLA_KW)),
        ('decb', (b, nc, GLA_KW), F32, dec),
        ('vst', (b, nc, GLA_DV, GLA_KW), BF16, pl.BlockSpec((1, tc, GLA_DV, GLA_KW), lambda b, i: (b, i, 0, 0))),
        ('oi', (b, t, GLA_VW), F32, tok(GLA_VW)),
        ('r', (b, t, GLA_VW), BF16, tok(GLA_VW)), ('u', (b, t, POOL_WIDTH), F32, tok(POOL_WIDTH)),
        ('gp', (b, t, d), BF16, tok(d)), ('gg', (b, t, d), BF16, tok(d)),
    ]
    weights = [lw[n] for n in _PROJ_WEIGHTS]
    res = pl.pallas_call(
        _proj_kernel,
        grid=(b, t // tm),
        in_specs=[tok(d), pl.BlockSpec((1, N_MOD, d), _per_batch_or_shared(mod)), _resident((1, d))]
        + [_resident(w.shape) for w in weights] + [_resident(lw['b_gate'].shape)],
        out_specs=[o[3] for o in outs],
        out_shape=[jax.ShapeDtypeStruct(o[1], o[2]) for o in outs],
        compiler_params=_params("parallel", "parallel"),
        name="mixer_proj",
    )(x, mod, nw, *weights, lw['b_gate'])
    return {o[0]: a for o, a in zip(outs, res)}


def _gla_kernel(qf_ref, kef_ref, decf_ref, qb_ref, keb_ref, decb_ref, vst_ref, oi_ref, r_ref,
                sf0_ref, sb0_ref, nw_ref, og_ref, sf_ref, sb_ref, st_ref, ob_ref, *, n_blocks, n_chunks):
    step = pl.program_id(1)
    tb = n_chunks * CHUNK

    def chunk(q_ref, ke_ref, dec_ref, c):
        rows = pl.ds(c * CHUNK, CHUNK)
        st = st_ref[...]
        o_stack = _dot_nt(_head_blocks(q_ref[0, rows, :]), st.astype(BF16))
        st_ref[...] = st * dec_ref[0, c:c + 1, :] + _dot(vst_ref[0, c], _head_blocks(ke_ref[0, rows, :]))
        return jnp.concatenate([o_stack[h * CHUNK:(h + 1) * CHUNK] for h in range(GLA_HEADS)], axis=1)

    @pl.when(step == 0)
    def _():
        st_ref[...] = sb0_ref[0]

    @pl.when(step < n_blocks)
    def _():
        base = (n_blocks - 1 - step) * tb
        for c in reversed(range(n_chunks)):
            o = chunk(qb_ref, keb_ref, decb_ref, c)
            ob_ref[pl.ds(pl.multiple_of(base + c * CHUNK, CHUNK), CHUNK), :] = o

    @pl.when(step == n_blocks - 1)
    def _():
        sb_ref[0] = st_ref[...]

    @pl.when(step == n_blocks)
    def _():
        st_ref[...] = sf0_ref[0]

    @pl.when(step >= n_blocks)
    def _():
        base = (step - n_blocks) * tb
        nw = nw_ref[...]
        for c in range(n_chunks):
            rows = pl.ds(c * CHUNK, CHUNK)
            o = chunk(qf_ref, kef_ref, decf_ref, c)
            o = o + ob_ref[pl.ds(pl.multiple_of(base + c * CHUNK, CHUNK), CHUNK), :] + oi_ref[0, rows, :]
            o = jnp.concatenate(
                [_rms(o[:, h * GLA_DV:(h + 1) * GLA_DV]) * nw for h in range(GLA_HEADS)], axis=1)
            og_ref[0, rows, :] = (o * r_ref[0, rows, :].astype(F32)).astype(og_ref.dtype)

    @pl.when(step == 2 * n_blocks - 1)
    def _():
        sf_ref[0] = st_ref[...]


def _gla(p, r, s_f0, s_b0, norm_w):
    b, t, _ = p['qf'].shape
    tb = _token_tile(t, SCAN_TOKEN_TILE)
    nb = t // tb
    tc = tb // CHUNK
    fwd = lambda b, s: (b, jnp.maximum(s - nb, 0), 0)
    rev = lambda b, s: (b, nb - 1 - jnp.minimum(s, nb - 1), 0)
    both = lambda b, s: (b, jnp.where(s < nb, nb - 1 - s, s - nb), 0, 0)
    tok = lambda w, m: pl.BlockSpec((1, tb, w), m)
    dec = lambda m: pl.BlockSpec((1, tc, GLA_KW), m)
    state = pl.BlockSpec((1, GLA_DV, GLA_KW), lambda b, s: (b, 0, 0))
    kern = functools.partial(_gla_kernel, n_blocks=nb, n_chunks=tc)
    return pl.pallas_call(
        kern,
        grid=(b, 2 * nb),
        in_specs=[tok(GLA_KW, fwd), tok(GLA_KW, fwd), dec(fwd), tok(GLA_KW, rev), tok(GLA_KW, rev), dec(rev),
                  pl.BlockSpec((1, tc, GLA_DV, GLA_KW), both), tok(GLA_VW, fwd), tok(GLA_VW, fwd),
                  state, state, _resident((1, GLA_DV))],
        out_specs=[tok(GLA_VW, fwd), state, state],
        out_shape=[jax.ShapeDtypeStruct((b, t, GLA_VW), BF16),
                   jax.ShapeDtypeStruct((b, GLA_DV, GLA_KW), F32),
                   jax.ShapeDtypeStruct((b, GLA_DV, GLA_KW), F32)],
        scratch_shapes=[pltpu.VMEM((GLA_DV, GLA_KW), F32), pltpu.VMEM((t, GLA_VW), F32)],
        compiler_params=_params("arbitrary", "arbitrary"),
        name="gla_scan",
    )(p['qf'], p['kef'], p['decf'], p['qb'], p['keb'], p['decb'], p['vst'], p['oi'], r, s_f0, s_b0, norm_w)


def _col_box_sum(x, n_cols, w):
    half = w // 2
    c_out = lax.broadcasted_iota(jnp.int32, (n_cols, 2 * n_cols), 0)
    c_in = lax.broadcasted_iota(jnp.int32, (n_cols, 2 * n_cols), 1) % n_cols
    band = jnp.where((c_in >= c_out - half) & (c_in < c_out + half), 1.0, 0.0).astype(BF16)
    hi = x.astype(BF16)
    lo = (x - hi.astype(F32)).astype(BF16)
    out = []
    for r in range(x.shape[0] // n_cols):
        rows = slice(r * n_cols, (r + 1) * n_cols)
        out.append(_dot(band, jnp.concatenate([hi[rows], lo[rows]], axis=0)))
    return jnp.concatenate(out, axis=0)


def _row_box_sum(u_ref, lanes, tile, n_tiles, tm, n_cols, w):
    half = w // 2
    halo = half * n_cols
    assert halo <= tm and tm % n_cols == 0
    cur = u_ref[0, pl.ds(pl.multiple_of(tile * tm, tm), tm), lanes]
    before = jnp.maximum(tile - 1, 0) * tm + (tm - halo)
    after = jnp.minimum(tile + 1, n_tiles - 1) * tm
    prev = u_ref[0, pl.ds(pl.multiple_of(before, n_cols), halo), lanes]
    nxt = u_ref[0, pl.ds(pl.multiple_of(after, n_cols), halo), lanes]
    win = jnp.concatenate([jnp.where(tile > 0, prev, 0.0), cur, jnp.where(tile < n_tiles - 1, nxt, 0.0)],
                          axis=0)
    span = 1
    while span < w:
        win = win[:-span * n_cols] + win[span * n_cols:]
        span *= 2
    return win[:tm]


def _merge_kernel(x_ref, mod_ref, og_ref, u_ref, gp_ref, gg_ref, wbg_ref, wbp_ref, wo_ref, pw_ref, ps_ref,
                  o_ref, *, n_rows, n_cols, n_tiles):
    tile = pl.program_id(1)
    tm = x_ref.shape[1]
    y_gla = _dot(og_ref[0], wbg_ref[...])
    tok = lax.broadcasted_iota(jnp.int32, (tm, POOL_GROUP), 0)
    col = (tok % n_cols).astype(F32)
    row = (tile * (tm // n_cols) + tok // n_cols).astype(F32)
    mixed = []
    for g, w in enumerate(POOL_WINDOWS):
        half = float(w // 2)
        lanes = slice(g * POOL_GROUP, (g + 1) * POOL_GROUP)
        u = u_ref[0, pl.ds(pl.multiple_of(tile * tm, tm), tm), lanes]
        cnt = jnp.minimum(col, half) + jnp.minimum(n_cols - col, half)
        m = u
        if n_rows > 1:
            cnt = cnt * (jnp.minimum(row, half) + jnp.minimum(n_rows - row, half))
            m = _row_box_sum(u_ref, lanes, tile, n_tiles, tm, n_cols, w)
        m = _col_box_sum(m, n_cols, w) / cnt
        mixed.append((_dot((m - u).astype(BF16), pw_ref[lanes, :]) * ps_ref[:, lanes]).astype(BF16))
    y_pool = _dot(jnp.concatenate(mixed, axis=1), wbp_ref[...])
    merged = gp_ref[0].astype(F32) * y_pool + gg_ref[0].astype(F32) * y_gla
    gate = mod_ref[0][5:6]
    o_ref[0] = x_ref[0] + gate * _dot(merged.astype(BF16), wo_ref[...])


def _merge(x, mod, og, u, gp, gg, lw, *, n_rows, n_cols):
    b, t, d = x.shape
    tm = _token_tile(t)
    tok = lambda w: pl.BlockSpec((1, tm, w), lambda b, i: (b, i, 0))
    whole = pl.BlockSpec((1, t, POOL_WIDTH), lambda b, i: (b, 0, 0), pipeline_mode=pl.Buffered(1))
    weights = [lw['w_br_gla'], lw['w_br_pool'], lw['w_out'], lw['pool_w']]
    kern = functools.partial(_merge_kernel, n_rows=n_rows, n_cols=n_cols, n_tiles=t // tm)
    return pl.pallas_call(
        kern,
        grid=(b, t // tm),
        in_specs=[tok(d), pl.BlockSpec((1, N_MOD, d), _per_batch_or_shared(mod)),
                  tok(GLA_VW), whole, tok(d), tok(d)] + [_resident(w.shape) for w in weights]
        + [_resident(lw['pool_scale'].shape)],
        out_specs=tok(d),
        out_shape=jax.ShapeDtypeStruct(x.shape, F32),
        compiler_params=_params("parallel", "parallel"),
        name="mixer_merge",
    )(x, mod, og, u, gp, gg, *weights, lw['pool_scale'])


_HALF_D = D_MODEL // 2
_W_IN_PARTS = (('w_k', OFF_K, GLA_KW), ('w_v', OFF_V, GLA_VW), ('w_q', OFF_Q, GLA_KW), ('w_r', OFF_R, GLA_VW),
               ('w_u', OFF_POOL, POOL_WIDTH), ('w_gp0', OFF_GP, _HALF_D), ('w_gp1', OFF_GP + _HALF_D, _HALF_D),
               ('w_gg0', OFF_GG, _HALF_D), ('w_gg1', OFF_GG + _HALF_D, _HALF_D))
_MIXER_MATS = ('w_br_gla', 'w_br_pool', 'w_out', 'pool_w')


def _mixer_cast_jobs(l, w_in_t, w_br_gla, w_br_pool, w_out, pool_w):
    whole = lambda w: (w, l, 0, w.shape[1])
    return ([(w_in_t, l, start, n) for _, start, n in _W_IN_PARTS]
            + [whole(w_br_gla), whole(w_br_pool), whole(w_out), whole(pool_w.reshape(DEPTH, POOL_WIDTH, POOL_GROUP))])


def _mixer(x, mod, nw, lw, s_f0, s_b0, gla_norm_w, *, n_rows, n_cols, states_only=False):
    p = _proj(x, mod, nw, lw)
    og, s_f, s_b = _gla(p, p['r'], s_f0, s_b0, gla_norm_w)
    if states_only:
        return None, s_f, s_b
    return _merge(x, mod, og, p['u'], p['gp'], p['gg'], lw, n_rows=n_rows, n_cols=n_cols), s_f, s_b


def kernel(x, c, ctx, c_ctx, w_mod, b_mod, norm_w, ffn1_up, ffn1_down, w_in, w_af_up, b_af, w_ab_up, b_ab,
           gla_norm_w, pool_w, pool_scale, w_br_pool, w_br_gla, w_out, ffn2_up, ffn2_down, final_norm_w):
    batch, seq, d = x.shape
    ctx_len = ctx.shape[1]
    cond = jnp.concatenate([c, c_ctx[None, :], jnp.zeros((SUBLANES - batch - 1, d), F32)], axis=0)
    mod_all = _modulation(cond, w_mod, b_mod)
    zero_state = jnp.zeros((batch, GLA_DV, GLA_KW), F32)
    fw = final_norm_w[None, :]
    zeros = jnp.zeros((GATE_RANK, GLA_KW), F32)
    up1, down1 = ffn1_up[0].astype(BF16), ffn1_down[0].astype(BF16)
    whole = lambda w, l: (w, l, 0, w.shape[1])
    w_in_t = jnp.swapaxes(w_in, 1, 2)
    for l in range(DEPTH):
        last = l == DEPTH - 1
        mod_x = mod_all[l, :batch].reshape(batch, N_MOD, d)
        mod_c = mod_all[l, batch:batch + 1].reshape(1, N_MOD, d)
        nw = norm_w[l][:, None, :]
        gnw = gla_norm_w[l][None, :]
        lw = {
            'w_gate': jnp.concatenate([jnp.concatenate([w_af_up[l], zeros], axis=1),
                                       jnp.concatenate([zeros, w_ab_up[l]], axis=1)], axis=0).astype(BF16),
            'b_gate': jnp.concatenate([b_af[l], b_ab[l]])[None, :],
            'pool_scale': pool_scale[l][None, :],
            'w_a': w_in_t[l, OFF_AF:OFF_Q].astype(BF16),
        }

        x, cast = _ffn(x, mod_x, nw[0], up1, down1, fw, mod_row=0, final_norm=False,
                       cast=[whole(ffn2_up, l), whole(ffn2_down, l)]
                       + _mixer_cast_jobs(l, w_in_t, w_br_gla, w_br_pool, w_out, pool_w))
        up2, down2 = cast[:2]
        lw.update(zip([n for n, _, _ in _W_IN_PARTS] + list(_MIXER_MATS), cast[2:]))
        ctx, _ = _ffn(ctx, mod_c, nw[0], up1, down1, fw, mod_row=0, final_norm=False)

        ctx_mixed, s_f, s_b = _mixer(ctx, mod_c, nw[1], lw, zero_state, zero_state, gnw,
                                     n_rows=1, n_cols=ctx_len, states_only=last)
        x, _, _ = _mixer(x, mod_x, nw[1], lw, s_f, s_b, gnw, n_rows=seq // GRID_W, n_cols=GRID_W)
        x, cast = _ffn(x, mod_x, nw[2], up2, down2, fw, mod_row=6, final_norm=last,
                       cast=[] if last else [whole(ffn1_up, l + 1), whole(ffn1_down, l + 1)])
        if not last:
            ctx, _ = _ffn(ctx_mixed, mod_c, nw[2], up2, down2, fw, mod_row=6, final_norm=False)
            up1, down1 = cast
    return x
```

```python
import functools

import jax
import jax.numpy as jnp
from jax import lax
from jax.experimental import pallas as pl
from jax.experimental.pallas import tpu as pltpu

F32 = jnp.float32
BF16 = jnp.bfloat16

D_MODEL = 1024
DEPTH = 2
GRID_W = 64
N_POOL_GROUPS = 4
POOL_GROUP = 128
POOL_WIDTH = N_POOL_GROUPS * POOL_GROUP
POOL_WINDOWS = (2, 4, 8, 16)
GLA_HEADS = 4
GLA_DK = 64
GLA_DV = 128
GLA_KW = GLA_HEADS * GLA_DK
GLA_VW = GLA_HEADS * GLA_DV
GATE_RANK = 16
GATE_TAU = 16.0
CHUNK = 64
D_FF = 2816
N_MOD = 9
EPS = 1e-6

OFF_K = 0
OFF_V = OFF_K + GLA_KW
OFF_AF = OFF_V + GLA_VW
OFF_AB = OFF_AF + GATE_RANK
OFF_Q = OFF_AB + GATE_RANK
OFF_R = OFF_Q + GLA_KW
OFF_POOL = OFF_R + GLA_VW
OFF_GP = OFF_POOL + POOL_WIDTH
OFF_GG = OFF_GP + D_MODEL
IN_COLS = OFF_GG + D_MODEL

V7X_VMEM_BYTES = 64 * 1024 * 1024
VMEM_LIMIT_BYTES = 56 * 1024 * 1024
SUBLANES = 8
TOKEN_TILE = 1024
FFN_TOKEN_TILE = 1024
SCAN_TOKEN_TILE = 1024
MERGE_TOKEN_TILE = 1024
FF_CHUNK = 256
MOD_COL_TILE = 1152


def _resident(shape):
    return pl.BlockSpec(shape, lambda *_: (0,) * len(shape), pipeline_mode=pl.Buffered(1))


def _params(*semantics):
    return pltpu.CompilerParams(dimension_semantics=semantics, vmem_limit_bytes=VMEM_LIMIT_BYTES)


def _per_batch_or_shared(arr):
    zeros = (0,) * (arr.ndim - 1)
    if arr.shape[0] == 1:
        return lambda b, i: (0,) + zeros
    return lambda b, i: (b,) + zeros


def _token_tile(t, tile=TOKEN_TILE):
    return min(tile, t)


def _rms(x):
    return x * lax.rsqrt(jnp.mean(x * x, axis=-1, keepdims=True) + EPS)


def _modulated_norm(x, w, shift, scale):
    return (_rms(x) * w) * (1.0 + scale) + shift


def _silu(a):
    return a / (1.0 + jnp.exp(-a))


def _sigmoid(a):
    return 1.0 / (1.0 + jnp.exp(-a))


def _dot(a, b):
    return jnp.dot(a, b, preferred_element_type=F32)


def _dot_nt(a, b):
    return lax.dot_general(a, b, (((1,), (1,)), ((), ())), preferred_element_type=F32)


def _head_blocks(x):
    head = lax.broadcasted_iota(jnp.int32, x.shape, 1) // GLA_DK
    zero = jnp.zeros((), x.dtype)
    return jnp.concatenate([jnp.where(head == h, x, zero) for h in range(GLA_HEADS)], axis=0)


def _chunk_cumsum(g, *, reverse):
    n = g.shape[0]
    pos = lax.broadcasted_iota(jnp.int32, g.shape, 0) % CHUNK
    acc = g
    span = 1
    while span < CHUNK:
        if reverse:
            acc = acc + jnp.where(pos + span < CHUNK, pltpu.roll(acc, n - span, axis=0), 0.0)
        else:
            acc = acc + jnp.where(pos >= span, pltpu.roll(acc, span, axis=0), 0.0)
        span *= 2
    return acc


def _mod_kernel(cond_ref, w_ref, b_ref, o_ref):
    cond = cond_ref[...]
    o_ref[0] = _dot(_silu(cond).astype(BF16), w_ref[0].astype(BF16)) + b_ref[0]


def _modulation(cond, w_mod, b_mod):
    n_cols = N_MOD * D_MODEL
    rows = cond.shape[0]
    return pl.pallas_call(
        _mod_kernel,
        grid=(DEPTH, n_cols // MOD_COL_TILE),
        in_specs=[
            pl.BlockSpec((rows, D_MODEL), lambda l, j: (0, 0)),
            pl.BlockSpec((1, D_MODEL, MOD_COL_TILE), lambda l, j: (l, 0, j)),
            pl.BlockSpec((1, 1, MOD_COL_TILE), lambda l, j: (l, 0, j)),
        ],
        out_specs=pl.BlockSpec((1, rows, MOD_COL_TILE), lambda l, j: (l, 0, j)),
        out_shape=jax.ShapeDtypeStruct((DEPTH, rows, n_cols), F32),
        compiler_params=_params("arbitrary", "arbitrary"),
        name="modulation",
    )(cond, w_mod, b_mod.reshape(DEPTH, 1, n_cols))


def _ffn_kernel(*refs, mod_row, final_norm, n_cast):
    x_ref, mod_ref, nw_ref, up_ref, down_ref, fw_ref = refs[:6]
    cast_in = refs[6:6 + n_cast]
    o_ref = refs[6 + n_cast]
    cast_out = refs[7 + n_cast:7 + 2 * n_cast]
    act_ref = refs[7 + 2 * n_cast]
    x = x_ref[0]
    mod = mod_ref[0]
    shift = mod[mod_row:mod_row + 1]
    scale = mod[mod_row + 1:mod_row + 2]
    gate = mod[mod_row + 2:mod_row + 3]
    h = _modulated_norm(x, nw_ref[...], shift, scale).astype(BF16)
    n_chunks = D_FF // FF_CHUNK
    for j in range(n_chunks):
        lo = j * FF_CHUNK
        a = _dot(h, up_ref[:, lo:lo + FF_CHUNK])
        b = _dot(h, up_ref[:, D_FF + lo:D_FF + lo + FF_CHUNK])
        act_ref[:, lo:lo + FF_CHUNK] = (_silu(a) * b).astype(BF16)
        for k in range(j, n_cast, n_chunks):
            cast_out[k][...] = cast_in[k][0].astype(BF16)
    y = x + (0.5 * gate) * _dot(act_ref[...], down_ref[...])
    if final_norm:
        y = _rms(y) * fw_ref[...]
    o_ref[0] = y


def _ffn(x, mod, nw, up, down, fw, *, mod_row, final_norm, cast=()):
    b, t, d = x.shape
    tm = _token_tile(t, FFN_TOKEN_TILE)
    nt = t // tm
    steps = b * nt
    cast_specs, out_specs, out_shapes = [], [], []
    for w, l, start, n_rows in cast:
        rows = n_rows // steps
        assert rows * steps == n_rows and rows % (2 * SUBLANES) == 0 and start % rows == 0, (w.shape, start)
        cast_specs.append(pl.BlockSpec((1, rows, w.shape[2]),
                                       lambda b, i, l=l, first=start // rows: (l, first + b * nt + i, 0)))
        out_specs.append(pl.BlockSpec((rows, w.shape[2]), lambda b, i: (b * nt + i, 0)))
        out_shapes.append(jax.ShapeDtypeStruct((n_rows, w.shape[2]), BF16))
    kern = functools.partial(_ffn_kernel, mod_row=mod_row, final_norm=final_norm, n_cast=len(cast))
    res = pl.pallas_call(
        kern,
        grid=(b, nt),
        in_specs=[
            pl.BlockSpec((1, tm, d), lambda b, i: (b, i, 0)),
            pl.BlockSpec((1, N_MOD, d), _per_batch_or_shared(mod)),
            _resident((1, d)),
            _resident(up.shape),
            _resident(down.shape),
            _resident((1, d)),
        ] + cast_specs,
        out_specs=[pl.BlockSpec((1, tm, d), lambda b, i: (b, i, 0))] + out_specs,
        out_shape=[jax.ShapeDtypeStruct(x.shape, F32)] + out_shapes,
        scratch_shapes=[pltpu.VMEM((tm, D_FF), BF16)],
        compiler_params=_params("parallel", "parallel"),
        name="ffn",
    )(x, mod, nw, up, down, fw, *[job[0] for job in cast])
    return res[0], list(res[1:])


_PROJ_WEIGHTS = ('w_k', 'w_v', 'w_a', 'w_q', 'w_r', 'w_u', 'w_gp0', 'w_gp1', 'w_gg0', 'w_gg1', 'w_gate')


def _proj_kernel(x_ref, mod_ref, nw_ref, wk_ref, wv_ref, wa_ref, wq_ref, wr_ref, wu_ref, wgp0_ref, wgp1_ref,
                 wgg0_ref, wgg1_ref, wgate_ref, bgate_ref,
                 qf_ref, kef_ref, decf_ref, qb_ref, keb_ref, decb_ref, vst_ref, oi_ref,
                 r_ref, u_ref, gp_ref, gg_ref):
    x = x_ref[0]
    mod = mod_ref[0]
    tm = x.shape[0]
    chunks = [slice(c * CHUNK, (c + 1) * CHUNK) for c in range(tm // CHUNK)]
    h = _modulated_norm(x, nw_ref[...], mod[3:4], mod[4:5]).astype(BF16)
    k_all = _dot_nt(h, wk_ref[...])
    q_all = _dot_nt(h, wq_ref[...]) * (GLA_DK ** -0.5)
    v_all = _dot_nt(h, wv_ref[...])
    a_low = _dot_nt(h, wa_ref[...].astype(BF16)).astype(BF16)
    z = _dot(a_low, wgate_ref[...]) + bgate_ref[...]
    u_ref[0] = _dot_nt(h, wu_ref[...])
    gp_raw = jnp.concatenate([_dot_nt(h, wgp0_ref[...]), _dot_nt(h, wgp1_ref[...])], axis=1)
    g_all = (jnp.minimum(z, 0.0) - jnp.log(1.0 + jnp.exp(-jnp.abs(z)))) / GATE_TAU

    gg_raw = jnp.concatenate([_dot_nt(h, wgg0_ref[...]), _dot_nt(h, wgg1_ref[...])], axis=1)
    b_f = _chunk_cumsum(g_all[:, :GLA_KW], reverse=False)
    b_b = _chunk_cumsum(g_all[:, GLA_KW:], reverse=True)
    r_raw = _dot_nt(h, wr_ref[...])

    q_f, q_b, k_f, k_b = [], [], [], []
    for c, rows in enumerate(chunks):
        q, k, bf, bb = q_all[rows], k_all[rows], b_f[rows], b_b[rows]
        edge_f = bf[CHUNK - 1:CHUNK]
        edge_b = bb[0:1]
        q_f.append((q * jnp.exp(bf)).astype(BF16))
        q_b.append((q * jnp.exp(bb)).astype(BF16))
        k_f.append((k * jnp.exp(-bf)).astype(BF16))
        k_b.append((k * jnp.exp(-bb)).astype(BF16))
        qf_ref[0, rows, :] = q_f[c]
        qb_ref[0, rows, :] = q_b[c]
        kef_ref[0, rows, :] = (k * jnp.exp(edge_f - bf)).astype(BF16)
        keb_ref[0, rows, :] = (k * jnp.exp(edge_b - bb)).astype(BF16)
        decf_ref[0, c:c + 1, :] = jnp.exp(edge_f)
        decb_ref[0, c:c + 1, :] = jnp.exp(edge_b)

    gp_ref[0] = _sigmoid(gp_raw).astype(BF16)
    arow = lax.broadcasted_iota(jnp.int32, (CHUNK, GLA_KW), 0)
    acol = lax.broadcasted_iota(jnp.int32, (CHUNK, GLA_KW), 1) % CHUNK
    causal = acol <= arow
    att = [jnp.where(causal, _dot_nt(q_f[c], _head_blocks(k_f[c])), _dot_nt(q_b[c], _head_blocks(k_b[c])))
           .astype(BF16) for c in range(len(chunks))]
    gg_ref[0] = _sigmoid(gg_raw).astype(BF16)

    v_head = lax.broadcasted_iota(jnp.int32, (CHUNK, GLA_VW), 1) // GLA_DV
    zero = jnp.zeros((), BF16)
    for c, rows in enumerate(chunks):
        v_bf = v_all[rows].astype(BF16)
        v_blk = jnp.concatenate([jnp.where(v_head == hd, v_bf, zero) for hd in range(GLA_HEADS)], axis=0)
        oi_ref[0, rows, :] = _dot(att[c], v_blk)
    r_ref[0] = _silu(r_raw).astype(BF16)

    for c, rows in enumerate(chunks):
        v = v_all[rows]
        v_stack = jnp.concatenate([v[:, hd * GLA_DV:(hd + 1) * GLA_DV] for hd in range(GLA_HEADS)], axis=0)
        vst_ref[0, c] = v_stack.T.astype(BF16)


def _proj(x, mod, nw, lw):
    b, t, d = x.shape
    tm = _token_tile(t)
    nc, tc = t // CHUNK, tm // CHUNK
    tok = lambda w: pl.BlockSpec((1, tm, w), lambda b, i: (b, i, 0))
    dec = pl.BlockSpec((1, tc, GLA_KW), lambda b, i: (b, i, 0))
    outs = [
        ('qf', (b, t, GLA_KW), BF16, tok(GLA_KW)), ('kef', (b, t, GLA_KW), BF16, tok(GLA_KW)),
        ('decf', (b, nc, GLA_KW), F32, dec),
        ('qb', (b, t, GLA_KW), BF16, tok(GLA_KW)), ('keb', (b, t, GLA_KW), BF16, tok(GLA_KW)),
        ('decb', (b, nc, GLA_KW), F32, dec),
        ('vst', (b, nc, GLA_DV, GLA_KW), BF16, pl.BlockSpec((1, tc, GLA_DV, GLA_KW), lambda b, i: (b, i, 0, 0))),
        ('oi', (b, t, GLA_VW), F32, tok(GLA_VW)),
        ('r', (b, t, GLA_VW), BF16, tok(GLA_VW)), ('u', (b, t, POOL_WIDTH), F32, tok(POOL_WIDTH)),
        ('gp', (b, t, d), BF16, tok(d)), ('gg', (b, t, d), BF16, tok(d)),
    ]
    weights = [lw[n] for n in _PROJ_WEIGHTS]
    res = pl.pallas_call(
        _proj_kernel,
        grid=(b, t // tm),
        in_specs=[tok(d), pl.BlockSpec((1, N_MOD, d), _per_batch_or_shared(mod)), _resident((1, d))]
        + [_resident(w.shape) for w in weights] + [_resident(lw['b_gate'].shape)],
        out_specs=[o[3] for o in outs],
        out_shape=[jax.ShapeDtypeStruct(o[1], o[2]) for o in outs],
        compiler_params=_params("parallel", "parallel"),
        name="mixer_proj",
    )(x, mod, nw, *weights, lw['b_gate'])
    return {o[0]: a for o, a in zip(outs, res)}


def _gla_kernel(qf_ref, kef_ref, decf_ref, qb_ref, keb_ref, decb_ref, vst_ref, oi_ref, r_ref,
                sf0_ref, sb0_ref, nw_ref, og_ref, sf_ref, sb_ref, st_ref, ob_ref, *, n_blocks, n_chunks):
    step = pl.program_id(1)
    tb = n_chunks * CHUNK

    def chunk(q_ref, ke_ref, dec_ref, c):
        rows = pl.ds(c * CHUNK, CHUNK)
        st = st_ref[...]
        o_stack = _dot_nt(_head_blocks(q_ref[0, rows, :]), st.astype(BF16))
        st_ref[...] = st * dec_ref[0, c:c + 1, :] + _dot(vst_ref[0, c], _head_blocks(ke_ref[0, rows, :]))
        return jnp.concatenate([o_stack[h * CHUNK:(h + 1) * CHUNK] for h in range(GLA_HEADS)], axis=1)

    @pl.when(step == 0)
    def _():
        st_ref[...] = sb0_ref[0]

    @pl.when(step < n_blocks)
    def _():
        base = (n_blocks - 1 - step) * tb
        for c in reversed(range(n_chunks)):
            o = chunk(qb_ref, keb_ref, decb_ref, c)
            ob_ref[pl.ds(pl.multiple_of(base + c * CHUNK, CHUNK), CHUNK), :] = o

    @pl.when(step == n_blocks - 1)
    def _():
        sb_ref[0] = st_ref[...]

    @pl.when(step == n_blocks)
    def _():
        st_ref[...] = sf0_ref[0]

    @pl.when(step >= n_blocks)
    def _():
        base = (step - n_blocks) * tb
        nw = nw_ref[...]
        for c in range(n_chunks):
            rows = pl.ds(c * CHUNK, CHUNK)
            o = chunk(qf_ref, kef_ref, decf_ref, c)
            o = o + ob_ref[pl.ds(pl.multiple_of(base + c * CHUNK, CHUNK), CHUNK), :] + oi_ref[0, rows, :]
            o = jnp.concatenate(
                [_rms(o[:, h * GLA_DV:(h + 1) * GLA_DV]) * nw for h in range(GLA_HEADS)], axis=1)
            og_ref[0, rows, :] = (o * r_ref[0, rows, :].astype(F32)).astype(og_ref.dtype)

    @pl.when(step == 2 * n_blocks - 1)
    def _():
        sf_ref[0] = st_ref[...]


def _gla(p, r, s_f0, s_b0, norm_w):
    b, t, _ = p['qf'].shape
    tb = _token_tile(t, SCAN_TOKEN_TILE)
    nb = t // tb
    tc = tb // CHUNK
    fwd = lambda b, s: (b, jnp.maximum(s - nb, 0), 0)
    rev = lambda b, s: (b, nb - 1 - jnp.minimum(s, nb - 1), 0)
    both = lambda b, s: (b, jnp.where(s < nb, nb - 1 - s, s - nb), 0, 0)
    tok = lambda w, m: pl.BlockSpec((1, tb, w), m)
    dec = lambda m: pl.BlockSpec((1, tc, GLA_KW), m)
    state = pl.BlockSpec((1, GLA_DV, GLA_KW), lambda b, s: (b, 0, 0))
    kern = functools.partial(_gla_kernel, n_blocks=nb, n_chunks=tc)
    return pl.pallas_call(
        kern,
        grid=(b, 2 * nb),
        in_specs=[tok(GLA_KW, fwd), tok(GLA_KW, fwd), dec(fwd), tok(GLA_KW, rev), tok(GLA_KW, rev), dec(rev),
                  pl.BlockSpec((1, tc, GLA_DV, GLA_KW), both), tok(GLA_VW, fwd), tok(GLA_VW, fwd),
                  state, state, _resident((1, GLA_DV))],
        out_specs=[tok(GLA_VW, fwd), state, state],
        out_shape=[jax.ShapeDtypeStruct((b, t, GLA_VW), BF16),
                   jax.ShapeDtypeStruct((b, GLA_DV, GLA_KW), F32),
                   jax.ShapeDtypeStruct((b, GLA_DV, GLA_KW), F32)],
        scratch_shapes=[pltpu.VMEM((GLA_DV, GLA_KW), F32), pltpu.VMEM((t, GLA_VW), F32)],
        compiler_params=_params("arbitrary", "arbitrary"),
        name="gla_scan",
    )(p['qf'], p['kef'], p['decf'], p['qb'], p['keb'], p['decb'], p['vst'], p['oi'], r, s_f0, s_b0, norm_w)


def _col_box_sum(x, n_cols, w):
    half = w // 2
    c_out = lax.broadcasted_iota(jnp.int32, (n_cols, 2 * n_cols), 0)
    c_in = lax.broadcasted_iota(jnp.int32, (n_cols, 2 * n_cols), 1) % n_cols
    band = jnp.where((c_in >= c_out - half) & (c_in < c_out + half), 1.0, 0.0).astype(BF16)
    hi = x.astype(BF16)
    lo = (x - hi.astype(F32)).astype(BF16)
    out = []
    for r in range(x.shape[0] // n_cols):
        rows = slice(r * n_cols, (r + 1) * n_cols)
        out.append(_dot(band, jnp.concatenate([hi[rows], lo[rows]], axis=0)))
    return jnp.concatenate(out, axis=0)


def _row_box_sum(u_ref, lanes, tile, n_tiles, tm, n_cols, w):
    half = w // 2
    halo = half * n_cols
    assert halo <= tm and tm % n_cols == 0
    cur = u_ref[0, pl.ds(pl.multiple_of(tile * tm, tm), tm), lanes]
    before = jnp.maximum(tile - 1, 0) * tm + (tm - halo)
    after = jnp.minimum(tile + 1, n_tiles - 1) * tm
    prev = u_ref[0, pl.ds(pl.multiple_of(before, n_cols), halo), lanes]
    nxt = u_ref[0, pl.ds(pl.multiple_of(after, n_cols), halo), lanes]
    win = jnp.concatenate([jnp.where(tile > 0, prev, 0.0), cur, jnp.where(tile < n_tiles - 1, nxt, 0.0)],
                          axis=0)
    span = 1
    while span < w:
        win = win[:-span * n_cols] + win[span * n_cols:]
        span *= 2
    return win[:tm]


def _merge_kernel(x_ref, mod_ref, og_ref, u_ref, gp_ref, gg_ref, wbg_ref, wbp_ref, wo_ref, pw_ref, ps_ref,
                  o_ref, *, n_rows, n_cols, n_tiles):
    tile = pl.program_id(1)
    tm = x_ref.shape[1]
    y_gla = _dot(og_ref[0], wbg_ref[...])
    tok = lax.broadcasted_iota(jnp.int32, (tm, POOL_GROUP), 0)
    col = (tok % n_cols).astype(F32)
    row = (tile * (tm // n_cols) + tok // n_cols).astype(F32)
    mixed = []
    for g, w in enumerate(POOL_WINDOWS):
        half = float(w // 2)
        lanes = slice(g * POOL_GROUP, (g + 1) * POOL_GROUP)
        u = u_ref[0, pl.ds(pl.multiple_of(tile * tm, tm), tm), lanes]
        cnt = jnp.minimum(col, half) + jnp.minimum(n_cols - col, half)
        m = u
        if n_rows > 1:
            cnt = cnt * (jnp.minimum(row, half) + jnp.minimum(n_rows - row, half))
            m = _row_box_sum(u_ref, lanes, tile, n_tiles, tm, n_cols, w)
        m = _col_box_sum(m, n_cols, w) / cnt
        mixed.append((_dot((m - u).astype(BF16), pw_ref[lanes, :]) * ps_ref[:, lanes]).astype(BF16))
    y_pool = _dot(jnp.concatenate(mixed, axis=1), wbp_ref[...])
    merged = gp_ref[0].astype(F32) * y_pool + gg_ref[0].astype(F32) * y_gla
    gate = mod_ref[0][5:6]
    o_ref[0] = x_ref[0] + gate * _dot(merged.astype(BF16), wo_ref[...])


def _merge(x, mod, og, u, gp, gg, lw, *, n_rows, n_cols):
    b, t, d = x.shape
    tm = _token_tile(t, MERGE_TOKEN_TILE)
    tok = lambda w: pl.BlockSpec((1, tm, w), lambda b, i: (b, i, 0))
    whole = pl.BlockSpec((1, t, POOL_WIDTH), lambda b, i: (b, 0, 0), pipeline_mode=pl.Buffered(1))
    weights = [lw['w_br_gla'], lw['w_br_pool'], lw['w_out'], lw['pool_w']]
    kern = functools.partial(_merge_kernel, n_rows=n_rows, n_cols=n_cols, n_tiles=t // tm)
    return pl.pallas_call(
        kern,
        grid=(b, t // tm),
        in_specs=[tok(d), pl.BlockSpec((1, N_MOD, d), _per_batch_or_shared(mod)),
                  tok(GLA_VW), whole, tok(d), tok(d)] + [_resident(w.shape) for w in weights]
        + [_resident(lw['pool_scale'].shape)],
        out_specs=tok(d),
        out_shape=jax.ShapeDtypeStruct(x.shape, F32),
        compiler_params=_params("parallel", "parallel"),
        name="mixer_merge",
    )(x, mod, og, u, gp, gg, *weights, lw['pool_scale'])


_HALF_D = D_MODEL // 2
_W_IN_PARTS = (('w_k', OFF_K, GLA_KW), ('w_v', OFF_V, GLA_VW), ('w_q', OFF_Q, GLA_KW), ('w_r', OFF_R, GLA_VW),
               ('w_u', OFF_POOL, POOL_WIDTH), ('w_gp0', OFF_GP, _HALF_D), ('w_gp1', OFF_GP + _HALF_D, _HALF_D),
               ('w_gg0', OFF_GG, _HALF_D), ('w_gg1', OFF_GG + _HALF_D, _HALF_D))
_MIXER_MATS = ('w_br_gla', 'w_br_pool', 'w_out', 'pool_w')


def _mixer_cast_jobs(l, w_in_t, w_br_gla, w_br_pool, w_out, pool_w):
    whole = lambda w: (w, l, 0, w.shape[1])
    return ([(w_in_t, l, start, n) for _, start, n in _W_IN_PARTS]
            + [whole(w_br_gla), whole(w_br_pool), whole(w_out), whole(pool_w.reshape(DEPTH, POOL_WIDTH, POOL_GROUP))])


def _mixer(x, mod, nw, lw, s_f0, s_b0, gla_norm_w, *, n_rows, n_cols, states_only=False):
    p = _proj(x, mod, nw, lw)
    og, s_f, s_b = _gla(p, p['r'], s_f0, s_b0, gla_norm_w)
    if states_only:
        return None, s_f, s_b
    return _merge(x, mod, og, p['u'], p['gp'], p['gg'], lw, n_rows=n_rows, n_cols=n_cols), s_f, s_b


def kernel(x, c, ctx, c_ctx, w_mod, b_mod, norm_w, ffn1_up, ffn1_down, w_in, w_af_up, b_af, w_ab_up, b_ab,
           gla_norm_w, pool_w, pool_scale, w_br_pool, w_br_gla, w_out, ffn2_up, ffn2_down, final_norm_w):
    batch, seq, d = x.shape
    ctx_len = ctx.shape[1]
    cond = jnp.concatenate([c, c_ctx[None, :], jnp.zeros((SUBLANES - batch - 1, d), F32)], axis=0)
    mod_all = _modulation(cond, w_mod, b_mod)
    zero_state = jnp.zeros((batch, GLA_DV, GLA_KW), F32)
    fw = final_norm_w[None, :]
    zeros = jnp.zeros((GATE_RANK, GLA_KW), F32)
    up1, down1 = ffn1_up[0].astype(BF16), ffn1_down[0].astype(BF16)
    whole = lambda w, l: (w, l, 0, w.shape[1])
    w_in_t = jnp.swapaxes(w_in, 1, 2)
    for l in range(DEPTH):
        last = l == DEPTH - 1
        mod_x = mod_all[l, :batch].reshape(batch, N_MOD, d)
        mod_c = mod_all[l, batch:batch + 1].reshape(1, N_MOD, d)
        nw = norm_w[l][:, None, :]
        gnw = gla_norm_w[l][None, :]
        lw = {
            'w_gate': jnp.concatenate([jnp.concatenate([w_af_up[l], zeros], axis=1),
                                       jnp.concatenate([zeros, w_ab_up[l]], axis=1)], axis=0).astype(BF16),
            'b_gate': jnp.concatenate([b_af[l], b_ab[l]])[None, :],
            'pool_scale': pool_scale[l][None, :],
            'w_a': w_in_t[l, OFF_AF:OFF_Q],
        }

        x, cast = _ffn(x, mod_x, nw[0], up1, down1, fw, mod_row=0, final_norm=False,
                       cast=[whole(ffn2_up, l), whole(ffn2_down, l)]
                       + _mixer_cast_jobs(l, w_in_t, w_br_gla, w_br_pool, w_out, pool_w))
        up2, down2 = cast[:2]
        lw.update(zip([n for n, _, _ in _W_IN_PARTS] + list(_MIXER_MATS), cast[2:]))
        ctx, _ = _ffn(ctx, mod_c, nw[0], up1, down1, fw, mod_row=0, final_norm=False)

        ctx_mixed, s_f, s_b = _mixer(ctx, mod_c, nw[1], lw, zero_state, zero_state, gnw,
                                     n_rows=1, n_cols=ctx_len, states_only=last)
        x, _, _ = _mixer(x, mod_x, nw[1], lw, s_f, s_b, gnw, n_rows=seq // GRID_W, n_cols=GRID_W)
        x, cast = _ffn(x, mod_x, nw[2], up2, down2, fw, mod_row=6, final_norm=last,
                       cast=[] if last else [whole(ffn1_up, l + 1), whole(ffn1_down, l + 1)])
        if not last:
            ctx, _ = _ffn(ctx_mixed, mod_c, nw[2], up2, down2, fw, mod_row=6, final_norm=False)
            up1, down1 = cast
    return x
```

```python
import functools

import jax
import jax.numpy as jnp
from jax import lax
from jax.experimental import pallas as pl
from jax.experimental.pallas import tpu as pltpu

F32 = jnp.float32
BF16 = jnp.bfloat16

D_MODEL = 1024
DEPTH = 2
GRID_W = 64
N_POOL_GROUPS = 4
POOL_GROUP = 128
POOL_WIDTH = N_POOL_GROUPS * POOL_GROUP
POOL_WINDOWS = (2, 4, 8, 16)
GLA_HEADS = 4
GLA_DK = 64
GLA_DV = 128
GLA_KW = GLA_HEADS * GLA_DK
GLA_VW = GLA_HEADS * GLA_DV
GATE_RANK = 16
GATE_TAU = 16.0
CHUNK = 64
D_FF = 2816
N_MOD = 9
EPS = 1e-6

OFF_K = 0
OFF_V = OFF_K + GLA_KW
OFF_AF = OFF_V + GLA_VW
OFF_AB = OFF_AF + GATE_RANK
OFF_Q = OFF_AB + GATE_RANK
OFF_R = OFF_Q + GLA_KW
OFF_POOL = OFF_R + GLA_VW
OFF_GP = OFF_POOL + POOL_WIDTH
OFF_GG = OFF_GP + D_MODEL
IN_COLS = OFF_GG + D_MODEL

V7X_VMEM_BYTES = 64 * 1024 * 1024
VMEM_LIMIT_BYTES = 56 * 1024 * 1024
SUBLANES = 8
TOKEN_TILE = 1024
FFN_TOKEN_TILE = 1024
SCAN_TOKEN_TILE = 1024
MERGE_TOKEN_TILE = 1024
FF_CHUNK = 256
NORM_PART = 256
PROJ_NORM_PART = 512
PROJ_ROW_PIECE = 128
PROJ_COL_PIECE = 256
MOD_COL_TILE = 1152


def _resident(shape):
    return pl.BlockSpec(shape, lambda *_: (0,) * len(shape), pipeline_mode=pl.Buffered(1))


def _params(*semantics):
    return pltpu.CompilerParams(dimension_semantics=semantics, vmem_limit_bytes=VMEM_LIMIT_BYTES)


def _per_batch_or_shared(arr):
    zeros = (0,) * (arr.ndim - 1)
    if arr.shape[0] == 1:
        return lambda b, i: (0,) + zeros
    return lambda b, i: (b,) + zeros


def _token_tile(t, tile=TOKEN_TILE):
    return min(tile, t)


def _rms(x):
    return x * lax.rsqrt(jnp.mean(x * x, axis=-1, keepdims=True) + EPS)


def _modulated_norm(x, w, shift, scale):
    return (_rms(x) * w) * (1.0 + scale) + shift


def _silu(a):
    return a / (1.0 + jnp.exp(-a))


def _sigmoid(a):
    return 1.0 / (1.0 + jnp.exp(-a))


def _dot(a, b):
    return jnp.dot(a, b, preferred_element_type=F32)


def _dot_nt(a, b):
    return lax.dot_general(a, b, (((1,), (1,)), ((), ())), preferred_element_type=F32)


def _head_blocks(x):
    head = lax.broadcasted_iota(jnp.int32, x.shape, 1) // GLA_DK
    zero = jnp.zeros((), x.dtype)
    return jnp.concatenate([jnp.where(head == h, x, zero) for h in range(GLA_HEADS)], axis=0)


def _chunk_cumsum(g, *, reverse):
    n = g.shape[0]
    pos = lax.broadcasted_iota(jnp.int32, g.shape, 0) % CHUNK
    acc = g
    span = 1
    while span < CHUNK:
        if reverse:
            acc = acc + jnp.where(pos + span < CHUNK, pltpu.roll(acc, n - span, axis=0), 0.0)
        else:
            acc = acc + jnp.where(pos >= span, pltpu.roll(acc, span, axis=0), 0.0)
        span *= 2
    return acc


def _mod_kernel(cond_ref, w_ref, b_ref, o_ref):
    cond = cond_ref[...]
    o_ref[0] = _dot(_silu(cond).astype(BF16), w_ref[0].astype(BF16)) + b_ref[0]


def _modulation(cond, w_mod, b_mod):
    n_cols = N_MOD * D_MODEL
    rows = cond.shape[0]
    return pl.pallas_call(
        _mod_kernel,
        grid=(DEPTH, n_cols // MOD_COL_TILE),
        in_specs=[
            pl.BlockSpec((rows, D_MODEL), lambda l, j: (0, 0)),
            pl.BlockSpec((1, D_MODEL, MOD_COL_TILE), lambda l, j: (l, 0, j)),
            pl.BlockSpec((1, 1, MOD_COL_TILE), lambda l, j: (l, 0, j)),
        ],
        out_specs=pl.BlockSpec((1, rows, MOD_COL_TILE), lambda l, j: (l, 0, j)),
        out_shape=jax.ShapeDtypeStruct((DEPTH, rows, n_cols), F32),
        compiler_params=_params("arbitrary", "arbitrary"),
        name="modulation",
    )(cond, w_mod, b_mod.reshape(DEPTH, 1, n_cols))


def _ffn_kernel(*refs, mod_row, final_norm, n_cast):
    x_ref, mod_ref, nw_ref, up_ref, down_ref, fw_ref = refs[:6]
    cast_in = refs[6:6 + n_cast]
    o_ref = refs[6 + n_cast]
    cast_out = refs[7 + n_cast:7 + 2 * n_cast]
    act_ref = refs[7 + 2 * n_cast]
    x = x_ref[0]
    mod = mod_ref[0]
    shift = mod[mod_row:mod_row + 1]
    scale = mod[mod_row + 1:mod_row + 2]
    gate = mod[mod_row + 2:mod_row + 3]
    parts = [slice(r, r + NORM_PART) for r in range(0, x.shape[0], min(NORM_PART, x.shape[0]))]
    h_parts, a_parts, b_parts = [], [], []
    for p in parts:
        h_parts.append(_modulated_norm(x[p], nw_ref[...], shift, scale).astype(BF16))
        a_parts.append(_dot(h_parts[-1], up_ref[:, :FF_CHUNK]))
        b_parts.append(_dot(h_parts[-1], up_ref[:, D_FF:D_FF + FF_CHUNK]))
    h = jnp.concatenate(h_parts, axis=0)
    n_chunks = D_FF // FF_CHUNK
    for j in range(n_chunks):
        lo = j * FF_CHUNK
        if j == 0:
            a = jnp.concatenate(a_parts, axis=0)
            b = jnp.concatenate(b_parts, axis=0)
        else:
            a = _dot(h, up_ref[:, lo:lo + FF_CHUNK])
            b = _dot(h, up_ref[:, D_FF + lo:D_FF + lo + FF_CHUNK])
        act_ref[:, lo:lo + FF_CHUNK] = (_silu(a) * b).astype(BF16)
        for k in range(j, n_cast, n_chunks):
            cast_out[k][...] = cast_in[k][0].astype(BF16)
    y = x + (0.5 * gate) * _dot(act_ref[...], down_ref[...])
    if final_norm:
        y = _rms(y) * fw_ref[...]
    o_ref[0] = y


def _ffn(x, mod, nw, up, down, fw, *, mod_row, final_norm, cast=()):
    b, t, d = x.shape
    tm = _token_tile(t, FFN_TOKEN_TILE)
    nt = t // tm
    steps = b * nt
    cast_specs, out_specs, out_shapes = [], [], []
    for w, l, start, n_rows in cast:
        rows = n_rows // steps
        assert rows * steps == n_rows and rows % (2 * SUBLANES) == 0 and start % rows == 0, (w.shape, start)
        cast_specs.append(pl.BlockSpec((1, rows, w.shape[2]),
                                       lambda b, i, l=l, first=start // rows: (l, first + b * nt + i, 0)))
        out_specs.append(pl.BlockSpec((rows, w.shape[2]), lambda b, i: (b * nt + i, 0)))
        out_shapes.append(jax.ShapeDtypeStruct((n_rows, w.shape[2]), BF16))
    kern = functools.partial(_ffn_kernel, mod_row=mod_row, final_norm=final_norm, n_cast=len(cast))
    res = pl.pallas_call(
        kern,
        grid=(b, nt),
        in_specs=[
            pl.BlockSpec((1, tm, d), lambda b, i: (b, i, 0)),
            pl.BlockSpec((1, N_MOD, d), _per_batch_or_shared(mod)),
            _resident((1, d)),
            _resident(up.shape),
            _resident(down.shape),
            _resident((1, d)),
        ] + cast_specs,
        out_specs=[pl.BlockSpec((1, tm, d), lambda b, i: (b, i, 0))] + out_specs,
        out_shape=[jax.ShapeDtypeStruct(x.shape, F32)] + out_shapes,
        scratch_shapes=[pltpu.VMEM((tm, D_FF), BF16)],
        compiler_params=_params("parallel", "parallel"),
        name="ffn",
    )(x, mod, nw, up, down, fw, *[job[0] for job in cast])
    return res[0], list(res[1:])


_PROJ_WEIGHTS = ('w_k', 'w_v', 'w_a', 'w_q', 'w_r', 'w_u', 'w_gp0', 'w_gp1', 'w_gg0', 'w_gg1', 'w_gate')


def _proj_kernel(x_ref, mod_ref, nw_ref, wk_ref, wv_ref, wa_ref, wq_ref, wr_ref, wu_ref, wgp0_ref, wgp1_ref,
                 wgg0_ref, wgg1_ref, wgate_ref, bgate_ref,
                 qf_ref, kef_ref, decf_ref, qb_ref, keb_ref, decb_ref, vst_ref, oi_ref,
                 r_ref, u_ref, gp_ref, gg_ref):
    x = x_ref[0]
    mod = mod_ref[0]
    tm = x.shape[0]
    chunks = [slice(c * CHUNK, (c + 1) * CHUNK) for c in range(tm // CHUNK)]
    h_parts, k_parts, q_parts, v_parts = [], [], [], []
    for r in range(0, tm, min(PROJ_NORM_PART, tm)):
        hp = _modulated_norm(x[r:r + PROJ_NORM_PART], nw_ref[...], mod[3:4], mod[4:5]).astype(BF16)
        h_parts.append(hp)
        k_parts.append(_dot_nt(hp, wk_ref[...]))
        q_parts.append(_dot_nt(hp, wq_ref[...]))
        v_parts.append(_dot_nt(hp, wv_ref[...]))
    h = jnp.concatenate(h_parts, axis=0)
    k_all = jnp.concatenate(k_parts, axis=0)
    q_all = jnp.concatenate(q_parts, axis=0) * (GLA_DK ** -0.5)
    v_all = jnp.concatenate(v_parts, axis=0)
    a_low = _dot_nt(h, wa_ref[...].astype(BF16)).astype(BF16)
    z = _dot(a_low, wgate_ref[...]) + bgate_ref[...]
    todo = [(name, ref, lo) for name, ref in (('u', wu_ref), ('gp', wgp0_ref), ('gp', wgp1_ref), ('gg', wgg0_ref),
                                              ('gg', wgg1_ref), ('r', wr_ref))
            for lo in range(0, ref.shape[0], PROJ_COL_PIECE)]
    raw = {'u': [], 'gp': [], 'gg': [], 'r': []}
    q_f, q_b, k_f, k_b = [], [], [], []

    def decay_piece(r0):
        zp = z[r0:r0 + PROJ_ROW_PIECE]
        g = (jnp.minimum(zp, 0.0) - jnp.log(1.0 + jnp.exp(-jnp.abs(zp)))) / GATE_TAU
        b_f = _chunk_cumsum(g[:, :GLA_KW], reverse=False)
        b_b = _chunk_cumsum(g[:, GLA_KW:], reverse=True)
        for c in range(r0 // CHUNK, (r0 + PROJ_ROW_PIECE) // CHUNK):
            rows = chunks[c]
            local = slice(c * CHUNK - r0, (c + 1) * CHUNK - r0)
            q, k, bf, bb = q_all[rows], k_all[rows], b_f[local], b_b[local]
            edge_f = bf[CHUNK - 1:CHUNK]
            edge_b = bb[0:1]
            q_f.append((q * jnp.exp(bf)).astype(BF16))
            q_b.append((q * jnp.exp(bb)).astype(BF16))
            k_f.append((k * jnp.exp(-bf)).astype(BF16))
            k_b.append((k * jnp.exp(-bb)).astype(BF16))
            qf_ref[0, rows, :] = q_f[c]
            qb_ref[0, rows, :] = q_b[c]
            kef_ref[0, rows, :] = (k * jnp.exp(edge_f - bf)).astype(BF16)
            keb_ref[0, rows, :] = (k * jnp.exp(edge_b - bb)).astype(BF16)
            decf_ref[0, c:c + 1, :] = jnp.exp(edge_f)
            decb_ref[0, c:c + 1, :] = jnp.exp(edge_b)

    row_pieces = list(range(0, tm, min(PROJ_ROW_PIECE, tm)))
    for i in range(max(len(todo), len(row_pieces))):
        if i < len(todo):
            name, ref, lo = todo[i]
            raw[name].append(_dot_nt(h, ref[lo:lo + PROJ_COL_PIECE, :]))
        if i < len(row_pieces):
            decay_piece(row_pieces[i])
    u_ref[0] = jnp.concatenate(raw['u'], axis=1)
    gp_raw = jnp.concatenate(raw['gp'], axis=1)
    gg_raw = jnp.concatenate(raw['gg'], axis=1)
    r_raw = jnp.concatenate(raw['r'], axis=1)

    gp_ref[0] = _sigmoid(gp_raw).astype(BF16)
    arow = lax.broadcasted_iota(jnp.int32, (CHUNK, GLA_KW), 0)
    acol = lax.broadcasted_iota(jnp.int32, (CHUNK, GLA_KW), 1) % CHUNK
    causal = acol <= arow
    att = [jnp.where(causal, _dot_nt(q_f[c], _head_blocks(k_f[c])), _dot_nt(q_b[c], _head_blocks(k_b[c])))
           .astype(BF16) for c in range(len(chunks))]
    gg_ref[0] = _sigmoid(gg_raw).astype(BF16)

    v_head = lax.broadcasted_iota(jnp.int32, (CHUNK, GLA_VW), 1) // GLA_DV
    zero = jnp.zeros((), BF16)
    for c, rows in enumerate(chunks):
        v_bf = v_all[rows].astype(BF16)
        v_blk = jnp.concatenate([jnp.where(v_head == hd, v_bf, zero) for hd in range(GLA_HEADS)], axis=0)
        oi_ref[0, rows, :] = _dot(att[c], v_blk)
    r_ref[0] = _silu(r_raw).astype(BF16)

    for c, rows in enumerate(chunks):
        v = v_all[rows]
        v_stack = jnp.concatenate([v[:, hd * GLA_DV:(hd + 1) * GLA_DV] for hd in range(GLA_HEADS)], axis=0)
        vst_ref[0, c] = v_stack.T.astype(BF16)


def _proj(x, mod, nw, lw):
    b, t, d = x.shape
    tm = _token_tile(t)
    nc, tc = t // CHUNK, tm // CHUNK
    tok = lambda w: pl.BlockSpec((1, tm, w), lambda b, i: (b, i, 0))
    dec = pl.BlockSpec((1, tc, GLA_KW), lambda b, i: (b, i, 0))
    outs = [
        ('qf', (b, t, GLA_KW), BF16, tok(GLA_KW)), ('kef', (b, t, GLA_KW), BF16, tok(GLA_KW)),
        ('decf', (b, nc, GLA_KW), F32, dec),
        ('qb', (b, t, GLA_KW), BF16, tok(GLA_KW)), ('keb', (b, t, GLA_KW), BF16, tok(GLA_KW)),
        ('decb', (b, nc, GLA_KW), F32, dec),
        ('vst', (b, nc, GLA_DV, GLA_KW), BF16, pl.BlockSpec((1, tc, GLA_DV, GLA_KW), lambda b, i: (b, i, 0, 0))),
        ('oi', (b, t, GLA_VW), F32, tok(GLA_VW)),
        ('r', (b, t, GLA_VW), BF16, tok(GLA_VW)), ('u', (b, t, POOL_WIDTH), F32, tok(POOL_WIDTH)),
        ('gp', (b, t, d), BF16, tok(d)), ('gg', (b, t, d), BF16, tok(d)),
    ]
    weights = [lw[n] for n in _PROJ_WEIGHTS]
    res = pl.pallas_call(
        _proj_kernel,
        grid=(b, t // tm),
        in_specs=[tok(d), pl.BlockSpec((1, N_MOD, d), _per_batch_or_shared(mod)), _resident((1, d))]
        + [_resident(w.shape) for w in weights] + [_resident(lw['b_gate'].shape)],
        out_specs=[o[3] for o in outs],
        out_shape=[jax.ShapeDtypeStruct(o[1], o[2]) for o in outs],
        compiler_params=_params("parallel", "parallel"),
        name="mixer_proj",
    )(x, mod, nw, *weights, lw['b_gate'])
    return {o[0]: a for o, a in zip(outs, res)}


def _gla_kernel(qf_ref, kef_ref, decf_ref, qb_ref, keb_ref, decb_ref, vst_ref, oi_ref, r_ref,
                sf0_ref, sb0_ref, nw_ref, og_ref, sf_ref, sb_ref, st_ref, ob_ref, *, n_blocks, n_chunks):
    step = pl.program_id(1)
    tb = n_chunks * CHUNK

    def chunk(q_ref, ke_ref, dec_ref, c):
        rows = pl.ds(c * CHUNK, CHUNK)
        st = st_ref[...]
        o_stack = _dot_nt(_head_blocks(q_ref[0, rows, :]), st.astype(BF16))
        st_ref[...] = st * dec_ref[0, c:c + 1, :] + _dot(vst_ref[0, c], _head_blocks(ke_ref[0, rows, :]))
        return jnp.concatenate([o_stack[h * CHUNK:(h + 1) * CHUNK] for h in range(GLA_HEADS)], axis=1)

    @pl.when(step == 0)
    def _():
        st_ref[...] = sb0_ref[0]

    @pl.when(step < n_blocks)
    def _():
        base = (n_blocks - 1 - step) * tb
        for c in reversed(range(n_chunks)):
            o = chunk(qb_ref, keb_ref, decb_ref, c)
            ob_ref[pl.ds(pl.multiple_of(base + c * CHUNK, CHUNK), CHUNK), :] = o

    @pl.when(step == n_blocks - 1)
    def _():
        sb_ref[0] = st_ref[...]

    @pl.when(step == n_blocks)
    def _():
        st_ref[...] = sf0_ref[0]

    @pl.when(step >= n_blocks)
    def _():
        base = (step - n_blocks) * tb
        nw = nw_ref[...]
        for c in range(n_chunks):
            rows = pl.ds(c * CHUNK, CHUNK)
            o = chunk(qf_ref, kef_ref, decf_ref, c)
            o = o + ob_ref[pl.ds(pl.multiple_of(base + c * CHUNK, CHUNK), CHUNK), :] + oi_ref[0, rows, :]
            o = jnp.concatenate(
                [_rms(o[:, h * GLA_DV:(h + 1) * GLA_DV]) * nw for h in range(GLA_HEADS)], axis=1)
            og_ref[0, rows, :] = (o * r_ref[0, rows, :].astype(F32)).astype(og_ref.dtype)

    @pl.when(step == 2 * n_blocks - 1)
    def _():
        sf_ref[0] = st_ref[...]


def _gla(p, r, s_f0, s_b0, norm_w):
    b, t, _ = p['qf'].shape
    tb = _token_tile(t, SCAN_TOKEN_TILE)
    nb = t // tb
    tc = tb // CHUNK
    fwd = lambda b, s: (b, jnp.maximum(s - nb, 0), 0)
    rev = lambda b, s: (b, nb - 1 - jnp.minimum(s, nb - 1), 0)
    both = lambda b, s: (b, jnp.where(s < nb, nb - 1 - s, s - nb), 0, 0)
    tok = lambda w, m: pl.BlockSpec((1, tb, w), m)
    dec = lambda m: pl.BlockSpec((1, tc, GLA_KW), m)
    state = pl.BlockSpec((1, GLA_DV, GLA_KW), lambda b, s: (b, 0, 0))
    kern = functools.partial(_gla_kernel, n_blocks=nb, n_chunks=tc)
    return pl.pallas_call(
        kern,
        grid=(b, 2 * nb),
        in_specs=[tok(GLA_KW, fwd), tok(GLA_KW, fwd), dec(fwd), tok(GLA_KW, rev), tok(GLA_KW, rev), dec(rev),
                  pl.BlockSpec((1, tc, GLA_DV, GLA_KW), both), tok(GLA_VW, fwd), tok(GLA_VW, fwd),
                  state, state, _resident((1, GLA_DV))],
        out_specs=[tok(GLA_VW, fwd), state, state],
        out_shape=[jax.ShapeDtypeStruct((b, t, GLA_VW), BF16),
                   jax.ShapeDtypeStruct((b, GLA_DV, GLA_KW), F32),
                   jax.ShapeDtypeStruct((b, GLA_DV, GLA_KW), F32)],
        scratch_shapes=[pltpu.VMEM((GLA_DV, GLA_KW), F32), pltpu.VMEM((t, GLA_VW), F32)],
        compiler_params=_params("arbitrary", "arbitrary"),
        name="gla_scan",
    )(p['qf'], p['kef'], p['decf'], p['qb'], p['keb'], p['decb'], p['vst'], p['oi'], r, s_f0, s_b0, norm_w)


def _col_box_sum(x, n_cols, w):
    half = w // 2
    c_out = lax.broadcasted_iota(jnp.int32, (n_cols, 2 * n_cols), 0)
    c_in = lax.broadcasted_iota(jnp.int32, (n_cols, 2 * n_cols), 1) % n_cols
    band = jnp.where((c_in >= c_out - half) & (c_in < c_out + half), 1.0, 0.0).astype(BF16)
    hi = x.astype(BF16)
    lo = (x - hi.astype(F32)).astype(BF16)
    out = []
    for r in range(x.shape[0] // n_cols):
        rows = slice(r * n_cols, (r + 1) * n_cols)
        out.append(_dot(band, jnp.concatenate([hi[rows], lo[rows]], axis=0)))
    return jnp.concatenate(out, axis=0)


def _row_box_sum(u_ref, lanes, tile, n_tiles, tm, n_cols, w):
    half = w // 2
    halo = half * n_cols
    assert halo <= tm and tm % n_cols == 0
    cur = u_ref[0, pl.ds(pl.multiple_of(tile * tm, tm), tm), lanes]
    before = jnp.maximum(tile - 1, 0) * tm + (tm - halo)
    after = jnp.minimum(tile + 1, n_tiles - 1) * tm
    prev = u_ref[0, pl.ds(pl.multiple_of(before, n_cols), halo), lanes]
    nxt = u_ref[0, pl.ds(pl.multiple_of(after, n_cols), halo), lanes]
    win = jnp.concatenate([jnp.where(tile > 0, prev, 0.0), cur, jnp.where(tile < n_tiles - 1, nxt, 0.0)],
                          axis=0)
    span = 1
    while span < w:
        win = win[:-span * n_cols] + win[span * n_cols:]
        span *= 2
    return win[:tm]


def _merge_kernel(x_ref, mod_ref, og_ref, u_ref, gp_ref, gg_ref, wbg_ref, wbp_ref, wo_ref, pw_ref, ps_ref,
                  o_ref, *, n_rows, n_cols, n_tiles):
    tile = pl.program_id(1)
    tm = x_ref.shape[1]
    y_gla = _dot(og_ref[0], wbg_ref[...])
    tok = lax.broadcasted_iota(jnp.int32, (tm, POOL_GROUP), 0)
    col = (tok % n_cols).astype(F32)
    row = (tile * (tm // n_cols) + tok // n_cols).astype(F32)
    mixed = []
    for g, w in enumerate(POOL_WINDOWS):
        half = float(w // 2)
        lanes = slice(g * POOL_GROUP, (g + 1) * POOL_GROUP)
        u = u_ref[0, pl.ds(pl.multiple_of(tile * tm, tm), tm), lanes]
        cnt = jnp.minimum(col, half) + jnp.minimum(n_cols - col, half)
        m = u
        if n_rows > 1:
            cnt = cnt * (jnp.minimum(row, half) + jnp.minimum(n_rows - row, half))
            m = _row_box_sum(u_ref, lanes, tile, n_tiles, tm, n_cols, w)
        m = _col_box_sum(m, n_cols, w) / cnt
        mixed.append((_dot((m - u).astype(BF16), pw_ref[lanes, :]) * ps_ref[:, lanes]).astype(BF16))
    y_pool = _dot(jnp.concatenate(mixed, axis=1), wbp_ref[...])
    merged = gp_ref[0].astype(F32) * y_pool + gg_ref[0].astype(F32) * y_gla
    gate = mod_ref[0][5:6]
    o_ref[0] = x_ref[0] + gate * _dot(merged.astype(BF16), wo_ref[...])


def _merge(x, mod, og, u, gp, gg, lw, *, n_rows, n_cols):
    b, t, d = x.shape
    tm = _token_tile(t, MERGE_TOKEN_TILE)
    tok = lambda w: pl.BlockSpec((1, tm, w), lambda b, i: (b, i, 0))
    whole = pl.BlockSpec((1, t, POOL_WIDTH), lambda b, i: (b, 0, 0), pipeline_mode=pl.Buffered(1))
    weights = [lw['w_br_gla'], lw['w_br_pool'], lw['w_out'], lw['pool_w']]
    kern = functools.partial(_merge_kernel, n_rows=n_rows, n_cols=n_cols, n_tiles=t // tm)
    return pl.pallas_call(
        kern,
        grid=(b, t // tm),
        in_specs=[tok(d), pl.BlockSpec((1, N_MOD, d), _per_batch_or_shared(mod)),
                  tok(GLA_VW), whole, tok(d), tok(d)] + [_resident(w.shape) for w in weights]
        + [_resident(lw['pool_scale'].shape)],
        out_specs=tok(d),
        out_shape=jax.ShapeDtypeStruct(x.shape, F32),
        compiler_params=_params("parallel", "parallel"),
        name="mixer_merge",
    )(x, mod, og, u, gp, gg, *weights, lw['pool_scale'])


_HALF_D = D_MODEL // 2
_W_IN_PARTS = (('w_k', OFF_K, GLA_KW), ('w_v', OFF_V, GLA_VW), ('w_q', OFF_Q, GLA_KW), ('w_r', OFF_R, GLA_VW),
               ('w_u', OFF_POOL, POOL_WIDTH), ('w_gp0', OFF_GP, _HALF_D), ('w_gp1', OFF_GP + _HALF_D, _HALF_D),
               ('w_gg0', OFF_GG, _HALF_D), ('w_gg1', OFF_GG + _HALF_D, _HALF_D))
_MIXER_MATS = ('w_br_gla', 'w_br_pool', 'w_out', 'pool_w')


def _mixer_cast_jobs(l, w_in_t, w_br_gla, w_br_pool, w_out, pool_w):
    whole = lambda w: (w, l, 0, w.shape[1])
    return ([(w_in_t, l, start, n) for _, start, n in _W_IN_PARTS]
            + [whole(w_br_gla), whole(w_br_pool), whole(w_out), whole(pool_w.reshape(DEPTH, POOL_WIDTH, POOL_GROUP))])


def _mixer(x, mod, nw, lw, s_f0, s_b0, gla_norm_w, *, n_rows, n_cols, states_only=False):
    p = _proj(x, mod, nw, lw)
    og, s_f, s_b = _gla(p, p['r'], s_f0, s_b0, gla_norm_w)
    if states_only:
        return None, s_f, s_b
    return _merge(x, mod, og, p['u'], p['gp'], p['gg'], lw, n_rows=n_rows, n_cols=n_cols), s_f, s_b


def kernel(x, c, ctx, c_ctx, w_mod, b_mod, norm_w, ffn1_up, ffn1_down, w_in, w_af_up, b_af, w_ab_up, b_ab,
           gla_norm_w, pool_w, pool_scale, w_br_pool, w_br_gla, w_out, ffn2_up, ffn2_down, final_norm_w):
    batch, seq, d = x.shape
    ctx_len = ctx.shape[1]
    cond = jnp.concatenate([c, c_ctx[None, :], jnp.zeros((SUBLANES - batch - 1, d), F32)], axis=0)
    mod_all = _modulation(cond, w_mod, b_mod)
    zero_state = jnp.zeros((batch, GLA_DV, GLA_KW), F32)
    fw = final_norm_w[None, :]
    zeros = jnp.zeros((GATE_RANK, GLA_KW), F32)
    up1, down1 = ffn1_up[0].astype(BF16), ffn1_down[0].astype(BF16)
    whole = lambda w, l: (w, l, 0, w.shape[1])
    w_in_t = jnp.swapaxes(w_in, 1, 2)
    for l in range(DEPTH):
        last = l == DEPTH - 1
        mod_x = mod_all[l, :batch].reshape(batch, N_MOD, d)
        mod_c = mod_all[l, batch:batch + 1].reshape(1, N_MOD, d)
        nw = norm_w[l][:, None, :]
        gnw = gla_norm_w[l][None, :]
        lw = {
            'w_gate': jnp.concatenate([jnp.concatenate([w_af_up[l], zeros], axis=1),
                                       jnp.concatenate([zeros, w_ab_up[l]], axis=1)], axis=0).astype(BF16),
            'b_gate': jnp.concatenate([b_af[l], b_ab[l]])[None, :],
            'pool_scale': pool_scale[l][None, :],
            'w_a': w_in_t[l, OFF_AF:OFF_Q],
        }

        x, cast = _ffn(x, mod_x, nw[0], up1, down1, fw, mod_row=0, final_norm=False,
                       cast=[whole(ffn2_up, l), whole(ffn2_down, l)]
                       + _mixer_cast_jobs(l, w_in_t, w_br_gla, w_br_pool, w_out, pool_w))
        up2, down2 = cast[:2]
        lw.update(zip([n for n, _, _ in _W_IN_PARTS] + list(_MIXER_MATS), cast[2:]))
        ctx, _ = _ffn(ctx, mod_c, nw[0], up1, down1, fw, mod_row=0, final_norm=False)

        ctx_mixed, s_f, s_b = _mixer(ctx, mod_c, nw[1], lw, zero_state, zero_state, gnw,
                                     n_rows=1, n_cols=ctx_len, states_only=last)
        x, _, _ = _mixer(x, mod_x, nw[1], lw, s_f, s_b, gnw, n_rows=seq // GRID_W, n_cols=GRID_W)
        x, cast = _ffn(x, mod_x, nw[2], up2, down2, fw, mod_row=6, final_norm=last,
                       cast=[] if last else [whole(ffn1_up, l + 1), whole(ffn1_down, l + 1)])
        if not last:
            ctx, _ = _ffn(ctx_mixed, mod_c, nw[2], up2, down2, fw, mod_row=6, final_norm=False)
            up1, down1 = cast
    return x
```

```python
import functools

import jax
import jax.numpy as jnp
from jax import lax
from jax.experimental import pallas as pl
from jax.experimental.pallas import tpu as pltpu

F32 = jnp.float32
BF16 = jnp.bfloat16

D_MODEL = 1024
DEPTH = 2
GRID_W = 64
N_POOL_GROUPS = 4
POOL_GROUP = 128
POOL_WIDTH = N_POOL_GROUPS * POOL_GROUP
POOL_WINDOWS = (2, 4, 8, 16)
GLA_HEADS = 4
GLA_DK = 64
GLA_DV = 128
GLA_KW = GLA_HEADS * GLA_DK
GLA_VW = GLA_HEADS * GLA_DV
GATE_RANK = 16
GATE_TAU = 16.0
CHUNK = 64
D_FF = 2816
N_MOD = 9
EPS = 1e-6

OFF_K = 0
OFF_V = OFF_K + GLA_KW
OFF_AF = OFF_V + GLA_VW
OFF_AB = OFF_AF + GATE_RANK
OFF_Q = OFF_AB + GATE_RANK
OFF_R = OFF_Q + GLA_KW
OFF_POOL = OFF_R + GLA_VW
OFF_GP = OFF_POOL + POOL_WIDTH
OFF_GG = OFF_GP + D_MODEL
IN_COLS = OFF_GG + D_MODEL

V7X_VMEM_BYTES = 64 * 1024 * 1024
VMEM_LIMIT_BYTES = 56 * 1024 * 1024
SUBLANES = 8
TOKEN_TILE = 1024
FFN_TOKEN_TILE = 1024
SCAN_TOKEN_TILE = 1024
MERGE_TOKEN_TILE = 1024
FF_CHUNK = 256
NORM_PART = 256
PROJ_NORM_PART = 512
PROJ_ROW_PIECE = 128
PROJ_COL_PIECE = 256
MOD_COL_TILE = 1152


def _resident(shape):
    return pl.BlockSpec(shape, lambda *_: (0,) * len(shape), pipeline_mode=pl.Buffered(1))


def _params(*semantics):
    return pltpu.CompilerParams(dimension_semantics=semantics, vmem_limit_bytes=VMEM_LIMIT_BYTES)


def _per_batch_or_shared(arr):
    zeros = (0,) * (arr.ndim - 1)
    if arr.shape[0] == 1:
        return lambda b, i: (0,) + zeros
    return lambda b, i: (b,) + zeros


def _token_tile(t, tile=TOKEN_TILE):
    return min(tile, t)


def _rms(x):
    return x * lax.rsqrt(jnp.mean(x * x, axis=-1, keepdims=True) + EPS)


def _modulated_norm(x, w, shift, scale):
    return (_rms(x) * w) * (1.0 + scale) + shift


def _silu(a):
    return a / (1.0 + jnp.exp(-a))


def _sigmoid(a):
    return 1.0 / (1.0 + jnp.exp(-a))


def _dot(a, b):
    return jnp.dot(a, b, preferred_element_type=F32)


def _dot_nt(a, b):
    return lax.dot_general(a, b, (((1,), (1,)), ((), ())), preferred_element_type=F32)


def _head_blocks(x):
    head = lax.broadcasted_iota(jnp.int32, x.shape, 1) // GLA_DK
    zero = jnp.zeros((), x.dtype)
    return jnp.concatenate([jnp.where(head == h, x, zero) for h in range(GLA_HEADS)], axis=0)


def _chunk_cumsum(g, *, reverse):
    n = g.shape[0]
    pos = lax.broadcasted_iota(jnp.int32, g.shape, 0) % CHUNK
    acc = g
    span = 1
    while span < CHUNK:
        if reverse:
            acc = acc + jnp.where(pos + span < CHUNK, pltpu.roll(acc, n - span, axis=0), 0.0)
        else:
            acc = acc + jnp.where(pos >= span, pltpu.roll(acc, span, axis=0), 0.0)
        span *= 2
    return acc


def _mod_kernel(cond_ref, w_ref, b_ref, o_ref):
    cond = cond_ref[...]
    o_ref[0] = _dot(_silu(cond).astype(BF16), w_ref[0].astype(BF16)) + b_ref[0]


def _modulation(cond, w_mod, b_mod):
    n_cols = N_MOD * D_MODEL
    rows = cond.shape[0]
    return pl.pallas_call(
        _mod_kernel,
        grid=(DEPTH, n_cols // MOD_COL_TILE),
        in_specs=[
            pl.BlockSpec((rows, D_MODEL), lambda l, j: (0, 0)),
            pl.BlockSpec((1, D_MODEL, MOD_COL_TILE), lambda l, j: (l, 0, j)),
            pl.BlockSpec((1, 1, MOD_COL_TILE), lambda l, j: (l, 0, j)),
        ],
        out_specs=pl.BlockSpec((1, rows, MOD_COL_TILE), lambda l, j: (l, 0, j)),
        out_shape=jax.ShapeDtypeStruct((DEPTH, rows, n_cols), F32),
        compiler_params=_params("arbitrary", "arbitrary"),
        name="modulation",
    )(cond, w_mod, b_mod.reshape(DEPTH, 1, n_cols))


def _ffn_kernel(*refs, mod_row, final_norm, n_cast):
    x_ref, mod_ref, nw_ref, up_ref, down_ref, fw_ref = refs[:6]
    cast_in = refs[6:6 + n_cast]
    o_ref = refs[6 + n_cast]
    cast_out = refs[7 + n_cast:7 + 2 * n_cast]
    act_ref = refs[7 + 2 * n_cast]
    x = x_ref[0]
    mod = mod_ref[0]
    shift = mod[mod_row:mod_row + 1]
    scale = mod[mod_row + 1:mod_row + 2]
    gate = mod[mod_row + 2:mod_row + 3]
    parts = [slice(r, r + NORM_PART) for r in range(0, x.shape[0], min(NORM_PART, x.shape[0]))]
    h_parts, a_parts, b_parts = [], [], []
    for p in parts:
        h_parts.append(_modulated_norm(x[p], nw_ref[...], shift, scale).astype(BF16))
        a_parts.append(_dot(h_parts[-1], up_ref[:, :FF_CHUNK]))
        b_parts.append(_dot(h_parts[-1], up_ref[:, D_FF:D_FF + FF_CHUNK]))
    h = jnp.concatenate(h_parts, axis=0)
    n_chunks = D_FF // FF_CHUNK
    for j in range(n_chunks):
        lo = j * FF_CHUNK
        if j == 0:
            a = jnp.concatenate(a_parts, axis=0)
            b = jnp.concatenate(b_parts, axis=0)
        else:
            a = _dot(h, up_ref[:, lo:lo + FF_CHUNK])
            b = _dot(h, up_ref[:, D_FF + lo:D_FF + lo + FF_CHUNK])
        act_ref[:, lo:lo + FF_CHUNK] = (_silu(a) * b).astype(BF16)
        for k in range(j, n_cast, n_chunks):
            cast_out[k][...] = cast_in[k][0].astype(BF16)
    y = x + (0.5 * gate) * _dot(act_ref[...], down_ref[...])
    if final_norm:
        y = _rms(y) * fw_ref[...]
    o_ref[0] = y


def _ffn(x, mod, nw, up, down, fw, *, mod_row, final_norm, cast=()):
    b, t, d = x.shape
    tm = _token_tile(t, FFN_TOKEN_TILE)
    nt = t // tm
    steps = b * nt
    cast_specs, out_specs, out_shapes = [], [], []
    for w, l, start, n_rows in cast:
        rows = n_rows // steps
        assert rows * steps == n_rows and rows % (2 * SUBLANES) == 0 and start % rows == 0, (w.shape, start)
        cast_specs.append(pl.BlockSpec((1, rows, w.shape[2]),
                                       lambda b, i, l=l, first=start // rows: (l, first + b * nt + i, 0)))
        out_specs.append(pl.BlockSpec((rows, w.shape[2]), lambda b, i: (b * nt + i, 0)))
        out_shapes.append(jax.ShapeDtypeStruct((n_rows, w.shape[2]), BF16))
    kern = functools.partial(_ffn_kernel, mod_row=mod_row, final_norm=final_norm, n_cast=len(cast))
    res = pl.pallas_call(
        kern,
        grid=(b, nt),
        in_specs=[
            pl.BlockSpec((1, tm, d), lambda b, i: (b, i, 0)),
            pl.BlockSpec((1, N_MOD, d), _per_batch_or_shared(mod)),
            _resident((1, d)),
            _resident(up.shape),
            _resident(down.shape),
            _resident((1, d)),
        ] + cast_specs,
        out_specs=[pl.BlockSpec((1, tm, d), lambda b, i: (b, i, 0))] + out_specs,
        out_shape=[jax.ShapeDtypeStruct(x.shape, F32)] + out_shapes,
        scratch_shapes=[pltpu.VMEM((tm, D_FF), BF16)],
        compiler_params=_params("parallel", "parallel"),
        name="ffn",
    )(x, mod, nw, up, down, fw, *[job[0] for job in cast])
    return res[0], list(res[1:])


_PROJ_WEIGHTS = ('w_k', 'w_v', 'w_a', 'w_q', 'w_r', 'w_u', 'w_gp0', 'w_gp1', 'w_gg0', 'w_gg1', 'w_gate')


def _proj_kernel(x_ref, mod_ref, nw_ref, wk_ref, wv_ref, wa_ref, wq_ref, wr_ref, wu_ref, wgp0_ref, wgp1_ref,
                 wgg0_ref, wgg1_ref, wgate_ref, bgate_ref,
                 qf_ref, kef_ref, decf_ref, qb_ref, keb_ref, decb_ref, vst_ref, oi_ref,
                 r_ref, u_ref, gp_ref, gg_ref):
    x = x_ref[0]
    mod = mod_ref[0]
    tm = x.shape[0]
    chunks = [slice(c * CHUNK, (c + 1) * CHUNK) for c in range(tm // CHUNK)]
    h_parts, k_parts, q_parts, v_parts = [], [], [], []
    for r in range(0, tm, min(PROJ_NORM_PART, tm)):
        hp = _modulated_norm(x[r:r + PROJ_NORM_PART], nw_ref[...], mod[3:4], mod[4:5]).astype(BF16)
        h_parts.append(hp)
        k_parts.append(_dot_nt(hp, wk_ref[...]))
        q_parts.append(_dot_nt(hp, wq_ref[...]))
        v_parts.append(_dot_nt(hp, wv_ref[...]))
    h = jnp.concatenate(h_parts, axis=0)
    k_all = jnp.concatenate(k_parts, axis=0)
    q_all = jnp.concatenate(q_parts, axis=0) * (GLA_DK ** -0.5)
    v_all = jnp.concatenate(v_parts, axis=0)
    a_low = _dot_nt(h, wa_ref[...].astype(BF16)).astype(BF16)
    z = _dot(a_low, wgate_ref[...]) + bgate_ref[...]
    todo = [(name, ref, lo) for name, ref in (('u', wu_ref), ('gp', wgp0_ref), ('gp', wgp1_ref), ('gg', wgg0_ref),
                                              ('gg', wgg1_ref), ('r', wr_ref))
            for lo in range(0, ref.shape[0], PROJ_COL_PIECE)]
    raw = {'u': [], 'gp': [], 'gg': [], 'r': []}
    q_f, q_b, k_f, k_b = [], [], [], []

    def decay_piece(r0):
        zp = z[r0:r0 + PROJ_ROW_PIECE]
        g = (jnp.minimum(zp, 0.0) - jnp.log(1.0 + jnp.exp(-jnp.abs(zp)))) / GATE_TAU
        b_f = _chunk_cumsum(g[:, :GLA_KW], reverse=False)
        b_b = _chunk_cumsum(g[:, GLA_KW:], reverse=True)
        for c in range(r0 // CHUNK, (r0 + PROJ_ROW_PIECE) // CHUNK):
            rows = chunks[c]
            local = slice(c * CHUNK - r0, (c + 1) * CHUNK - r0)
            q, k, bf, bb = q_all[rows], k_all[rows], b_f[local], b_b[local]
            edge_f = bf[CHUNK - 1:CHUNK]
            edge_b = bb[0:1]
            q_f.append((q * jnp.exp(bf)).astype(BF16))
            q_b.append((q * jnp.exp(bb)).astype(BF16))
            k_f.append((k * jnp.exp(-bf)).astype(BF16))
            k_b.append((k * jnp.exp(-bb)).astype(BF16))
            qf_ref[0, rows, :] = q_f[c]
            qb_ref[0, rows, :] = q_b[c]
            kef_ref[0, rows, :] = (k * jnp.exp(edge_f - bf)).astype(BF16)
            keb_ref[0, rows, :] = (k * jnp.exp(edge_b - bb)).astype(BF16)
            decf_ref[0, c:c + 1, :] = jnp.exp(edge_f)
            decb_ref[0, c:c + 1, :] = jnp.exp(edge_b)

    row_pieces = list(range(0, tm, min(PROJ_ROW_PIECE, tm)))
    for i in range(max(len(todo), len(row_pieces))):
        if i < len(todo):
            name, ref, lo = todo[i]
            raw[name].append(_dot_nt(h, ref[lo:lo + PROJ_COL_PIECE, :]))
        if i < len(row_pieces):
            decay_piece(row_pieces[i])
    u_ref[0] = jnp.concatenate(raw['u'], axis=1)
    gp_raw = jnp.concatenate(raw['gp'], axis=1)
    gg_raw = jnp.concatenate(raw['gg'], axis=1)
    r_raw = jnp.concatenate(raw['r'], axis=1)

    gp_ref[0] = _sigmoid(gp_raw).astype(BF16)
    arow = lax.broadcasted_iota(jnp.int32, (CHUNK, GLA_KW), 0)
    acol = lax.broadcasted_iota(jnp.int32, (CHUNK, GLA_KW), 1) % CHUNK
    causal = acol <= arow
    att = [jnp.where(causal, _dot_nt(q_f[c], _head_blocks(k_f[c])), _dot_nt(q_b[c], _head_blocks(k_b[c])))
           .astype(BF16) for c in range(len(chunks))]
    gg_ref[0] = _sigmoid(gg_raw).astype(BF16)

    v_head = lax.broadcasted_iota(jnp.int32, (CHUNK, GLA_VW), 1) // GLA_DV
    zero = jnp.zeros((), BF16)
    for c, rows in enumerate(chunks):
        v_bf = v_all[rows].astype(BF16)
        v_blk = jnp.concatenate([jnp.where(v_head == hd, v_bf, zero) for hd in range(GLA_HEADS)], axis=0)
        oi_ref[0, rows, :] = _dot(att[c], v_blk)
    r_ref[0] = _silu(r_raw).astype(BF16)

    for c, rows in enumerate(chunks):
        v = v_all[rows]
        v_stack = jnp.concatenate([v[:, hd * GLA_DV:(hd + 1) * GLA_DV] for hd in range(GLA_HEADS)], axis=0)
        vst_ref[0, c] = v_stack.T.astype(BF16)


def _proj(x, mod, nw, lw):
    b, t, d = x.shape
    tm = _token_tile(t)
    nc, tc = t // CHUNK, tm // CHUNK
    tok = lambda w: pl.BlockSpec((1, tm, w), lambda b, i: (b, i, 0))
    dec = pl.BlockSpec((1, tc, GLA_KW), lambda b, i: (b, i, 0))
    outs = [
        ('qf', (b, t, GLA_KW), BF16, tok(GLA_KW)), ('kef', (b, t, GLA_KW), BF16, tok(GLA_KW)),
        ('decf', (b, nc, GLA_KW), F32, dec),
        ('qb', (b, t, GLA_KW), BF16, tok(GLA_KW)), ('keb', (b, t, GLA_KW), BF16, tok(GLA_KW)),
        ('decb', (b, nc, GLA_KW), F32, dec),
        ('vst', (b, nc, GLA_DV, GLA_KW), BF16, pl.BlockSpec((1, tc, GLA_DV, GLA_KW), lambda b, i: (b, i, 0, 0))),
        ('oi', (b, t, GLA_VW), F32, tok(GLA_VW)),
        ('r', (b, t, GLA_VW), BF16, tok(GLA_VW)), ('u', (b, t, POOL_WIDTH), F32, tok(POOL_WIDTH)),
        ('gp', (b, t, d), BF16, tok(d)), ('gg', (b, t, d), BF16, tok(d)),
    ]
    weights = [lw[n] for n in _PROJ_WEIGHTS]
    res = pl.pallas_call(
        _proj_kernel,
        grid=(b, t // tm),
        in_specs=[tok(d), pl.BlockSpec((1, N_MOD, d), _per_batch_or_shared(mod)), _resident((1, d))]
        + [_resident(w.shape) for w in weights] + [_resident(lw['b_gate'].shape)],
        out_specs=[o[3] for o in outs],
        out_shape=[jax.ShapeDtypeStruct(o[1], o[2]) for o in outs],
        compiler_params=_params("parallel", "parallel"),
        name="mixer_proj",
    )(x, mod, nw, *weights, lw['b_gate'])
    return {o[0]: a for o, a in zip(outs, res)}


def _gla_kernel(qf_ref, kef_ref, decf_ref, qb_ref, keb_ref, decb_ref, vst_ref, oi_ref, r_ref,
                sf0_ref, sb0_ref, nw_ref, og_ref, sf_ref, sb_ref, st_ref, ob_ref, *, n_blocks, n_chunks):
    step = pl.program_id(1)
    tb = n_chunks * CHUNK

    def chunk(q_ref, ke_ref, dec_ref, c):
        rows = pl.ds(c * CHUNK, CHUNK)
        st = st_ref[...]
        o_stack = _dot_nt(_head_blocks(q_ref[0, rows, :]), st.astype(BF16))
        st_ref[...] = st * dec_ref[0, c:c + 1, :] + _dot(vst_ref[0, c], _head_blocks(ke_ref[0, rows, :]))
        return jnp.concatenate([o_stack[h * CHUNK:(h + 1) * CHUNK] for h in range(GLA_HEADS)], axis=1)

    @pl.when(step == 0)
    def _():
        st_ref[...] = sb0_ref[0]

    @pl.when(step < n_blocks)
    def _():
        base = (n_blocks - 1 - step) * tb
        for c in reversed(range(n_chunks)):
            o = chunk(qb_ref, keb_ref, decb_ref, c)
            ob_ref[pl.ds(pl.multiple_of(base + c * CHUNK, CHUNK), CHUNK), :] = o

    @pl.when(step == n_blocks - 1)
    def _():
        sb_ref[0] = st_ref[...]

    @pl.when(step == n_blocks)
    def _():
        st_ref[...] = sf0_ref[0]

    @pl.when(step >= n_blocks)
    def _():
        base = (step - n_blocks) * tb
        nw = nw_ref[...]
        for c in range(n_chunks):
            rows = pl.ds(c * CHUNK, CHUNK)
            o = chunk(qf_ref, kef_ref, decf_ref, c)
            o = o + ob_ref[pl.ds(pl.multiple_of(base + c * CHUNK, CHUNK), CHUNK), :] + oi_ref[0, rows, :]
            o = jnp.concatenate(
                [_rms(o[:, h * GLA_DV:(h + 1) * GLA_DV]) * nw for h in range(GLA_HEADS)], axis=1)
            og_ref[0, rows, :] = (o * r_ref[0, rows, :].astype(F32)).astype(og_ref.dtype)

    @pl.when(step == 2 * n_blocks - 1)
    def _():
        sf_ref[0] = st_ref[...]


def _gla(p, r, s_f0, s_b0, norm_w):
    b, t, _ = p['qf'].shape
    tb = _token_tile(t, SCAN_TOKEN_TILE)
    nb = t // tb
    tc = tb // CHUNK
    fwd = lambda b, s: (b, jnp.maximum(s - nb, 0), 0)
    rev = lambda b, s: (b, nb - 1 - jnp.minimum(s, nb - 1), 0)
    both = lambda b, s: (b, jnp.where(s < nb, nb - 1 - s, s - nb), 0, 0)
    tok = lambda w, m: pl.BlockSpec((1, tb, w), m)
    dec = lambda m: pl.BlockSpec((1, tc, GLA_KW), m)
    state = pl.BlockSpec((1, GLA_DV, GLA_KW), lambda b, s: (b, 0, 0))
    kern = functools.partial(_gla_kernel, n_blocks=nb, n_chunks=tc)
    return pl.pallas_call(
        kern,
        grid=(b, 2 * nb),
        in_specs=[tok(GLA_KW, fwd), tok(GLA_KW, fwd), dec(fwd), tok(GLA_KW, rev), tok(GLA_KW, rev), dec(rev),
                  pl.BlockSpec((1, tc, GLA_DV, GLA_KW), both), tok(GLA_VW, fwd), tok(GLA_VW, fwd),
                  state, state, _resident((1, GLA_DV))],
        out_specs=[tok(GLA_VW, fwd), state, state],
        out_shape=[jax.ShapeDtypeStruct((b, t, GLA_VW), BF16),
                   jax.ShapeDtypeStruct((b, GLA_DV, GLA_KW), F32),
                   jax.ShapeDtypeStruct((b, GLA_DV, GLA_KW), F32)],
        scratch_shapes=[pltpu.VMEM((GLA_DV, GLA_KW), F32), pltpu.VMEM((t, GLA_VW), F32)],
        compiler_params=_params("arbitrary", "arbitrary"),
        name="gla_scan",
    )(p['qf'], p['kef'], p['decf'], p['qb'], p['keb'], p['decb'], p['vst'], p['oi'], r, s_f0, s_b0, norm_w)


def _col_box_sum(x, n_cols, w):
    half = w // 2
    c_out = lax.broadcasted_iota(jnp.int32, (n_cols, 2 * n_cols), 0)
    c_in = lax.broadcasted_iota(jnp.int32, (n_cols, 2 * n_cols), 1) % n_cols
    band = jnp.where((c_in >= c_out - half) & (c_in < c_out + half), 1.0, 0.0).astype(BF16)
    hi = x.astype(BF16)
    lo = (x - hi.astype(F32)).astype(BF16)
    out = []
    for r in range(x.shape[0] // n_cols):
        rows = slice(r * n_cols, (r + 1) * n_cols)
        out.append(_dot(band, jnp.concatenate([hi[rows], lo[rows]], axis=0)))
    return jnp.concatenate(out, axis=0)


def _row_box_sum(cur, before_ref, after_ref, lanes, tile, n_tiles, n_cols, w):
    half = w // 2
    halo = half * n_cols
    tm = cur.shape[0]
    assert halo <= before_ref.shape[1] and tm % n_cols == 0
    prev = before_ref[0, before_ref.shape[1] - halo:, lanes]
    nxt = after_ref[0, :halo, lanes]
    win = jnp.concatenate([jnp.where(tile > 0, prev, 0.0), cur, jnp.where(tile < n_tiles - 1, nxt, 0.0)],
                          axis=0)
    span = 1
    while span < w:
        win = win[:-span * n_cols] + win[span * n_cols:]
        span *= 2
    return win[:tm]


def _merge_kernel(x_ref, mod_ref, og_ref, u_ref, ubefore_ref, uafter_ref, gp_ref, gg_ref, wbg_ref, wbp_ref, wo_ref,
                  pw_ref, ps_ref, o_ref, *, n_rows, n_cols, n_tiles):
    tile = pl.program_id(1)
    tm = x_ref.shape[1]
    y_gla = _dot(og_ref[0], wbg_ref[...])
    tok = lax.broadcasted_iota(jnp.int32, (tm, POOL_GROUP), 0)
    col = (tok % n_cols).astype(F32)
    row = (tile * (tm // n_cols) + tok // n_cols).astype(F32)
    mixed = []
    for g, w in enumerate(POOL_WINDOWS):
        half = float(w // 2)
        lanes = slice(g * POOL_GROUP, (g + 1) * POOL_GROUP)
        u = u_ref[0, :, lanes]
        cnt = jnp.minimum(col, half) + jnp.minimum(n_cols - col, half)
        m = u
        if n_rows > 1:
            cnt = cnt * (jnp.minimum(row, half) + jnp.minimum(n_rows - row, half))
            m = _row_box_sum(u, ubefore_ref, uafter_ref, lanes, tile, n_tiles, n_cols, w)
        m = _col_box_sum(m, n_cols, w) / cnt
        mixed.append((_dot((m - u).astype(BF16), pw_ref[lanes, :]) * ps_ref[:, lanes]).astype(BF16))
    y_pool = _dot(jnp.concatenate(mixed, axis=1), wbp_ref[...])
    merged = gp_ref[0].astype(F32) * y_pool + gg_ref[0].astype(F32) * y_gla
    gate = mod_ref[0][5:6]
    o_ref[0] = x_ref[0] + gate * _dot(merged.astype(BF16), wo_ref[...])


def _merge(x, mod, og, u, gp, gg, lw, *, n_rows, n_cols):
    b, t, d = x.shape
    tm = _token_tile(t, MERGE_TOKEN_TILE)
    tok = lambda w: pl.BlockSpec((1, tm, w), lambda b, i: (b, i, 0))
    halo = min(tm, max(POOL_WINDOWS) // 2 * n_cols)
    per_tile, last = tm // halo, t // halo - 1
    before = pl.BlockSpec((1, halo, POOL_WIDTH), lambda b, i: (b, jnp.maximum(i * per_tile - 1, 0), 0))
    after = pl.BlockSpec((1, halo, POOL_WIDTH), lambda b, i: (b, jnp.minimum((i + 1) * per_tile, last), 0))
    weights = [lw['w_br_gla'], lw['w_br_pool'], lw['w_out'], lw['pool_w']]
    kern = functools.partial(_merge_kernel, n_rows=n_rows, n_cols=n_cols, n_tiles=t // tm)
    return pl.pallas_call(
        kern,
        grid=(b, t // tm),
        in_specs=[tok(d), pl.BlockSpec((1, N_MOD, d), _per_batch_or_shared(mod)),
                  tok(GLA_VW), tok(POOL_WIDTH), before, after, tok(d), tok(d)]
        + [_resident(w.shape) for w in weights] + [_resident(lw['pool_scale'].shape)],
        out_specs=tok(d),
        out_shape=jax.ShapeDtypeStruct(x.shape, F32),
        compiler_params=_params("parallel", "parallel"),
        name="mixer_merge",
    )(x, mod, og, u, u, u, gp, gg, *weights, lw['pool_scale'])


_HALF_D = D_MODEL // 2
_W_IN_PARTS = (('w_k', OFF_K, GLA_KW), ('w_v', OFF_V, GLA_VW), ('w_q', OFF_Q, GLA_KW), ('w_r', OFF_R, GLA_VW),
               ('w_u', OFF_POOL, POOL_WIDTH), ('w_gp0', OFF_GP, _HALF_D), ('w_gp1', OFF_GP + _HALF_D, _HALF_D),
               ('w_gg0', OFF_GG, _HALF_D), ('w_gg1', OFF_GG + _HALF_D, _HALF_D))
_MIXER_MATS = ('w_br_gla', 'w_br_pool', 'w_out', 'pool_w')


def _mixer_cast_jobs(l, w_in_t, w_br_gla, w_br_pool, w_out, pool_w):
    whole = lambda w: (w, l, 0, w.shape[1])
    return ([(w_in_t, l, start, n) for _, start, n in _W_IN_PARTS]
            + [whole(w_br_gla), whole(w_br_pool), whole(w_out), whole(pool_w.reshape(DEPTH, POOL_WIDTH, POOL_GROUP))])


def _mixer(x, mod, nw, lw, s_f0, s_b0, gla_norm_w, *, n_rows, n_cols, states_only=False):
    p = _proj(x, mod, nw, lw)
    og, s_f, s_b = _gla(p, p['r'], s_f0, s_b0, gla_norm_w)
    if states_only:
        return None, s_f, s_b
    return _merge(x, mod, og, p['u'], p['gp'], p['gg'], lw, n_rows=n_rows, n_cols=n_cols), s_f, s_b


def kernel(x, c, ctx, c_ctx, w_mod, b_mod, norm_w, ffn1_up, ffn1_down, w_in, w_af_up, b_af, w_ab_up, b_ab,
           gla_norm_w, pool_w, pool_scale, w_br_pool, w_br_gla, w_out, ffn2_up, ffn2_down, final_norm_w):
    batch, seq, d = x.shape
    ctx_len = ctx.shape[1]
    cond = jnp.concatenate([c, c_ctx[None, :], jnp.zeros((SUBLANES - batch - 1, d), F32)], axis=0)
    mod_all = _modulation(cond, w_mod, b_mod)
    zero_state = jnp.zeros((batch, GLA_DV, GLA_KW), F32)
    fw = final_norm_w[None, :]
    zeros = jnp.zeros((GATE_RANK, GLA_KW), F32)
    up1, down1 = ffn1_up[0].astype(BF16), ffn1_down[0].astype(BF16)
    whole = lambda w, l: (w, l, 0, w.shape[1])
    w_in_t = jnp.swapaxes(w_in, 1, 2)
    for l in range(DEPTH):
        last = l == DEPTH - 1
        mod_x = mod_all[l, :batch].reshape(batch, N_MOD, d)
        mod_c = mod_all[l, batch:batch + 1].reshape(1, N_MOD, d)
        nw = norm_w[l][:, None, :]
        gnw = gla_norm_w[l][None, :]
        lw = {
            'w_gate': jnp.concatenate([jnp.concatenate([w_af_up[l], zeros], axis=1),
                                       jnp.concatenate([zeros, w_ab_up[l]], axis=1)], axis=0).astype(BF16),
            'b_gate': jnp.concatenate([b_af[l], b_ab[l]])[None, :],
            'pool_scale': pool_scale[l][None, :],
            'w_a': w_in_t[l, OFF_AF:OFF_Q],
        }

        x, cast = _ffn(x, mod_x, nw[0], up1, down1, fw, mod_row=0, final_norm=False,
                       cast=[whole(ffn2_up, l), whole(ffn2_down, l)]
                       + _mixer_cast_jobs(l, w_in_t, w_br_gla, w_br_pool, w_out, pool_w))
        up2, down2 = cast[:2]
        lw.update(zip([n for n, _, _ in _W_IN_PARTS] + list(_MIXER_MATS), cast[2:]))
        ctx, _ = _ffn(ctx, mod_c, nw[0], up1, down1, fw, mod_row=0, final_norm=False)

        ctx_mixed, s_f, s_b = _mixer(ctx, mod_c, nw[1], lw, zero_state, zero_state, gnw,
                                     n_rows=1, n_cols=ctx_len, states_only=last)
        x, _, _ = _mixer(x, mod_x, nw[1], lw, s_f, s_b, gnw, n_rows=seq // GRID_W, n_cols=GRID_W)
        x, cast = _ffn(x, mod_x, nw[2], up2, down2, fw, mod_row=6, final_norm=last,
                       cast=[] if last else [whole(ffn1_up, l + 1), whole(ffn1_down, l + 1)])
        if not last:
            ctx, _ = _ffn(ctx_mixed, mod_c, nw[2], up2, down2, fw, mod_row=6, final_norm=False)
            up1, down1 = cast
    return x
```

```python
import functools

import jax
import jax.numpy as jnp
from jax import lax
from jax.experimental import pallas as pl
from jax.experimental.pallas import tpu as pltpu

F32 = jnp.float32
BF16 = jnp.bfloat16

D_MODEL = 1024
DEPTH = 2
GRID_W = 64
N_POOL_GROUPS = 4
POOL_GROUP = 128
POOL_WIDTH = N_POOL_GROUPS * POOL_GROUP
POOL_WINDOWS = (2, 4, 8, 16)
GLA_HEADS = 4
GLA_DK = 64
GLA_DV = 128
GLA_KW = GLA_HEADS * GLA_DK
GLA_VW = GLA_HEADS * GLA_DV
GATE_RANK = 16
GATE_TAU = 16.0
CHUNK = 64
D_FF = 2816
N_MOD = 9
EPS = 1e-6

OFF_K = 0
OFF_V = OFF_K + GLA_KW
OFF_AF = OFF_V + GLA_VW
OFF_AB = OFF_AF + GATE_RANK
OFF_Q = OFF_AB + GATE_RANK
OFF_R = OFF_Q + GLA_KW
OFF_POOL = OFF_R + GLA_VW
OFF_GP = OFF_POOL + POOL_WIDTH
OFF_GG = OFF_GP + D_MODEL
IN_COLS = OFF_GG + D_MODEL

V7X_VMEM_BYTES = 64 * 1024 * 1024
VMEM_LIMIT_BYTES = 56 * 1024 * 1024
SUBLANES = 8
TOKEN_TILE = 1024
FFN_TOKEN_TILE = 1024
SCAN_TOKEN_TILE = 1024
MERGE_TOKEN_TILE = 1024
FF_CHUNK = 256
NORM_PART = 256
PROJ_NORM_PART = 512
PROJ_ROW_PIECE = 128
PROJ_COL_PIECE = 256
MOD_COL_TILE = 1152


def _resident(shape):
    return pl.BlockSpec(shape, lambda *_: (0,) * len(shape), pipeline_mode=pl.Buffered(1))


def _params(*semantics):
    return pltpu.CompilerParams(dimension_semantics=semantics, vmem_limit_bytes=VMEM_LIMIT_BYTES)


def _per_batch_or_shared(arr):
    zeros = (0,) * (arr.ndim - 1)
    if arr.shape[0] == 1:
        return lambda b, i: (0,) + zeros
    return lambda b, i: (b,) + zeros


def _token_tile(t, tile=TOKEN_TILE):
    return min(tile, t)


def _rms(x):
    return x * lax.rsqrt(jnp.mean(x * x, axis=-1, keepdims=True) + EPS)


def _modulated_norm(x, w, shift, scale):
    return (_rms(x) * w) * (1.0 + scale) + shift


def _silu(a):
    return a / (1.0 + jnp.exp(-a))


def _sigmoid(a):
    return 1.0 / (1.0 + jnp.exp(-a))


def _dot(a, b):
    return jnp.dot(a, b, preferred_element_type=F32)


def _dot_nt(a, b):
    return lax.dot_general(a, b, (((1,), (1,)), ((), ())), preferred_element_type=F32)


def _head_blocks(x):
    head = lax.broadcasted_iota(jnp.int32, x.shape, 1) // GLA_DK
    zero = jnp.zeros((), x.dtype)
    return jnp.concatenate([jnp.where(head == h, x, zero) for h in range(GLA_HEADS)], axis=0)


def _chunk_cumsum(g, *, reverse):
    n = g.shape[0]
    pos = lax.broadcasted_iota(jnp.int32, g.shape, 0) % CHUNK
    acc = g
    span = 1
    while span < CHUNK:
        if reverse:
            acc = acc + jnp.where(pos + span < CHUNK, pltpu.roll(acc, n - span, axis=0), 0.0)
        else:
            acc = acc + jnp.where(pos >= span, pltpu.roll(acc, span, axis=0), 0.0)
        span *= 2
    return acc


def _mod_kernel(cond_ref, w_ref, b_ref, o_ref):
    cond = cond_ref[...]
    o_ref[0] = _dot(_silu(cond).astype(BF16), w_ref[0].astype(BF16)) + b_ref[0]


def _modulation(cond, w_mod, b_mod):
    n_cols = N_MOD * D_MODEL
    rows = cond.shape[0]
    return pl.pallas_call(
        _mod_kernel,
        grid=(DEPTH, n_cols // MOD_COL_TILE),
        in_specs=[
            pl.BlockSpec((rows, D_MODEL), lambda l, j: (0, 0)),
            pl.BlockSpec((1, D_MODEL, MOD_COL_TILE), lambda l, j: (l, 0, j)),
            pl.BlockSpec((1, 1, MOD_COL_TILE), lambda l, j: (l, 0, j)),
        ],
        out_specs=pl.BlockSpec((1, rows, MOD_COL_TILE), lambda l, j: (l, 0, j)),
        out_shape=jax.ShapeDtypeStruct((DEPTH, rows, n_cols), F32),
        compiler_params=_params("arbitrary", "arbitrary"),
        name="modulation",
    )(cond, w_mod, b_mod.reshape(DEPTH, 1, n_cols))


def _ffn_kernel(*refs, mod_row, final_norm, n_cast):
    x_ref, mod_ref, nw_ref, up_ref, down_ref, fw_ref = refs[:6]
    cast_in = refs[6:6 + n_cast]
    o_ref = refs[6 + n_cast]
    cast_out = refs[7 + n_cast:7 + 2 * n_cast]
    act_ref = refs[7 + 2 * n_cast]
    x = x_ref[0]
    mod = mod_ref[0]
    shift = mod[mod_row:mod_row + 1]
    scale = mod[mod_row + 1:mod_row + 2]
    gate = mod[mod_row + 2:mod_row + 3]
    parts = [slice(r, r + NORM_PART) for r in range(0, x.shape[0], min(NORM_PART, x.shape[0]))]
    h_parts, a_parts, b_parts = [], [], []
    for p in parts:
        h_parts.append(_modulated_norm(x[p], nw_ref[...], shift, scale).astype(BF16))
        a_parts.append(_dot(h_parts[-1], up_ref[:, :FF_CHUNK]))
        b_parts.append(_dot(h_parts[-1], up_ref[:, D_FF:D_FF + FF_CHUNK]))
    h = jnp.concatenate(h_parts, axis=0)
    n_chunks = D_FF // FF_CHUNK
    for j in range(n_chunks):
        lo = j * FF_CHUNK
        if j == 0:
            a = jnp.concatenate(a_parts, axis=0)
            b = jnp.concatenate(b_parts, axis=0)
        else:
            a = _dot(h, up_ref[:, lo:lo + FF_CHUNK])
            b = _dot(h, up_ref[:, D_FF + lo:D_FF + lo + FF_CHUNK])
        act_ref[:, lo:lo + FF_CHUNK] = (_silu(a) * b).astype(BF16)
        for k in range(j, n_cast, n_chunks):
            cast_out[k][...] = cast_in[k][0].astype(BF16)
    y = x + (0.5 * gate) * _dot(act_ref[...], down_ref[...])
    if final_norm:
        y = _rms(y) * fw_ref[...]
    o_ref[0] = y


def _ffn(x, mod, nw, up, down, fw, *, mod_row, final_norm, cast=()):
    b, t, d = x.shape
    tm = _token_tile(t, FFN_TOKEN_TILE)
    nt = t // tm
    steps = b * nt
    cast_specs, out_specs, out_shapes = [], [], []
    for w, l, start, n_rows in cast:
        rows = n_rows // steps
        assert rows * steps == n_rows and rows % (2 * SUBLANES) == 0 and start % rows == 0, (w.shape, start)
        cast_specs.append(pl.BlockSpec((1, rows, w.shape[2]),
                                       lambda b, i, l=l, first=start // rows: (l, first + b * nt + i, 0)))
        out_specs.append(pl.BlockSpec((rows, w.shape[2]), lambda b, i: (b * nt + i, 0)))
        out_shapes.append(jax.ShapeDtypeStruct((n_rows, w.shape[2]), BF16))
    kern = functools.partial(_ffn_kernel, mod_row=mod_row, final_norm=final_norm, n_cast=len(cast))
    res = pl.pallas_call(
        kern,
        grid=(b, nt),
        in_specs=[
            pl.BlockSpec((1, tm, d), lambda b, i: (b, i, 0)),
            pl.BlockSpec((1, N_MOD, d), _per_batch_or_shared(mod)),
            _resident((1, d)),
            _resident(up.shape),
            _resident(down.shape),
            _resident((1, d)),
        ] + cast_specs,
        out_specs=[pl.BlockSpec((1, tm, d), lambda b, i: (b, i, 0))] + out_specs,
        out_shape=[jax.ShapeDtypeStruct(x.shape, F32)] + out_shapes,
        scratch_shapes=[pltpu.VMEM((tm, D_FF), BF16)],
        compiler_params=_params("parallel", "parallel"),
        name="ffn",
    )(x, mod, nw, up, down, fw, *[job[0] for job in cast])
    return res[0], list(res[1:])


_PROJ_WEIGHTS = ('w_k', 'w_v', 'w_a', 'w_q', 'w_r', 'w_u', 'w_gp0', 'w_gp1', 'w_gg0', 'w_gg1', 'w_gate')


def _proj_kernel(x_ref, mod_ref, nw_ref, wk_ref, wv_ref, wa_ref, wq_ref, wr_ref, wu_ref, wgp0_ref, wgp1_ref,
                 wgg0_ref, wgg1_ref, wgate_ref, bgate_ref,
                 qf_ref, kef_ref, decf_ref, qb_ref, keb_ref, decb_ref, vst_ref, oi_ref,
                 r_ref, u_ref, gp_ref, gg_ref):
    x = x_ref[0]
    mod = mod_ref[0]
    tm = x.shape[0]
    chunks = [slice(c * CHUNK, (c + 1) * CHUNK) for c in range(tm // CHUNK)]
    h_parts, z_parts, k_parts, q_parts, v_parts = [], [], [], [], []
    wa = wa_ref[...].astype(BF16)
    for r in range(0, tm, min(PROJ_NORM_PART, tm)):
        hp = _modulated_norm(x[r:r + PROJ_NORM_PART], nw_ref[...], mod[3:4], mod[4:5]).astype(BF16)
        h_parts.append(hp)
        z_parts.append(_dot(_dot_nt(hp, wa).astype(BF16), wgate_ref[...]) + bgate_ref[...])
        k_parts.append(_dot_nt(hp, wk_ref[...]))
        q_parts.append(_dot_nt(hp, wq_ref[...]))
        v_parts.append(_dot_nt(hp, wv_ref[...]))
    h = jnp.concatenate(h_parts, axis=0)
    z = jnp.concatenate(z_parts, axis=0)
    k_all = jnp.concatenate(k_parts, axis=0)
    q_all = jnp.concatenate(q_parts, axis=0) * (GLA_DK ** -0.5)
    v_all = jnp.concatenate(v_parts, axis=0)
    todo = [(name, ref, lo) for name, ref in (('u', wu_ref), ('gp', wgp0_ref), ('gp', wgp1_ref), ('gg', wgg0_ref),
                                              ('gg', wgg1_ref), ('r', wr_ref))
            for lo in range(0, ref.shape[0], PROJ_COL_PIECE)]
    raw = {'u': [], 'gp': [], 'gg': [], 'r': []}
    q_f, q_b, k_f, k_b = [], [], [], []

    def decay_piece(r0):
        zp = z[r0:r0 + PROJ_ROW_PIECE]
        g = (jnp.minimum(zp, 0.0) - jnp.log(1.0 + jnp.exp(-jnp.abs(zp)))) / GATE_TAU
        b_f = _chunk_cumsum(g[:, :GLA_KW], reverse=False)
        b_b = _chunk_cumsum(g[:, GLA_KW:], reverse=True)
        for c in range(r0 // CHUNK, (r0 + PROJ_ROW_PIECE) // CHUNK):
            rows = chunks[c]
            local = slice(c * CHUNK - r0, (c + 1) * CHUNK - r0)
            q, k, bf, bb = q_all[rows], k_all[rows], b_f[local], b_b[local]
            edge_f = bf[CHUNK - 1:CHUNK]
            edge_b = bb[0:1]
            q_f.append((q * jnp.exp(bf)).astype(BF16))
            q_b.append((q * jnp.exp(bb)).astype(BF16))
            k_f.append((k * jnp.exp(-bf)).astype(BF16))
            k_b.append((k * jnp.exp(-bb)).astype(BF16))
            qf_ref[0, rows, :] = q_f[c]
            qb_ref[0, rows, :] = q_b[c]
            kef_ref[0, rows, :] = (k * jnp.exp(edge_f - bf)).astype(BF16)
            keb_ref[0, rows, :] = (k * jnp.exp(edge_b - bb)).astype(BF16)
            decf_ref[0, c:c + 1, :] = jnp.exp(edge_f)
            decb_ref[0, c:c + 1, :] = jnp.exp(edge_b)

    row_pieces = list(range(0, tm, min(PROJ_ROW_PIECE, tm)))
    for i in range(max(len(todo), len(row_pieces))):
        if i < len(todo):
            name, ref, lo = todo[i]
            raw[name].append(_dot_nt(h, ref[lo:lo + PROJ_COL_PIECE, :]))
        if i < len(row_pieces):
            decay_piece(row_pieces[i])
    u_ref[0] = jnp.concatenate(raw['u'], axis=1)
    gp_raw = jnp.concatenate(raw['gp'], axis=1)
    gg_raw = jnp.concatenate(raw['gg'], axis=1)
    r_raw = jnp.concatenate(raw['r'], axis=1)

    gp_ref[0] = _sigmoid(gp_raw).astype(BF16)
    arow = lax.broadcasted_iota(jnp.int32, (CHUNK, GLA_KW), 0)
    acol = lax.broadcasted_iota(jnp.int32, (CHUNK, GLA_KW), 1) % CHUNK
    causal = acol <= arow
    att = [jnp.where(causal, _dot_nt(q_f[c], _head_blocks(k_f[c])), _dot_nt(q_b[c], _head_blocks(k_b[c])))
           .astype(BF16) for c in range(len(chunks))]
    gg_ref[0] = _sigmoid(gg_raw).astype(BF16)

    v_head = lax.broadcasted_iota(jnp.int32, (CHUNK, GLA_VW), 1) // GLA_DV
    zero = jnp.zeros((), BF16)
    for c, rows in enumerate(chunks):
        v_bf = v_all[rows].astype(BF16)
        v_blk = jnp.concatenate([jnp.where(v_head == hd, v_bf, zero) for hd in range(GLA_HEADS)], axis=0)
        oi_ref[0, rows, :] = _dot(att[c], v_blk)
    r_ref[0] = _silu(r_raw).astype(BF16)

    for c, rows in enumerate(chunks):
        v = v_all[rows]
        v_stack = jnp.concatenate([v[:, hd * GLA_DV:(hd + 1) * GLA_DV] for hd in range(GLA_HEADS)], axis=0)
        vst_ref[0, c] = v_stack.T.astype(BF16)


def _proj(x, mod, nw, lw):
    b, t, d = x.shape
    tm = _token_tile(t)
    nc, tc = t // CHUNK, tm // CHUNK
    tok = lambda w: pl.BlockSpec((1, tm, w), lambda b, i: (b, i, 0))
    dec = pl.BlockSpec((1, tc, GLA_KW), lambda b, i: (b, i, 0))
    outs = [
        ('qf', (b, t, GLA_KW), BF16, tok(GLA_KW)), ('kef', (b, t, GLA_KW), BF16, tok(GLA_KW)),
        ('decf', (b, nc, GLA_KW), F32, dec),
        ('qb', (b, t, GLA_KW), BF16, tok(GLA_KW)), ('keb', (b, t, GLA_KW), BF16, tok(GLA_KW)),
        ('decb', (b, nc, GLA_KW), F32, dec),
        ('vst', (b, nc, GLA_DV, GLA_KW), BF16, pl.BlockSpec((1, tc, GLA_DV, GLA_KW), lambda b, i: (b, i, 0, 0))),
        ('oi', (b, t, GLA_VW), F32, tok(GLA_VW)),
        ('r', (b, t, GLA_VW), BF16, tok(GLA_VW)), ('u', (b, t, POOL_WIDTH), F32, tok(POOL_WIDTH)),
        ('gp', (b, t, d), BF16, tok(d)), ('gg', (b, t, d), BF16, tok(d)),
    ]
    weights = [lw[n] for n in _PROJ_WEIGHTS]
    res = pl.pallas_call(
        _proj_kernel,
        grid=(b, t // tm),
        in_specs=[tok(d), pl.BlockSpec((1, N_MOD, d), _per_batch_or_shared(mod)), _resident((1, d))]
        + [_resident(w.shape) for w in weights] + [_resident(lw['b_gate'].shape)],
        out_specs=[o[3] for o in outs],
        out_shape=[jax.ShapeDtypeStruct(o[1], o[2]) for o in outs],
        compiler_params=_params("parallel", "parallel"),
        name="mixer_proj",
    )(x, mod, nw, *weights, lw['b_gate'])
    return {o[0]: a for o, a in zip(outs, res)}


def _gla_kernel(qf_ref, kef_ref, decf_ref, qb_ref, keb_ref, decb_ref, vst_ref, oi_ref, r_ref,
                sf0_ref, sb0_ref, nw_ref, og_ref, sf_ref, sb_ref, st_ref, ob_ref, *, n_blocks, n_chunks):
    step = pl.program_id(1)
    tb = n_chunks * CHUNK

    def chunk(q_ref, ke_ref, dec_ref, c):
        rows = pl.ds(c * CHUNK, CHUNK)
        st = st_ref[...]
        o_stack = _dot_nt(_head_blocks(q_ref[0, rows, :]), st.astype(BF16))
        st_ref[...] = st * dec_ref[0, c:c + 1, :] + _dot(vst_ref[0, c], _head_blocks(ke_ref[0, rows, :]))
        return jnp.concatenate([o_stack[h * CHUNK:(h + 1) * CHUNK] for h in range(GLA_HEADS)], axis=1)

    @pl.when(step == 0)
    def _():
        st_ref[...] = sb0_ref[0]

    @pl.when(step < n_blocks)
    def _():
        base = (n_blocks - 1 - step) * tb
        for c in reversed(range(n_chunks)):
            o = chunk(qb_ref, keb_ref, decb_ref, c)
            ob_ref[pl.ds(pl.multiple_of(base + c * CHUNK, CHUNK), CHUNK), :] = o

    @pl.when(step == n_blocks - 1)
    def _():
        sb_ref[0] = st_ref[...]

    @pl.when(step == n_blocks)
    def _():
        st_ref[...] = sf0_ref[0]

    @pl.when(step >= n_blocks)
    def _():
        base = (step - n_blocks) * tb
        nw = nw_ref[...]
        for c in range(n_chunks):
            rows = pl.ds(c * CHUNK, CHUNK)
            o = chunk(qf_ref, kef_ref, decf_ref, c)
            o = o + ob_ref[pl.ds(pl.multiple_of(base + c * CHUNK, CHUNK), CHUNK), :] + oi_ref[0, rows, :]
            o = jnp.concatenate(
                [_rms(o[:, h * GLA_DV:(h + 1) * GLA_DV]) * nw for h in range(GLA_HEADS)], axis=1)
            og_ref[0, rows, :] = (o * r_ref[0, rows, :].astype(F32)).astype(og_ref.dtype)

    @pl.when(step == 2 * n_blocks - 1)
    def _():
        sf_ref[0] = st_ref[...]


def _gla(p, r, s_f0, s_b0, norm_w):
    b, t, _ = p['qf'].shape
    tb = _token_tile(t, SCAN_TOKEN_TILE)
    nb = t // tb
    tc = tb // CHUNK
    fwd = lambda b, s: (b, jnp.maximum(s - nb, 0), 0)
    rev = lambda b, s: (b, nb - 1 - jnp.minimum(s, nb - 1), 0)
    both = lambda b, s: (b, jnp.where(s < nb, nb - 1 - s, s - nb), 0, 0)
    tok = lambda w, m: pl.BlockSpec((1, tb, w), m)
    dec = lambda m: pl.BlockSpec((1, tc, GLA_KW), m)
    state = pl.BlockSpec((1, GLA_DV, GLA_KW), lambda b, s: (b, 0, 0))
    kern = functools.partial(_gla_kernel, n_blocks=nb, n_chunks=tc)
    return pl.pallas_call(
        kern,
        grid=(b, 2 * nb),
        in_specs=[tok(GLA_KW, fwd), tok(GLA_KW, fwd), dec(fwd), tok(GLA_KW, rev), tok(GLA_KW, rev), dec(rev),
                  pl.BlockSpec((1, tc, GLA_DV, GLA_KW), both), tok(GLA_VW, fwd), tok(GLA_VW, fwd),
                  state, state, _resident((1, GLA_DV))],
        out_specs=[tok(GLA_VW, fwd), state, state],
        out_shape=[jax.ShapeDtypeStruct((b, t, GLA_VW), BF16),
                   jax.ShapeDtypeStruct((b, GLA_DV, GLA_KW), F32),
                   jax.ShapeDtypeStruct((b, GLA_DV, GLA_KW), F32)],
        scratch_shapes=[pltpu.VMEM((GLA_DV, GLA_KW), F32), pltpu.VMEM((t, GLA_VW), F32)],
        compiler_params=_params("arbitrary", "arbitrary"),
        name="gla_scan",
    )(p['qf'], p['kef'], p['decf'], p['qb'], p['keb'], p['decb'], p['vst'], p['oi'], r, s_f0, s_b0, norm_w)


def _col_box_sum(x, n_cols, w):
    half = w // 2
    c_out = lax.broadcasted_iota(jnp.int32, (n_cols, 2 * n_cols), 0)
    c_in = lax.broadcasted_iota(jnp.int32, (n_cols, 2 * n_cols), 1) % n_cols
    band = jnp.where((c_in >= c_out - half) & (c_in < c_out + half), 1.0, 0.0).astype(BF16)
    hi = x.astype(BF16)
    lo = (x - hi.astype(F32)).astype(BF16)
    out = []
    for r in range(x.shape[0] // n_cols):
        rows = slice(r * n_cols, (r + 1) * n_cols)
        out.append(_dot(band, jnp.concatenate([hi[rows], lo[rows]], axis=0)))
    return jnp.concatenate(out, axis=0)


def _row_box_sum(cur, before_ref, after_ref, lanes, tile, n_tiles, n_cols, w):
    half = w // 2
    halo = half * n_cols
    tm = cur.shape[0]
    assert halo <= before_ref.shape[1] and tm % n_cols == 0
    prev = before_ref[0, before_ref.shape[1] - halo:, lanes]
    nxt = after_ref[0, :halo, lanes]
    win = jnp.concatenate([jnp.where(tile > 0, prev, 0.0), cur, jnp.where(tile < n_tiles - 1, nxt, 0.0)],
                          axis=0)
    span = 1
    while span < w:
        win = win[:-span * n_cols] + win[span * n_cols:]
        span *= 2
    return win[:tm]


def _merge_kernel(x_ref, mod_ref, og_ref, u_ref, ubefore_ref, uafter_ref, gp_ref, gg_ref, wbg_ref, wbp_ref, wo_ref,
                  pw_ref, ps_ref, o_ref, *, n_rows, n_cols, n_tiles):
    tile = pl.program_id(1)
    tm = x_ref.shape[1]
    y_gla = _dot(og_ref[0], wbg_ref[...])
    tok = lax.broadcasted_iota(jnp.int32, (tm, POOL_GROUP), 0)
    col = (tok % n_cols).astype(F32)
    row = (tile * (tm // n_cols) + tok // n_cols).astype(F32)
    mixed = []
    for g, w in enumerate(POOL_WINDOWS):
        half = float(w // 2)
        lanes = slice(g * POOL_GROUP, (g + 1) * POOL_GROUP)
        u = u_ref[0, :, lanes]
        cnt = jnp.minimum(col, half) + jnp.minimum(n_cols - col, half)
        m = u
        if n_rows > 1:
            cnt = cnt * (jnp.minimum(row, half) + jnp.minimum(n_rows - row, half))
            m = _row_box_sum(u, ubefore_ref, uafter_ref, lanes, tile, n_tiles, n_cols, w)
        m = _col_box_sum(m, n_cols, w) / cnt
        mixed.append((_dot((m - u).astype(BF16), pw_ref[lanes, :]) * ps_ref[:, lanes]).astype(BF16))
    y_pool = _dot(jnp.concatenate(mixed, axis=1), wbp_ref[...])
    merged = gp_ref[0].astype(F32) * y_pool + gg_ref[0].astype(F32) * y_gla
    gate = mod_ref[0][5:6]
    o_ref[0] = x_ref[0] + gate * _dot(merged.astype(BF16), wo_ref[...])


def _merge(x, mod, og, u, gp, gg, lw, *, n_rows, n_cols):
    b, t, d = x.shape
    tm = _token_tile(t, MERGE_TOKEN_TILE)
    tok = lambda w: pl.BlockSpec((1, tm, w), lambda b, i: (b, i, 0))
    halo = min(tm, max(POOL_WINDOWS) // 2 * n_cols)
    per_tile, last = tm // halo, t // halo - 1
    before = pl.BlockSpec((1, halo, POOL_WIDTH), lambda b, i: (b, jnp.maximum(i * per_tile - 1, 0), 0))
    after = pl.BlockSpec((1, halo, POOL_WIDTH), lambda b, i: (b, jnp.minimum((i + 1) * per_tile, last), 0))
    weights = [lw['w_br_gla'], lw['w_br_pool'], lw['w_out'], lw['pool_w']]
    kern = functools.partial(_merge_kernel, n_rows=n_rows, n_cols=n_cols, n_tiles=t // tm)
    return pl.pallas_call(
        kern,
        grid=(b, t // tm),
        in_specs=[tok(d), pl.BlockSpec((1, N_MOD, d), _per_batch_or_shared(mod)),
                  tok(GLA_VW), tok(POOL_WIDTH), before, after, tok(d), tok(d)]
        + [_resident(w.shape) for w in weights] + [_resident(lw['pool_scale'].shape)],
        out_specs=tok(d),
        out_shape=jax.ShapeDtypeStruct(x.shape, F32),
        compiler_params=_params("parallel", "parallel"),
        name="mixer_merge",
    )(x, mod, og, u, u, u, gp, gg, *weights, lw['pool_scale'])


_HALF_D = D_MODEL // 2
_W_IN_PARTS = (('w_k', OFF_K, GLA_KW), ('w_v', OFF_V, GLA_VW), ('w_q', OFF_Q, GLA_KW), ('w_r', OFF_R, GLA_VW),
               ('w_u', OFF_POOL, POOL_WIDTH), ('w_gp0', OFF_GP, _HALF_D), ('w_gp1', OFF_GP + _HALF_D, _HALF_D),
               ('w_gg0', OFF_GG, _HALF_D), ('w_gg1', OFF_GG + _HALF_D, _HALF_D))
_MIXER_MATS = ('w_br_gla', 'w_br_pool', 'w_out', 'pool_w')


def _mixer_cast_jobs(l, w_in_t, w_br_gla, w_br_pool, w_out, pool_w):
    whole = lambda w: (w, l, 0, w.shape[1])
    return ([(w_in_t, l, start, n) for _, start, n in _W_IN_PARTS]
            + [whole(w_br_gla), whole(w_br_pool), whole(w_out), whole(pool_w.reshape(DEPTH, POOL_WIDTH, POOL_GROUP))])


def _mixer(x, mod, nw, lw, s_f0, s_b0, gla_norm_w, *, n_rows, n_cols, states_only=False):
    p = _proj(x, mod, nw, lw)
    og, s_f, s_b = _gla(p, p['r'], s_f0, s_b0, gla_norm_w)
    if states_only:
        return None, s_f, s_b
    return _merge(x, mod, og, p['u'], p['gp'], p['gg'], lw, n_rows=n_rows, n_cols=n_cols), s_f, s_b


def kernel(x, c, ctx, c_ctx, w_mod, b_mod, norm_w, ffn1_up, ffn1_down, w_in, w_af_up, b_af, w_ab_up, b_ab,
           gla_norm_w, pool_w, pool_scale, w_br_pool, w_br_gla, w_out, ffn2_up, ffn2_down, final_norm_w):
    batch, seq, d = x.shape
    ctx_len = ctx.shape[1]
    cond = jnp.concatenate([c, c_ctx[None, :], jnp.zeros((SUBLANES - batch - 1, d), F32)], axis=0)
    mod_all = _modulation(cond, w_mod, b_mod)
    zero_state = jnp.zeros((batch, GLA_DV, GLA_KW), F32)
    fw = final_norm_w[None, :]
    zeros = jnp.zeros((GATE_RANK, GLA_KW), F32)
    up1, down1 = ffn1_up[0].astype(BF16), ffn1_down[0].astype(BF16)
    whole = lambda w, l: (w, l, 0, w.shape[1])
    w_in_t = jnp.swapaxes(w_in, 1, 2)
    for l in range(DEPTH):
        last = l == DEPTH - 1
        mod_x = mod_all[l, :batch].reshape(batch, N_MOD, d)
        mod_c = mod_all[l, batch:batch + 1].reshape(1, N_MOD, d)
        nw = norm_w[l][:, None, :]
        gnw = gla_norm_w[l][None, :]
        lw = {
            'w_gate': jnp.concatenate([jnp.concatenate([w_af_up[l], zeros], axis=1),
                                       jnp.concatenate([zeros, w_ab_up[l]], axis=1)], axis=0).astype(BF16),
            'b_gate': jnp.concatenate([b_af[l], b_ab[l]])[None, :],
            'pool_scale': pool_scale[l][None, :],
            'w_a': w_in_t[l, OFF_AF:OFF_Q],
        }

        x, cast = _ffn(x, mod_x, nw[0], up1, down1, fw, mod_row=0, final_norm=False,
                       cast=[whole(ffn2_up, l), whole(ffn2_down, l)]
                       + _mixer_cast_jobs(l, w_in_t, w_br_gla, w_br_pool, w_out, pool_w))
        up2, down2 = cast[:2]
        lw.update(zip([n for n, _, _ in _W_IN_PARTS] + list(_MIXER_MATS), cast[2:]))
        ctx, _ = _ffn(ctx, mod_c, nw[0], up1, down1, fw, mod_row=0, final_norm=False)

        ctx_mixed, s_f, s_b = _mixer(ctx, mod_c, nw[1], lw, zero_state, zero_state, gnw,
                                     n_rows=1, n_cols=ctx_len, states_only=last)
        x, _, _ = _mixer(x, mod_x, nw[1], lw, s_f, s_b, gnw, n_rows=seq // GRID_W, n_cols=GRID_W)
        x, cast = _ffn(x, mod_x, nw[2], up2, down2, fw, mod_row=6, final_norm=last,
                       cast=[] if last else [whole(ffn1_up, l + 1), whole(ffn1_down, l + 1)])
        if not last:
            ctx, _ = _ffn(ctx_mixed, mod_c, nw[2], up2, down2, fw, mod_row=6, final_norm=False)
            up1, down1 = cast
    return x
```

```python
import functools

import jax
import jax.numpy as jnp
from jax import lax
from jax.experimental import pallas as pl
from jax.experimental.pallas import tpu as pltpu

F32 = jnp.float32
BF16 = jnp.bfloat16

D_MODEL = 1024
DEPTH = 2
GRID_W = 64
N_POOL_GROUPS = 4
POOL_GROUP = 128
POOL_WIDTH = N_POOL_GROUPS * POOL_GROUP
POOL_WINDOWS = (2, 4, 8, 16)
GLA_HEADS = 4
GLA_DK = 64
GLA_DV = 128
GLA_KW = GLA_HEADS * GLA_DK
GLA_VW = GLA_HEADS * GLA_DV
GATE_RANK = 16
GATE_TAU = 16.0
CHUNK = 64
D_FF = 2816
N_MOD = 9
EPS = 1e-6

OFF_K = 0
OFF_V = OFF_K + GLA_KW
OFF_AF = OFF_V + GLA_VW
OFF_AB = OFF_AF + GATE_RANK
OFF_Q = OFF_AB + GATE_RANK
OFF_R = OFF_Q + GLA_KW
OFF_POOL = OFF_R + GLA_VW
OFF_GP = OFF_POOL + POOL_WIDTH
OFF_GG = OFF_GP + D_MODEL
IN_COLS = OFF_GG + D_MODEL

V7X_VMEM_BYTES = 64 * 1024 * 1024
VMEM_LIMIT_BYTES = V7X_VMEM_BYTES * 7 // 8
SUBLANES = 8
TOKEN_TILE = 1024
FFN_TOKEN_TILE = 1024
SCAN_TOKEN_TILE = 1024
MERGE_TOKEN_TILE = 1024
FF_CHUNK = 256
NORM_PART = 256
PROJ_NORM_PART = 512
PROJ_ROW_PIECE = 128
PROJ_COL_PIECE = 256
MOD_COL_TILE = 2304


def _resident(shape):
    return pl.BlockSpec(shape, lambda *_: (0,) * len(shape), pipeline_mode=pl.Buffered(1))


def _params(*semantics):
    return pltpu.CompilerParams(dimension_semantics=semantics, vmem_limit_bytes=VMEM_LIMIT_BYTES)


def _per_batch_or_shared(arr):
    zeros = (0,) * (arr.ndim - 1)
    if arr.shape[0] == 1:
        return lambda b, i: (0,) + zeros
    return lambda b, i: (b,) + zeros


def _token_tile(t, tile=TOKEN_TILE):
    return min(tile, t)


def _rms(x):
    return x * lax.rsqrt(jnp.mean(x * x, axis=-1, keepdims=True) + EPS)


def _modulated_norm(x, w, shift, scale):
    return (_rms(x) * w) * (1.0 + scale) + shift


def _silu(a):
    return a / (1.0 + jnp.exp(-a))


def _sigmoid(a):
    return 1.0 / (1.0 + jnp.exp(-a))


def _dot(a, b):
    return jnp.dot(a, b, preferred_element_type=F32)


def _dot_nt(a, b):
    return lax.dot_general(a, b, (((1,), (1,)), ((), ())), preferred_element_type=F32)


def _head_blocks(x):
    head = lax.broadcasted_iota(jnp.int32, x.shape, 1) // GLA_DK
    zero = jnp.zeros((), x.dtype)
    return jnp.concatenate([jnp.where(head == h, x, zero) for h in range(GLA_HEADS)], axis=0)


def _chunk_cumsum(g, *, reverse):
    n = g.shape[0]
    pos = lax.broadcasted_iota(jnp.int32, g.shape, 0) % CHUNK
    acc = g
    span = 1
    while span < CHUNK:
        if reverse:
            acc = acc + jnp.where(pos + span < CHUNK, pltpu.roll(acc, n - span, axis=0), 0.0)
        else:
            acc = acc + jnp.where(pos >= span, pltpu.roll(acc, span, axis=0), 0.0)
        span *= 2
    return acc


def _mod_kernel(cond_ref, w_ref, b_ref, o_ref):
    cond = cond_ref[...]
    o_ref[0] = _dot(_silu(cond).astype(BF16), w_ref[0].astype(BF16)) + b_ref[0]


def _modulation(cond, w_mod, b_mod):
    n_cols = N_MOD * D_MODEL
    rows = cond.shape[0]
    return pl.pallas_call(
        _mod_kernel,
        grid=(DEPTH, n_cols // MOD_COL_TILE),
        in_specs=[
            pl.BlockSpec((rows, D_MODEL), lambda l, j: (0, 0)),
            pl.BlockSpec((1, D_MODEL, MOD_COL_TILE), lambda l, j: (l, 0, j)),
            pl.BlockSpec((1, 1, MOD_COL_TILE), lambda l, j: (l, 0, j)),
        ],
        out_specs=pl.BlockSpec((1, rows, MOD_COL_TILE), lambda l, j: (l, 0, j)),
        out_shape=jax.ShapeDtypeStruct((DEPTH, rows, n_cols), F32),
        compiler_params=_params("arbitrary", "arbitrary"),
        name="modulation",
    )(cond, w_mod, b_mod.reshape(DEPTH, 1, n_cols))


def _ffn_kernel(*refs, mod_row, final_norm, n_cast):
    x_ref, mod_ref, nw_ref, up_ref, down_ref, fw_ref = refs[:6]
    cast_in = refs[6:6 + n_cast]
    o_ref = refs[6 + n_cast]
    cast_out = refs[7 + n_cast:7 + 2 * n_cast]
    act_ref = refs[7 + 2 * n_cast]
    x = x_ref[0]
    mod = mod_ref[0]
    shift = mod[mod_row:mod_row + 1]
    scale = mod[mod_row + 1:mod_row + 2]
    gate = mod[mod_row + 2:mod_row + 3]
    parts = [slice(r, r + NORM_PART) for r in range(0, x.shape[0], min(NORM_PART, x.shape[0]))]
    h_parts, a_parts, b_parts = [], [], []
    for p in parts:
        h_parts.append(_modulated_norm(x[p], nw_ref[...], shift, scale).astype(BF16))
        a_parts.append(_dot(h_parts[-1], up_ref[:, :FF_CHUNK]))
        b_parts.append(_dot(h_parts[-1], up_ref[:, D_FF:D_FF + FF_CHUNK]))
    h = jnp.concatenate(h_parts, axis=0)
    n_chunks = D_FF // FF_CHUNK
    for j in range(n_chunks):
        lo = j * FF_CHUNK
        if j == 0:
            a = jnp.concatenate(a_parts, axis=0)
            b = jnp.concatenate(b_parts, axis=0)
        else:
            a = _dot(h, up_ref[:, lo:lo + FF_CHUNK])
            b = _dot(h, up_ref[:, D_FF + lo:D_FF + lo + FF_CHUNK])
        act_ref[:, lo:lo + FF_CHUNK] = (_silu(a) * b).astype(BF16)
        for k in range(j, n_cast, n_chunks):
            cast_out[k][...] = cast_in[k][0].astype(BF16)
    y = x + (0.5 * gate) * _dot(act_ref[...], down_ref[...])
    if final_norm:
        y = _rms(y) * fw_ref[...]
    o_ref[0] = y


def _ffn(x, mod, nw, up, down, fw, *, mod_row, final_norm, cast=()):
    b, t, d = x.shape
    tm = _token_tile(t, FFN_TOKEN_TILE)
    nt = t // tm
    steps = b * nt
    cast_specs, out_specs, out_shapes = [], [], []
    for w, l, start, n_rows in cast:
        rows = n_rows // steps
        assert rows * steps == n_rows and rows % (2 * SUBLANES) == 0 and start % rows == 0, (w.shape, start)
        cast_specs.append(pl.BlockSpec((1, rows, w.shape[2]),
                                       lambda b, i, l=l, first=start // rows: (l, first + b * nt + i, 0)))
        out_specs.append(pl.BlockSpec((rows, w.shape[2]), lambda b, i: (b * nt + i, 0)))
        out_shapes.append(jax.ShapeDtypeStruct((n_rows, w.shape[2]), BF16))
    kern = functools.partial(_ffn_kernel, mod_row=mod_row, final_norm=final_norm, n_cast=len(cast))
    res = pl.pallas_call(
        kern,
        grid=(b, nt),
        in_specs=[
            pl.BlockSpec((1, tm, d), lambda b, i: (b, i, 0)),
            pl.BlockSpec((1, N_MOD, d), _per_batch_or_shared(mod)),
            _resident((1, d)),
            _resident(up.shape),
            _resident(down.shape),
            _resident((1, d)),
        ] + cast_specs,
        out_specs=[pl.BlockSpec((1, tm, d), lambda b, i: (b, i, 0))] + out_specs,
        out_shape=[jax.ShapeDtypeStruct(x.shape, F32)] + out_shapes,
        scratch_shapes=[pltpu.VMEM((tm, D_FF), BF16)],
        compiler_params=_params("parallel", "parallel"),
        name="ffn",
    )(x, mod, nw, up, down, fw, *[job[0] for job in cast])
    return res[0], list(res[1:])


_PROJ_WEIGHTS = ('w_k', 'w_v', 'w_a', 'w_q', 'w_r', 'w_u', 'w_gp0', 'w_gp1', 'w_gg0', 'w_gg1', 'w_gate')


def _proj_kernel(x_ref, mod_ref, nw_ref, wk_ref, wv_ref, wa_ref, wq_ref, wr_ref, wu_ref, wgp0_ref, wgp1_ref,
                 wgg0_ref, wgg1_ref, wgate_ref, bgate_ref,
                 qf_ref, kef_ref, decf_ref, qb_ref, keb_ref, decb_ref, vst_ref, oi_ref,
                 r_ref, u_ref, gp_ref, gg_ref):
    x = x_ref[0]
    mod = mod_ref[0]
    tm = x.shape[0]
    chunks = [slice(c * CHUNK, (c + 1) * CHUNK) for c in range(tm // CHUNK)]
    h_parts, z_parts, k_parts, q_parts, v_parts = [], [], [], [], []
    wa = wa_ref[...].astype(BF16)
    for r in range(0, tm, min(PROJ_NORM_PART, tm)):
        hp = _modulated_norm(x[r:r + PROJ_NORM_PART], nw_ref[...], mod[3:4], mod[4:5]).astype(BF16)
        h_parts.append(hp)
        z_parts.append(_dot(_dot_nt(hp, wa).astype(BF16), wgate_ref[...]) + bgate_ref[...])
        k_parts.append(_dot_nt(hp, wk_ref[...]))
        q_parts.append(_dot_nt(hp, wq_ref[...]))
        v_parts.append(_dot_nt(hp, wv_ref[...]))
    h = jnp.concatenate(h_parts, axis=0)
    z = jnp.concatenate(z_parts, axis=0)
    k_all = jnp.concatenate(k_parts, axis=0)
    q_all = jnp.concatenate(q_parts, axis=0) * (GLA_DK ** -0.5)
    v_all = jnp.concatenate(v_parts, axis=0)
    todo = [(name, ref, lo) for name, ref in (('u', wu_ref), ('gp', wgp0_ref), ('gp', wgp1_ref), ('gg', wgg0_ref),
                                              ('gg', wgg1_ref), ('r', wr_ref))
            for lo in range(0, ref.shape[0], PROJ_COL_PIECE)]
    raw = {'u': [], 'gp': [], 'gg': [], 'r': []}
    q_f, q_b, k_f, k_b = [], [], [], []

    def decay_piece(r0):
        zp = z[r0:r0 + PROJ_ROW_PIECE]
        g = (jnp.minimum(zp, 0.0) - jnp.log(1.0 + jnp.exp(-jnp.abs(zp)))) / GATE_TAU
        b_f = _chunk_cumsum(g[:, :GLA_KW], reverse=False)
        b_b = _chunk_cumsum(g[:, GLA_KW:], reverse=True)
        for c in range(r0 // CHUNK, (r0 + PROJ_ROW_PIECE) // CHUNK):
            rows = chunks[c]
            local = slice(c * CHUNK - r0, (c + 1) * CHUNK - r0)
            q, k, bf, bb = q_all[rows], k_all[rows], b_f[local], b_b[local]
            edge_f = bf[CHUNK - 1:CHUNK]
            edge_b = bb[0:1]
            q_f.append((q * jnp.exp(bf)).astype(BF16))
            q_b.append((q * jnp.exp(bb)).astype(BF16))
            k_f.append((k * jnp.exp(-bf)).astype(BF16))
            k_b.append((k * jnp.exp(-bb)).astype(BF16))
            qf_ref[0, rows, :] = q_f[c]
            qb_ref[0, rows, :] = q_b[c]
            kef_ref[0, rows, :] = (k * jnp.exp(edge_f - bf)).astype(BF16)
            keb_ref[0, rows, :] = (k * jnp.exp(edge_b - bb)).astype(BF16)
            decf_ref[0, c:c + 1, :] = jnp.exp(edge_f)
            decb_ref[0, c:c + 1, :] = jnp.exp(edge_b)

    row_pieces = list(range(0, tm, min(PROJ_ROW_PIECE, tm)))
    for i in range(max(len(todo), len(row_pieces))):
        if i < len(todo):
            name, ref, lo = todo[i]
            raw[name].append(_dot_nt(h, ref[lo:lo + PROJ_COL_PIECE, :]))
        if i < len(row_pieces):
            decay_piece(row_pieces[i])
    u_ref[0] = jnp.concatenate(raw['u'], axis=1)
    gp_raw = jnp.concatenate(raw['gp'], axis=1)
    gg_raw = jnp.concatenate(raw['gg'], axis=1)
    r_raw = jnp.concatenate(raw['r'], axis=1)

    gp_ref[0] = _sigmoid(gp_raw).astype(BF16)
    arow = lax.broadcasted_iota(jnp.int32, (CHUNK, GLA_KW), 0)
    acol = lax.broadcasted_iota(jnp.int32, (CHUNK, GLA_KW), 1) % CHUNK
    causal = acol <= arow
    att = [jnp.where(causal, _dot_nt(q_f[c], _head_blocks(k_f[c])), _dot_nt(q_b[c], _head_blocks(k_b[c])))
           .astype(BF16) for c in range(len(chunks))]
    gg_ref[0] = _sigmoid(gg_raw).astype(BF16)

    v_head = lax.broadcasted_iota(jnp.int32, (CHUNK, GLA_VW), 1) // GLA_DV
    zero = jnp.zeros((), BF16)
    for c, rows in enumerate(chunks):
        v_bf = v_all[rows].astype(BF16)
        v_blk = jnp.concatenate([jnp.where(v_head == hd, v_bf, zero) for hd in range(GLA_HEADS)], axis=0)
        oi_ref[0, rows, :] = _dot(att[c], v_blk)
    r_ref[0] = _silu(r_raw).astype(BF16)

    for c, rows in enumerate(chunks):
        v = v_all[rows]
        v_stack = jnp.concatenate([v[:, hd * GLA_DV:(hd + 1) * GLA_DV] for hd in range(GLA_HEADS)], axis=0)
        vst_ref[0, c] = v_stack.T.astype(BF16)


def _proj(x, mod, nw, lw):
    b, t, d = x.shape
    tm = _token_tile(t)
    nc, tc = t // CHUNK, tm // CHUNK
    tok = lambda w: pl.BlockSpec((1, tm, w), lambda b, i: (b, i, 0))
    dec = pl.BlockSpec((1, tc, GLA_KW), lambda b, i: (b, i, 0))
    outs = [
        ('qf', (b, t, GLA_KW), BF16, tok(GLA_KW)), ('kef', (b, t, GLA_KW), BF16, tok(GLA_KW)),
        ('decf', (b, nc, GLA_KW), F32, dec),
        ('qb', (b, t, GLA_KW), BF16, tok(GLA_KW)), ('keb', (b, t, GLA_KW), BF16, tok(GLA_KW)),
        ('decb', (b, nc, GLA_KW), F32, dec),
        ('vst', (b, nc, GLA_DV, GLA_KW), BF16, pl.BlockSpec((1, tc, GLA_DV, GLA_KW), lambda b, i: (b, i, 0, 0))),
        ('oi', (b, t, GLA_VW), F32, tok(GLA_VW)),
        ('r', (b, t, GLA_VW), BF16, tok(GLA_VW)), ('u', (b, t, POOL_WIDTH), F32, tok(POOL_WIDTH)),
        ('gp', (b, t, d), BF16, tok(d)), ('gg', (b, t, d), BF16, tok(d)),
    ]
    weights = [lw[n] for n in _PROJ_WEIGHTS]
    res = pl.pallas_call(
        _proj_kernel,
        grid=(b, t // tm),
        in_specs=[tok(d), pl.BlockSpec((1, N_MOD, d), _per_batch_or_shared(mod)), _resident((1, d))]
        + [_resident(w.shape) for w in weights] + [_resident(lw['b_gate'].shape)],
        out_specs=[o[3] for o in outs],
        out_shape=[jax.ShapeDtypeStruct(o[1], o[2]) for o in outs],
        compiler_params=_params("parallel", "parallel"),
        name="mixer_proj",
    )(x, mod, nw, *weights, lw['b_gate'])
    return {o[0]: a for o, a in zip(outs, res)}


def _gla_kernel(qf_ref, kef_ref, decf_ref, qb_ref, keb_ref, decb_ref, vst_ref, oi_ref, r_ref,
                sf0_ref, sb0_ref, nw_ref, og_ref, sf_ref, sb_ref, st_ref, ob_ref, *, n_blocks, n_chunks):
    step = pl.program_id(1)
    tb = n_chunks * CHUNK

    def chunk(q_ref, ke_ref, dec_ref, c):
        rows = pl.ds(c * CHUNK, CHUNK)
        st = st_ref[...]
        o_stack = _dot_nt(_head_blocks(q_ref[0, rows, :]), st.astype(BF16))
        st_ref[...] = st * dec_ref[0, c:c + 1, :] + _dot(vst_ref[0, c], _head_blocks(ke_ref[0, rows, :]))
        return jnp.concatenate([o_stack[h * CHUNK:(h + 1) * CHUNK] for h in range(GLA_HEADS)], axis=1)

    @pl.when(step == 0)
    def _():
        st_ref[...] = sb0_ref[0]

    @pl.when(step < n_blocks)
    def _():
        base = (n_blocks - 1 - step) * tb
        for c in reversed(range(n_chunks)):
            o = chunk(qb_ref, keb_ref, decb_ref, c)
            ob_ref[pl.ds(pl.multiple_of(base + c * CHUNK, CHUNK), CHUNK), :] = o

    @pl.when(step == n_blocks - 1)
    def _():
        sb_ref[0] = st_ref[...]

    @pl.when(step == n_blocks)
    def _():
        st_ref[...] = sf0_ref[0]

    @pl.when(step >= n_blocks)
    def _():
        base = (step - n_blocks) * tb
        nw = nw_ref[...]
        for c in range(n_chunks):
            rows = pl.ds(c * CHUNK, CHUNK)
            o = chunk(qf_ref, kef_ref, decf_ref, c)
            o = o + ob_ref[pl.ds(pl.multiple_of(base + c * CHUNK, CHUNK), CHUNK), :] + oi_ref[0, rows, :]
            o = jnp.concatenate(
                [_rms(o[:, h * GLA_DV:(h + 1) * GLA_DV]) * nw for h in range(GLA_HEADS)], axis=1)
            og_ref[0, rows, :] = (o * r_ref[0, rows, :].astype(F32)).astype(og_ref.dtype)

    @pl.when(step == 2 * n_blocks - 1)
    def _():
        sf_ref[0] = st_ref[...]


def _gla(p, r, s_f0, s_b0, norm_w):
    b, t, _ = p['qf'].shape
    tb = _token_tile(t, SCAN_TOKEN_TILE)
    nb = t // tb
    tc = tb // CHUNK
    fwd = lambda b, s: (b, jnp.maximum(s - nb, 0), 0)
    rev = lambda b, s: (b, nb - 1 - jnp.minimum(s, nb - 1), 0)
    both = lambda b, s: (b, jnp.where(s < nb, nb - 1 - s, s - nb), 0, 0)
    tok = lambda w, m: pl.BlockSpec((1, tb, w), m)
    dec = lambda m: pl.BlockSpec((1, tc, GLA_KW), m)
    state = pl.BlockSpec((1, GLA_DV, GLA_KW), lambda b, s: (b, 0, 0))
    kern = functools.partial(_gla_kernel, n_blocks=nb, n_chunks=tc)
    return pl.pallas_call(
        kern,
        grid=(b, 2 * nb),
        in_specs=[tok(GLA_KW, fwd), tok(GLA_KW, fwd), dec(fwd), tok(GLA_KW, rev), tok(GLA_KW, rev), dec(rev),
                  pl.BlockSpec((1, tc, GLA_DV, GLA_KW), both), tok(GLA_VW, fwd), tok(GLA_VW, fwd),
                  state, state, _resident((1, GLA_DV))],
        out_specs=[tok(GLA_VW, fwd), state, state],
        out_shape=[jax.ShapeDtypeStruct((b, t, GLA_VW), BF16),
                   jax.ShapeDtypeStruct((b, GLA_DV, GLA_KW), F32),
                   jax.ShapeDtypeStruct((b, GLA_DV, GLA_KW), F32)],
        scratch_shapes=[pltpu.VMEM((GLA_DV, GLA_KW), F32), pltpu.VMEM((t, GLA_VW), F32)],
        compiler_params=_params("arbitrary", "arbitrary"),
        name="gla_scan",
    )(p['qf'], p['kef'], p['decf'], p['qb'], p['keb'], p['decb'], p['vst'], p['oi'], r, s_f0, s_b0, norm_w)


def _col_box_sum(x, n_cols, w):
    half = w // 2
    c_out = lax.broadcasted_iota(jnp.int32, (n_cols, 2 * n_cols), 0)
    c_in = lax.broadcasted_iota(jnp.int32, (n_cols, 2 * n_cols), 1) % n_cols
    band = jnp.where((c_in >= c_out - half) & (c_in < c_out + half), 1.0, 0.0).astype(BF16)
    hi = x.astype(BF16)
    lo = (x - hi.astype(F32)).astype(BF16)
    out = []
    for r in range(x.shape[0] // n_cols):
        rows = slice(r * n_cols, (r + 1) * n_cols)
        out.append(_dot(band, jnp.concatenate([hi[rows], lo[rows]], axis=0)))
    return jnp.concatenate(out, axis=0)


def _row_box_sum(cur, before_ref, after_ref, lanes, tile, n_tiles, n_cols, w):
    half = w // 2
    halo = half * n_cols
    tm = cur.shape[0]
    assert halo <= before_ref.shape[1] and tm % n_cols == 0
    prev = before_ref[0, before_ref.shape[1] - halo:, lanes]
    nxt = after_ref[0, :halo, lanes]
    win = jnp.concatenate([jnp.where(tile > 0, prev, 0.0), cur, jnp.where(tile < n_tiles - 1, nxt, 0.0)],
                          axis=0)
    span = 1
    while span < w:
        win = win[:-span * n_cols] + win[span * n_cols:]
        span *= 2
    return win[:tm]


def _merge_kernel(x_ref, mod_ref, og_ref, u_ref, ubefore_ref, uafter_ref, gp_ref, gg_ref, wbg_ref, wbp_ref, wo_ref,
                  pw_ref, ps_ref, o_ref, *, n_rows, n_cols, n_tiles):
    tile = pl.program_id(1)
    tm = x_ref.shape[1]
    y_gla = _dot(og_ref[0], wbg_ref[...])
    tok = lax.broadcasted_iota(jnp.int32, (tm, POOL_GROUP), 0)
    col = (tok % n_cols).astype(F32)
    row = (tile * (tm // n_cols) + tok // n_cols).astype(F32)
    mixed = []
    for g, w in enumerate(POOL_WINDOWS):
        half = float(w // 2)
        lanes = slice(g * POOL_GROUP, (g + 1) * POOL_GROUP)
        u = u_ref[0, :, lanes]
        cnt = jnp.minimum(col, half) + jnp.minimum(n_cols - col, half)
        m = u
        if n_rows > 1:
            cnt = cnt * (jnp.minimum(row, half) + jnp.minimum(n_rows - row, half))
            m = _row_box_sum(u, ubefore_ref, uafter_ref, lanes, tile, n_tiles, n_cols, w)
        m = _col_box_sum(m, n_cols, w) / cnt
        mixed.append((_dot((m - u).astype(BF16), pw_ref[lanes, :]) * ps_ref[:, lanes]).astype(BF16))
    y_pool = _dot(jnp.concatenate(mixed, axis=1), wbp_ref[...])
    merged = gp_ref[0].astype(F32) * y_pool + gg_ref[0].astype(F32) * y_gla
    gate = mod_ref[0][5:6]
    o_ref[0] = x_ref[0] + gate * _dot(merged.astype(BF16), wo_ref[...])


def _merge(x, mod, og, u, gp, gg, lw, *, n_rows, n_cols):
    b, t, d = x.shape
    tm = _token_tile(t, MERGE_TOKEN_TILE)
    tok = lambda w: pl.BlockSpec((1, tm, w), lambda b, i: (b, i, 0))
    halo = min(tm, max(POOL_WINDOWS) // 2 * n_cols)
    per_tile, last = tm // halo, t // halo - 1
    before = pl.BlockSpec((1, halo, POOL_WIDTH), lambda b, i: (b, jnp.maximum(i * per_tile - 1, 0), 0))
    after = pl.BlockSpec((1, halo, POOL_WIDTH), lambda b, i: (b, jnp.minimum((i + 1) * per_tile, last), 0))
    weights = [lw['w_br_gla'], lw['w_br_pool'], lw['w_out'], lw['pool_w']]
    kern = functools.partial(_merge_kernel, n_rows=n_rows, n_cols=n_cols, n_tiles=t // tm)
    return pl.pallas_call(
        kern,
        grid=(b, t // tm),
        in_specs=[tok(d), pl.BlockSpec((1, N_MOD, d), _per_batch_or_shared(mod)),
                  tok(GLA_VW), tok(POOL_WIDTH), before, after, tok(d), tok(d)]
        + [_resident(w.shape) for w in weights] + [_resident(lw['pool_scale'].shape)],
        out_specs=tok(d),
        out_shape=jax.ShapeDtypeStruct(x.shape, F32),
        compiler_params=_params("parallel", "parallel"),
        name="mixer_merge",
    )(x, mod, og, u, u, u, gp, gg, *weights, lw['pool_scale'])


_HALF_D = D_MODEL // 2
_W_IN_PARTS = (('w_k', OFF_K, GLA_KW), ('w_v', OFF_V, GLA_VW), ('w_q', OFF_Q, GLA_KW), ('w_r', OFF_R, GLA_VW),
               ('w_u', OFF_POOL, POOL_WIDTH), ('w_gp0', OFF_GP, _HALF_D), ('w_gp1', OFF_GP + _HALF_D, _HALF_D),
               ('w_gg0', OFF_GG, _HALF_D), ('w_gg1', OFF_GG + _HALF_D, _HALF_D))
_MIXER_MATS = ('w_br_gla', 'w_br_pool', 'w_out', 'pool_w')


def _mixer_cast_jobs(l, w_in_t, w_br_gla, w_br_pool, w_out, pool_w):
    whole = lambda w: (w, l, 0, w.shape[1])
    return ([(w_in_t, l, start, n) for _, start, n in _W_IN_PARTS]
            + [whole(w_br_gla), whole(w_br_pool), whole(w_out), whole(pool_w.reshape(DEPTH, POOL_WIDTH, POOL_GROUP))])


def _mixer(x, mod, nw, lw, s_f0, s_b0, gla_norm_w, *, n_rows, n_cols, states_only=False):
    p = _proj(x, mod, nw, lw)
    og, s_f, s_b = _gla(p, p['r'], s_f0, s_b0, gla_norm_w)
    if states_only:
        return None, s_f, s_b
    return _merge(x, mod, og, p['u'], p['gp'], p['gg'], lw, n_rows=n_rows, n_cols=n_cols), s_f, s_b


def kernel(x, c, ctx, c_ctx, w_mod, b_mod, norm_w, ffn1_up, ffn1_down, w_in, w_af_up, b_af, w_ab_up, b_ab,
           gla_norm_w, pool_w, pool_scale, w_br_pool, w_br_gla, w_out, ffn2_up, ffn2_down, final_norm_w):
    batch, seq, d = x.shape
    ctx_len = ctx.shape[1]
    cond = jnp.concatenate([c, c_ctx[None, :], jnp.zeros((SUBLANES - batch - 1, d), F32)], axis=0)
    mod_all = _modulation(cond, w_mod, b_mod)
    zero_state = jnp.zeros((batch, GLA_DV, GLA_KW), F32)
    fw = final_norm_w[None, :]
    zeros = jnp.zeros((GATE_RANK, GLA_KW), F32)
    up1, down1 = ffn1_up[0].astype(BF16), ffn1_down[0].astype(BF16)
    whole = lambda w, l: (w, l, 0, w.shape[1])
    w_in_t = jnp.swapaxes(w_in, 1, 2)
    for l in range(DEPTH):
        last = l == DEPTH - 1
        mod_x = mod_all[l, :batch].reshape(batch, N_MOD, d)
        mod_c = mod_all[l, batch:batch + 1].reshape(1, N_MOD, d)
        nw = norm_w[l][:, None, :]
        gnw = gla_norm_w[l][None, :]
        lw = {
            'w_gate': jnp.concatenate([jnp.concatenate([w_af_up[l], zeros], axis=1),
                                       jnp.concatenate([zeros, w_ab_up[l]], axis=1)], axis=0).astype(BF16),
            'b_gate': jnp.concatenate([b_af[l], b_ab[l]])[None, :],
            'pool_scale': pool_scale[l][None, :],
            'w_a': w_in_t[l, OFF_AF:OFF_Q],
        }

        x, cast = _ffn(x, mod_x, nw[0], up1, down1, fw, mod_row=0, final_norm=False,
                       cast=[whole(ffn2_up, l), whole(ffn2_down, l)]
                       + _mixer_cast_jobs(l, w_in_t, w_br_gla, w_br_pool, w_out, pool_w))
        up2, down2 = cast[:2]
        lw.update(zip([n for n, _, _ in _W_IN_PARTS] + list(_MIXER_MATS), cast[2:]))
        ctx, _ = _ffn(ctx, mod_c, nw[0], up1, down1, fw, mod_row=0, final_norm=False)

        ctx_mixed, s_f, s_b = _mixer(ctx, mod_c, nw[1], lw, zero_state, zero_state, gnw,
                                     n_rows=1, n_cols=ctx_len, states_only=last)
        x, _, _ = _mixer(x, mod_x, nw[1], lw, s_f, s_b, gnw, n_rows=seq // GRID_W, n_cols=GRID_W)
        x, cast = _ffn(x, mod_x, nw[2], up2, down2, fw, mod_row=6, final_norm=last,
                       cast=[] if last else [whole(ffn1_up, l + 1), whole(ffn1_down, l + 1)])
        if not last:
            ctx, _ = _ffn(ctx_mixed, mod_c, nw[2], up2, down2, fw, mod_row=6, final_norm=False)
            up1, down1 = cast
    return x
```

```python
import functools

import jax
import jax.numpy as jnp
from jax import lax
from jax.experimental import pallas as pl
from jax.experimental.pallas import tpu as pltpu

F32 = jnp.float32
BF16 = jnp.bfloat16

D_MODEL = 1024
DEPTH = 2
GRID_W = 64
N_POOL_GROUPS = 4
POOL_GROUP = 128
POOL_WIDTH = N_POOL_GROUPS * POOL_GROUP
POOL_WINDOWS = (2, 4, 8, 16)
GLA_HEADS = 4
GLA_DK = 64
GLA_DV = 128
GLA_KW = GLA_HEADS * GLA_DK
GLA_VW = GLA_HEADS * GLA_DV
GATE_RANK = 16
GATE_TAU = 16.0
CHUNK = 64
D_FF = 2816
N_MOD = 9
EPS = 1e-6

OFF_K = 0
OFF_V = OFF_K + GLA_KW
OFF_AF = OFF_V + GLA_VW
OFF_AB = OFF_AF + GATE_RANK
OFF_Q = OFF_AB + GATE_RANK
OFF_R = OFF_Q + GLA_KW
OFF_POOL = OFF_R + GLA_VW
OFF_GP = OFF_POOL + POOL_WIDTH
OFF_GG = OFF_GP + D_MODEL
IN_COLS = OFF_GG + D_MODEL

V7X_VMEM_BYTES = 64 * 1024 * 1024
VMEM_LIMIT_BYTES = V7X_VMEM_BYTES * 7 // 8
SUBLANES = 8
TOKEN_TILE = 1024
FFN_TOKEN_TILE = 1024
SCAN_TOKEN_TILE = 1024
MERGE_TOKEN_TILE = 1024
FF_CHUNK = 256
NORM_PART = 256
PROJ_NORM_PART = 512
PROJ_ROW_PIECE = 128
PROJ_COL_PIECE = 256
MOD_COL_TILE = 2304


def _resident(shape):
    return pl.BlockSpec(shape, lambda *_: (0,) * len(shape), pipeline_mode=pl.Buffered(1))


def _params(*semantics):
    return pltpu.CompilerParams(dimension_semantics=semantics, vmem_limit_bytes=VMEM_LIMIT_BYTES)


def _per_batch_or_shared(arr):
    zeros = (0,) * (arr.ndim - 1)
    if arr.shape[0] == 1:
        return lambda b, i: (0,) + zeros
    return lambda b, i: (b,) + zeros


def _token_tile(t, tile=TOKEN_TILE):
    return min(tile, t)


def _rms(x):
    return x * lax.rsqrt(jnp.mean(x * x, axis=-1, keepdims=True) + EPS)


def _modulated_norm(x, w, shift, scale):
    return (_rms(x) * w) * (1.0 + scale) + shift


def _silu(a):
    return a / (1.0 + jnp.exp(-a))


def _sigmoid(a):
    return 1.0 / (1.0 + jnp.exp(-a))


def _dot(a, b):
    return jnp.dot(a, b, preferred_element_type=F32)


def _dot_nt(a, b):
    return lax.dot_general(a, b, (((1,), (1,)), ((), ())), preferred_element_type=F32)


def _head_blocks(x):
    head = lax.broadcasted_iota(jnp.int32, x.shape, 1) // GLA_DK
    zero = jnp.zeros((), x.dtype)
    return jnp.concatenate([jnp.where(head == h, x, zero) for h in range(GLA_HEADS)], axis=0)


def _chunk_cumsum(g, *, reverse):
    n = g.shape[0]
    pos = lax.broadcasted_iota(jnp.int32, g.shape, 0) % CHUNK
    acc = g
    span = 1
    while span < CHUNK:
        if reverse:
            acc = acc + jnp.where(pos + span < CHUNK, pltpu.roll(acc, n - span, axis=0), 0.0)
        else:
            acc = acc + jnp.where(pos >= span, pltpu.roll(acc, span, axis=0), 0.0)
        span *= 2
    return acc


def _mod_kernel(cond_ref, w_ref, b_ref, o_ref):
    cond = cond_ref[...]
    o_ref[0] = _dot(_silu(cond).astype(BF16), w_ref[0].astype(BF16)) + b_ref[0]


def _modulation(cond, w_mod, b_mod):
    n_cols = N_MOD * D_MODEL
    rows = cond.shape[0]
    return pl.pallas_call(
        _mod_kernel,
        grid=(DEPTH, n_cols // MOD_COL_TILE),
        in_specs=[
            pl.BlockSpec((rows, D_MODEL), lambda l, j: (0, 0)),
            pl.BlockSpec((1, D_MODEL, MOD_COL_TILE), lambda l, j: (l, 0, j)),
            pl.BlockSpec((1, 1, MOD_COL_TILE), lambda l, j: (l, 0, j)),
        ],
        out_specs=pl.BlockSpec((1, rows, MOD_COL_TILE), lambda l, j: (l, 0, j)),
        out_shape=jax.ShapeDtypeStruct((DEPTH, rows, n_cols), F32),
        compiler_params=_params("arbitrary", "arbitrary"),
        name="modulation",
    )(cond, w_mod, b_mod.reshape(DEPTH, 1, n_cols))


def _ffn_kernel(*refs, mod_row, final_norm, n_cast):
    x_ref, mod_ref, nw_ref, up_ref, down_ref, fw_ref = refs[:6]
    cast_in = refs[6:6 + n_cast]
    o_ref = refs[6 + n_cast]
    cast_out = refs[7 + n_cast:7 + 2 * n_cast]
    act_ref = refs[7 + 2 * n_cast]
    x = x_ref[0]
    mod = mod_ref[0]
    shift = mod[mod_row:mod_row + 1]
    scale = mod[mod_row + 1:mod_row + 2]
    gate = mod[mod_row + 2:mod_row + 3]
    parts = [slice(r, r + NORM_PART) for r in range(0, x.shape[0], min(NORM_PART, x.shape[0]))]
    h_parts, a_parts, b_parts = [], [], []
    for p in parts:
        h_parts.append(_modulated_norm(x[p], nw_ref[...], shift, scale).astype(BF16))
        a_parts.append(_dot(h_parts[-1], up_ref[:, :FF_CHUNK]))
        b_parts.append(_dot(h_parts[-1], up_ref[:, D_FF:D_FF + FF_CHUNK]))
    h = jnp.concatenate(h_parts, axis=0)
    n_chunks = D_FF // FF_CHUNK
    for j in range(n_chunks):
        lo = j * FF_CHUNK
        if j == 0:
            a = jnp.concatenate(a_parts, axis=0)
            b = jnp.concatenate(b_parts, axis=0)
        else:
            a = _dot(h, up_ref[:, lo:lo + FF_CHUNK])
            b = _dot(h, up_ref[:, D_FF + lo:D_FF + lo + FF_CHUNK])
        act_ref[:, lo:lo + FF_CHUNK] = (_silu(a) * b).astype(BF16)
        for k in range(j, n_cast, n_chunks):
            cast_out[k][...] = cast_in[k][0].astype(BF16)
    y = x + (0.5 * gate) * _dot(act_ref[...], down_ref[...])
    if final_norm:
        y = _rms(y) * fw_ref[...]
    o_ref[0] = y


def _ffn(x, mod, nw, up, down, fw, *, mod_row, final_norm, cast=()):
    b, t, d = x.shape
    tm = _token_tile(t, FFN_TOKEN_TILE)
    nt = t // tm
    steps = b * nt
    cast_specs, out_specs, out_shapes = [], [], []
    for w, l, start, n_rows in cast:
        rows = n_rows // steps
        assert rows * steps == n_rows and rows % (2 * SUBLANES) == 0 and start % rows == 0, (w.shape, start)
        cast_specs.append(pl.BlockSpec((1, rows, w.shape[2]),
                                       lambda b, i, l=l, first=start // rows: (l, first + b * nt + i, 0)))
        out_specs.append(pl.BlockSpec((rows, w.shape[2]), lambda b, i: (b * nt + i, 0)))
        out_shapes.append(jax.ShapeDtypeStruct((n_rows, w.shape[2]), BF16))
    kern = functools.partial(_ffn_kernel, mod_row=mod_row, final_norm=final_norm, n_cast=len(cast))
    res = pl.pallas_call(
        kern,
        grid=(b, nt),
        in_specs=[
            pl.BlockSpec((1, tm, d), lambda b, i: (b, i, 0)),
            pl.BlockSpec((1, N_MOD, d), _per_batch_or_shared(mod)),
            _resident((1, d)),
            _resident(up.shape),
            _resident(down.shape),
            _resident((1, d)),
        ] + cast_specs,
        out_specs=[pl.BlockSpec((1, tm, d), lambda b, i: (b, i, 0))] + out_specs,
        out_shape=[jax.ShapeDtypeStruct(x.shape, F32)] + out_shapes,
        scratch_shapes=[pltpu.VMEM((tm, D_FF), BF16)],
        compiler_params=_params("parallel", "parallel"),
        name="ffn",
    )(x, mod, nw, up, down, fw, *[job[0] for job in cast])
    return res[0], list(res[1:])


_PROJ_WEIGHTS = ('w_k', 'w_v', 'w_a', 'w_q', 'w_r', 'w_u', 'w_gp0', 'w_gp1', 'w_gg0', 'w_gg1', 'w_gate')


def _proj_kernel(x_ref, mod_ref, nw_ref, wk_ref, wv_ref, wa_ref, wq_ref, wr_ref, wu_ref, wgp0_ref, wgp1_ref,
                 wgg0_ref, wgg1_ref, wgate_ref, bgate_ref,
                 qf_ref, kef_ref, decf_ref, qb_ref, keb_ref, decb_ref, vst_ref, oi_ref,
                 r_ref, u_ref, gp_ref, gg_ref):
    x = x_ref[0]
    mod = mod_ref[0]
    tm = x.shape[0]
    chunks = [slice(c * CHUNK, (c + 1) * CHUNK) for c in range(tm // CHUNK)]
    h_parts, z_parts, k_parts, q_parts, v_parts = [], [], [], [], []
    wa = wa_ref[...].astype(BF16)
    for r in range(0, tm, min(PROJ_NORM_PART, tm)):
        hp = _modulated_norm(x[r:r + PROJ_NORM_PART], nw_ref[...], mod[3:4], mod[4:5]).astype(BF16)
        h_parts.append(hp)
        z_parts.append(_dot(_dot_nt(hp, wa).astype(BF16), wgate_ref[...]) + bgate_ref[...])
        k_parts.append(_dot_nt(hp, wk_ref[...]))
        q_parts.append(_dot_nt(hp, wq_ref[...]))
        v_parts.append(_dot_nt(hp, wv_ref[...]))
    h = jnp.concatenate(h_parts, axis=0)
    z = jnp.concatenate(z_parts, axis=0)
    k_all = jnp.concatenate(k_parts, axis=0)
    q_all = jnp.concatenate(q_parts, axis=0) * (GLA_DK ** -0.5)
    v_all = jnp.concatenate(v_parts, axis=0)
    todo = [(name, ref, lo) for name, ref in (('u', wu_ref), ('gp', wgp0_ref), ('gp', wgp1_ref), ('gg', wgg0_ref),
                                              ('gg', wgg1_ref), ('r', wr_ref))
            for lo in range(0, ref.shape[0], PROJ_COL_PIECE)]
    raw = {'u': [], 'gp': [], 'gg': [], 'r': []}
    q_f, q_b, k_f, k_b = [], [], [], []

    def decay_piece(r0):
        zp = z[r0:r0 + PROJ_ROW_PIECE]
        g = (jnp.minimum(zp, 0.0) - jnp.log(1.0 + jnp.exp(-jnp.abs(zp)))) / GATE_TAU
        b_f = _chunk_cumsum(g[:, :GLA_KW], reverse=False)
        b_b = _chunk_cumsum(g[:, GLA_KW:], reverse=True)
        for c in range(r0 // CHUNK, (r0 + PROJ_ROW_PIECE) // CHUNK):
            rows = chunks[c]
            local = slice(c * CHUNK - r0, (c + 1) * CHUNK - r0)
            q, k, bf, bb = q_all[rows], k_all[rows], b_f[local], b_b[local]
            edge_f = bf[CHUNK - 1:CHUNK]
            edge_b = bb[0:1]
            q_f.append((q * jnp.exp(bf)).astype(BF16))
            q_b.append((q * jnp.exp(bb)).astype(BF16))
            k_f.append((k * jnp.exp(-bf)).astype(BF16))
            k_b.append((k * jnp.exp(-bb)).astype(BF16))
            qf_ref[0, rows, :] = q_f[c]
            qb_ref[0, rows, :] = q_b[c]
            kef_ref[0, rows, :] = (k * jnp.exp(edge_f - bf)).astype(BF16)
            keb_ref[0, rows, :] = (k * jnp.exp(edge_b - bb)).astype(BF16)
            decf_ref[0, c:c + 1, :] = jnp.exp(edge_f)
            decb_ref[0, c:c + 1, :] = jnp.exp(edge_b)

    row_pieces = list(range(0, tm, min(PROJ_ROW_PIECE, tm)))
    for i in range(max(len(todo), len(row_pieces))):
        if i < len(todo):
            name, ref, lo = todo[i]
            raw[name].append(_dot_nt(h, ref[lo:lo + PROJ_COL_PIECE, :]))
        if i < len(row_pieces):
            decay_piece(row_pieces[i])
    u_ref[0] = jnp.concatenate(raw['u'], axis=1)
    gp_raw = jnp.concatenate(raw['gp'], axis=1)
    gg_raw = jnp.concatenate(raw['gg'], axis=1)
    r_raw = jnp.concatenate(raw['r'], axis=1)

    gp_ref[0] = _sigmoid(gp_raw).astype(BF16)
    arow = lax.broadcasted_iota(jnp.int32, (CHUNK, GLA_KW), 0)
    acol = lax.broadcasted_iota(jnp.int32, (CHUNK, GLA_KW), 1) % CHUNK
    causal = acol <= arow
    att = [jnp.where(causal, _dot_nt(q_f[c], _head_blocks(k_f[c])), _dot_nt(q_b[c], _head_blocks(k_b[c])))
           .astype(BF16) for c in range(len(chunks))]
    gg_ref[0] = _sigmoid(gg_raw).astype(BF16)

    v_head = lax.broadcasted_iota(jnp.int32, (CHUNK, GLA_VW), 1) // GLA_DV
    zero = jnp.zeros((), BF16)
    for c, rows in enumerate(chunks):
        v_bf = v_all[rows].astype(BF16)
        v_blk = jnp.concatenate([jnp.where(v_head == hd, v_bf, zero) for hd in range(GLA_HEADS)], axis=0)
        oi_ref[0, rows, :] = _dot(att[c], v_blk)
    r_ref[0] = _silu(r_raw).astype(BF16)

    for c, rows in enumerate(chunks):
        v = v_all[rows]
        v_stack = jnp.concatenate([v[:, hd * GLA_DV:(hd + 1) * GLA_DV] for hd in range(GLA_HEADS)], axis=0)
        vst_ref[0, c] = v_stack.T.astype(BF16)


def _proj(x, mod, nw, lw):
    b, t, d = x.shape
    tm = _token_tile(t)
    nc, tc = t // CHUNK, tm // CHUNK
    tok = lambda w: pl.BlockSpec((1, tm, w), lambda b, i: (b, i, 0))
    dec = pl.BlockSpec((1, tc, GLA_KW), lambda b, i: (b, i, 0))
    outs = [
        ('qf', (b, t, GLA_KW), BF16, tok(GLA_KW)), ('kef', (b, t, GLA_KW), BF16, tok(GLA_KW)),
        ('decf', (b, nc, GLA_KW), F32, dec),
        ('qb', (b, t, GLA_KW), BF16, tok(GLA_KW)), ('keb', (b, t, GLA_KW), BF16, tok(GLA_KW)),
        ('decb', (b, nc, GLA_KW), F32, dec),
        ('vst', (b, nc, GLA_DV, GLA_KW), BF16, pl.BlockSpec((1, tc, GLA_DV, GLA_KW), lambda b, i: (b, i, 0, 0))),
        ('oi', (b, t, GLA_VW), F32, tok(GLA_VW)),
        ('r', (b, t, GLA_VW), BF16, tok(GLA_VW)), ('u', (b, t, POOL_WIDTH), F32, tok(POOL_WIDTH)),
        ('gp', (b, t, d), BF16, tok(d)), ('gg', (b, t, d), BF16, tok(d)),
    ]
    weights = [lw[n] for n in _PROJ_WEIGHTS]
    res = pl.pallas_call(
        _proj_kernel,
        grid=(b, t // tm),
        in_specs=[tok(d), pl.BlockSpec((1, N_MOD, d), _per_batch_or_shared(mod)), _resident((1, d))]
        + [_resident(w.shape) for w in weights] + [_resident(lw['b_gate'].shape)],
        out_specs=[o[3] for o in outs],
        out_shape=[jax.ShapeDtypeStruct(o[1], o[2]) for o in outs],
        compiler_params=_params("parallel", "parallel"),
        name="mixer_proj",
    )(x, mod, nw, *weights, lw['b_gate'])
    return {o[0]: a for o, a in zip(outs, res)}


def _gla_kernel(qf_ref, kef_ref, decf_ref, qb_ref, keb_ref, decb_ref, vst_ref, oi_ref, r_ref,
                sf0_ref, sb0_ref, nw_ref, og_ref, sf_ref, sb_ref, st_ref, ob_ref, *, n_blocks, n_chunks):
    step = pl.program_id(1)
    tb = n_chunks * CHUNK

    def chunk(q_ref, ke_ref, dec_ref, c):
        rows = pl.ds(c * CHUNK, CHUNK)
        st = st_ref[...]
        o_stack = _dot_nt(_head_blocks(q_ref[0, rows, :]), st.astype(BF16))
        st_ref[...] = st * dec_ref[0, c:c + 1, :] + _dot(vst_ref[0, c], _head_blocks(ke_ref[0, rows, :]))
        return jnp.concatenate([o_stack[h * CHUNK:(h + 1) * CHUNK] for h in range(GLA_HEADS)], axis=1)

    @pl.when(step == 0)
    def _():
        st_ref[...] = sb0_ref[0]

    @pl.when(step < n_blocks)
    def _():
        base = (n_blocks - 1 - step) * tb
        for c in reversed(range(n_chunks)):
            o = chunk(qb_ref, keb_ref, decb_ref, c)
            ob_ref[pl.ds(pl.multiple_of(base + c * CHUNK, CHUNK), CHUNK), :] = o

    @pl.when(step == n_blocks - 1)
    def _():
        sb_ref[0] = st_ref[...]

    @pl.when(step == n_blocks)
    def _():
        st_ref[...] = sf0_ref[0]

    @pl.when(step >= n_blocks)
    def _():
        base = (step - n_blocks) * tb
        nw = nw_ref[...]
        for c in range(n_chunks):
            rows = pl.ds(c * CHUNK, CHUNK)
            o = chunk(qf_ref, kef_ref, decf_ref, c)
            o = o + ob_ref[pl.ds(pl.multiple_of(base + c * CHUNK, CHUNK), CHUNK), :] + oi_ref[0, rows, :]
            o = jnp.concatenate(
                [_rms(o[:, h * GLA_DV:(h + 1) * GLA_DV]) * nw for h in range(GLA_HEADS)], axis=1)
            og_ref[0, rows, :] = (o * r_ref[0, rows, :].astype(F32)).astype(og_ref.dtype)

    @pl.when(step == 2 * n_blocks - 1)
    def _():
        sf_ref[0] = st_ref[...]


def _gla(p, r, s_f0, s_b0, norm_w):
    b, t, _ = p['qf'].shape
    tb = _token_tile(t, SCAN_TOKEN_TILE)
    nb = t // tb
    tc = tb // CHUNK
    fwd = lambda b, s: (b, jnp.maximum(s - nb, 0), 0)
    rev = lambda b, s: (b, nb - 1 - jnp.minimum(s, nb - 1), 0)
    both = lambda b, s: (b, jnp.where(s < nb, nb - 1 - s, s - nb), 0, 0)
    tok = lambda w, m: pl.BlockSpec((1, tb, w), m)
    dec = lambda m: pl.BlockSpec((1, tc, GLA_KW), m)
    state = pl.BlockSpec((1, GLA_DV, GLA_KW), lambda b, s: (b, 0, 0))
    kern = functools.partial(_gla_kernel, n_blocks=nb, n_chunks=tc)
    return pl.pallas_call(
        kern,
        grid=(b, 2 * nb),
        in_specs=[tok(GLA_KW, fwd), tok(GLA_KW, fwd), dec(fwd), tok(GLA_KW, rev), tok(GLA_KW, rev), dec(rev),
                  pl.BlockSpec((1, tc, GLA_DV, GLA_KW), both), tok(GLA_VW, fwd), tok(GLA_VW, fwd),
                  state, state, _resident((1, GLA_DV))],
        out_specs=[tok(GLA_VW, fwd), state, state],
        out_shape=[jax.ShapeDtypeStruct((b, t, GLA_VW), BF16),
                   jax.ShapeDtypeStruct((b, GLA_DV, GLA_KW), F32),
                   jax.ShapeDtypeStruct((b, GLA_DV, GLA_KW), F32)],
        scratch_shapes=[pltpu.VMEM((GLA_DV, GLA_KW), F32), pltpu.VMEM((t, GLA_VW), F32)],
        compiler_params=_params("arbitrary", "arbitrary"),
        name="gla_scan",
    )(p['qf'], p['kef'], p['decf'], p['qb'], p['keb'], p['decb'], p['vst'], p['oi'], r, s_f0, s_b0, norm_w)


def _col_box_sum(x, n_cols, w):
    half = w // 2
    c_out = lax.broadcasted_iota(jnp.int32, (n_cols, 2 * n_cols), 0)
    c_in = lax.broadcasted_iota(jnp.int32, (n_cols, 2 * n_cols), 1) % n_cols
    band = jnp.where((c_in >= c_out - half) & (c_in < c_out + half), 1.0, 0.0).astype(BF16)
    hi = x.astype(BF16)
    lo = (x - hi.astype(F32)).astype(BF16)
    out = []
    for r in range(x.shape[0] // n_cols):
        rows = slice(r * n_cols, (r + 1) * n_cols)
        out.append(_dot(band, jnp.concatenate([hi[rows], lo[rows]], axis=0)))
    return jnp.concatenate(out, axis=0)


def _row_box_sum(cur, before_ref, after_ref, lanes, tile, n_tiles, n_cols, w):
    half = w // 2
    halo = half * n_cols
    tm = cur.shape[0]
    assert halo <= before_ref.shape[1] and tm % n_cols == 0
    prev = before_ref[0, before_ref.shape[1] - halo:, lanes]
    nxt = after_ref[0, :halo, lanes]
    win = jnp.concatenate([jnp.where(tile > 0, prev, 0.0), cur, jnp.where(tile < n_tiles - 1, nxt, 0.0)],
                          axis=0)
    span = 1
    while span < w:
        win = win[:-span * n_cols] + win[span * n_cols:]
        span *= 2
    return win[:tm]


def _merge_kernel(x_ref, mod_ref, og_ref, u_ref, ubefore_ref, uafter_ref, gp_ref, gg_ref, wbg_ref, wbp_ref, wo_ref,
                  pw_ref, ps_ref, o_ref, *, n_rows, n_cols, n_tiles):
    tile = pl.program_id(1)
    tm = x_ref.shape[1]
    y_gla = _dot(og_ref[0], wbg_ref[...])
    tok = lax.broadcasted_iota(jnp.int32, (tm, POOL_GROUP), 0)
    col = (tok % n_cols).astype(F32)
    row = (tile * (tm // n_cols) + tok // n_cols).astype(F32)
    mixed = []
    for g, w in enumerate(POOL_WINDOWS):
        half = float(w // 2)
        lanes = slice(g * POOL_GROUP, (g + 1) * POOL_GROUP)
        u = u_ref[0, :, lanes]
        cnt = jnp.minimum(col, half) + jnp.minimum(n_cols - col, half)
        m = u
        if n_rows > 1:
            cnt = cnt * (jnp.minimum(row, half) + jnp.minimum(n_rows - row, half))
            m = _row_box_sum(u, ubefore_ref, uafter_ref, lanes, tile, n_tiles, n_cols, w)
        m = _col_box_sum(m, n_cols, w) / cnt
        mixed.append((_dot((m - u).astype(BF16), pw_ref[lanes, :]) * ps_ref[:, lanes]).astype(BF16))
    y_pool = _dot(jnp.concatenate(mixed, axis=1), wbp_ref[...])
    merged = gp_ref[0].astype(F32) * y_pool + gg_ref[0].astype(F32) * y_gla
    gate = mod_ref[0][5:6]
    o_ref[0] = x_ref[0] + gate * _dot(merged.astype(BF16), wo_ref[...])


def _merge(x, mod, og, u, gp, gg, lw, *, n_rows, n_cols):
    b, t, d = x.shape
    tm = _token_tile(t, MERGE_TOKEN_TILE)
    tok = lambda w: pl.BlockSpec((1, tm, w), lambda b, i: (b, i, 0))
    halo = min(tm, max(POOL_WINDOWS) // 2 * n_cols)
    per_tile, last = tm // halo, t // halo - 1
    before = pl.BlockSpec((1, halo, POOL_WIDTH), lambda b, i: (b, jnp.maximum(i * per_tile - 1, 0), 0))
    after = pl.BlockSpec((1, halo, POOL_WIDTH), lambda b, i: (b, jnp.minimum((i + 1) * per_tile, last), 0))
    weights = [lw['w_br_gla'], lw['w_br_pool'], lw['w_out'], lw['pool_w']]
    kern = functools.partial(_merge_kernel, n_rows=n_rows, n_cols=n_cols, n_tiles=t // tm)
    return pl.pallas_call(
        kern,
        grid=(b, t // tm),
        in_specs=[tok(d), pl.BlockSpec((1, N_MOD, d), _per_batch_or_shared(mod)),
                  tok(GLA_VW), tok(POOL_WIDTH), before, after, tok(d), tok(d)]
        + [_resident(w.shape) for w in weights] + [_resident(lw['pool_scale'].shape)],
        out_specs=tok(d),
        out_shape=jax.ShapeDtypeStruct(x.shape, F32),
        compiler_params=_params("parallel", "parallel"),
        name="mixer_merge",
    )(x, mod, og, u, u, u, gp, gg, *weights, lw['pool_scale'])


_HALF_D = D_MODEL // 2
_W_IN_PARTS = (('w_k', OFF_K, GLA_KW), ('w_v', OFF_V, GLA_VW), ('w_q', OFF_Q, GLA_KW), ('w_r', OFF_R, GLA_VW),
               ('w_u', OFF_POOL, POOL_WIDTH), ('w_gp0', OFF_GP, _HALF_D), ('w_gp1', OFF_GP + _HALF_D, _HALF_D),
               ('w_gg0', OFF_GG, _HALF_D), ('w_gg1', OFF_GG + _HALF_D, _HALF_D))
_MIXER_MATS = ('w_br_gla', 'w_br_pool', 'w_out', 'pool_w')


def _mixer_cast_jobs(l, w_in_t, w_br_gla, w_br_pool, w_out, pool_w):
    whole = lambda w: (w, l, 0, w.shape[1])
    return ([(w_in_t, l, start, n) for _, start, n in _W_IN_PARTS]
            + [whole(w_br_gla), whole(w_br_pool), whole(w_out), whole(pool_w.reshape(DEPTH, POOL_WIDTH, POOL_GROUP))])


def _mixer(x, mod, nw, lw, s_f0, s_b0, gla_norm_w, *, n_rows, n_cols, states_only=False, sequences=None):
    p = _proj(x, mod, nw, lw)
    if sequences is not None:
        p = {k: v.reshape((sequences, v.shape[1] // sequences) + v.shape[2:]) for k, v in p.items()}
    og, s_f, s_b = _gla(p, p['r'], s_f0, s_b0, gla_norm_w)
    if sequences is not None:
        og = og.reshape((1, -1) + og.shape[2:])
        p = {k: v.reshape((1, -1) + v.shape[2:]) for k, v in p.items()}
    if states_only:
        return None, s_f, s_b
    return _merge(x, mod, og, p['u'], p['gp'], p['gg'], lw, n_rows=n_rows, n_cols=n_cols), s_f, s_b


def kernel(x, c, ctx, c_ctx, w_mod, b_mod, norm_w, ffn1_up, ffn1_down, w_in, w_af_up, b_af, w_ab_up, b_ab,
           gla_norm_w, pool_w, pool_scale, w_br_pool, w_br_gla, w_out, ffn2_up, ffn2_down, final_norm_w):
    batch, seq, d = x.shape
    ctx_len = ctx.shape[1]
    ctx = ctx.reshape(1, batch * ctx_len, d)
    cond = jnp.concatenate([c, c_ctx[None, :], jnp.zeros((SUBLANES - batch - 1, d), F32)], axis=0)
    mod_all = _modulation(cond, w_mod, b_mod)
    zero_state = jnp.zeros((batch, GLA_DV, GLA_KW), F32)
    fw = final_norm_w[None, :]
    zeros = jnp.zeros((GATE_RANK, GLA_KW), F32)
    up1, down1 = ffn1_up[0].astype(BF16), ffn1_down[0].astype(BF16)
    whole = lambda w, l: (w, l, 0, w.shape[1])
    w_in_t = jnp.swapaxes(w_in, 1, 2)
    for l in range(DEPTH):
        last = l == DEPTH - 1
        mod_x = mod_all[l, :batch].reshape(batch, N_MOD, d)
        mod_c = mod_all[l, batch:batch + 1].reshape(1, N_MOD, d)
        nw = norm_w[l][:, None, :]
        gnw = gla_norm_w[l][None, :]
        lw = {
            'w_gate': jnp.concatenate([jnp.concatenate([w_af_up[l], zeros], axis=1),
                                       jnp.concatenate([zeros, w_ab_up[l]], axis=1)], axis=0).astype(BF16),
            'b_gate': jnp.concatenate([b_af[l], b_ab[l]])[None, :],
            'pool_scale': pool_scale[l][None, :],
            'w_a': w_in_t[l, OFF_AF:OFF_Q],
        }

        x, cast = _ffn(x, mod_x, nw[0], up1, down1, fw, mod_row=0, final_norm=False,
                       cast=[whole(ffn2_up, l), whole(ffn2_down, l)]
                       + _mixer_cast_jobs(l, w_in_t, w_br_gla, w_br_pool, w_out, pool_w))
        up2, down2 = cast[:2]
        lw.update(zip([n for n, _, _ in _W_IN_PARTS] + list(_MIXER_MATS), cast[2:]))
        ctx, _ = _ffn(ctx, mod_c, nw[0], up1, down1, fw, mod_row=0, final_norm=False)

        ctx_mixed, s_f, s_b = _mixer(ctx, mod_c, nw[1], lw, zero_state, zero_state, gnw,
                                     n_rows=1, n_cols=ctx_len, states_only=last, sequences=batch)
        x, _, _ = _mixer(x, mod_x, nw[1], lw, s_f, s_b, gnw, n_rows=seq // GRID_W, n_cols=GRID_W)
        x, cast = _ffn(x, mod_x, nw[2], up2, down2, fw, mod_row=6, final_norm=last,
                       cast=[] if last else [whole(ffn1_up, l + 1), whole(ffn1_down, l + 1)])
        if not last:
            ctx, _ = _ffn(ctx_mixed, mod_c, nw[2], up2, down2, fw, mod_row=6, final_norm=False)
            up1, down1 = cast
    return x
```

```python
import functools

import jax
import jax.numpy as jnp
from jax import lax
from jax.experimental import pallas as pl
from jax.experimental.pallas import tpu as pltpu

F32 = jnp.float32
BF16 = jnp.bfloat16

D_MODEL = 1024
DEPTH = 2
GRID_W = 64
N_POOL_GROUPS = 4
POOL_GROUP = 128
POOL_WIDTH = N_POOL_GROUPS * POOL_GROUP
POOL_WINDOWS = (2, 4, 8, 16)
GLA_HEADS = 4
GLA_DK = 64
GLA_DV = 128
GLA_KW = GLA_HEADS * GLA_DK
GLA_VW = GLA_HEADS * GLA_DV
GATE_RANK = 16
GATE_TAU = 16.0
CHUNK = 64
D_FF = 2816
N_MOD = 9
EPS = 1e-6

OFF_K = 0
OFF_V = OFF_K + GLA_KW
OFF_AF = OFF_V + GLA_VW
OFF_AB = OFF_AF + GATE_RANK
OFF_Q = OFF_AB + GATE_RANK
OFF_R = OFF_Q + GLA_KW
OFF_POOL = OFF_R + GLA_VW
OFF_GP = OFF_POOL + POOL_WIDTH
OFF_GG = OFF_GP + D_MODEL
IN_COLS = OFF_GG + D_MODEL

V7X_VMEM_BYTES = 64 * 1024 * 1024
VMEM_LIMIT_BYTES = V7X_VMEM_BYTES * 7 // 8
SUBLANES = 8
TOKEN_TILE = 1024
FFN_TOKEN_TILE = 1024
SCAN_TOKEN_TILE = 1024
MERGE_TOKEN_TILE = 1024
FF_CHUNK = 256
NORM_PART = 256
PROJ_NORM_PART = 512
PROJ_ROW_PIECE = 128
PROJ_COL_PIECE = 256
MOD_COL_TILE = 2304


def _resident(shape):
    return pl.BlockSpec(shape, lambda *_: (0,) * len(shape), pipeline_mode=pl.Buffered(1))


def _params(*semantics):
    return pltpu.CompilerParams(dimension_semantics=semantics, vmem_limit_bytes=VMEM_LIMIT_BYTES)


def _per_batch_or_shared(arr):
    zeros = (0,) * (arr.ndim - 1)
    if arr.shape[0] == 1:
        return lambda b, i: (0,) + zeros
    return lambda b, i: (b,) + zeros


def _token_tile(t, tile=TOKEN_TILE):
    return min(tile, t)


def _rms(x):
    return x * lax.rsqrt(jnp.mean(x * x, axis=-1, keepdims=True) + EPS)


def _modulated_norm(x, w, shift, scale):
    return (_rms(x) * w) * (1.0 + scale) + shift


def _silu(a):
    return a / (1.0 + jnp.exp(-a))


def _sigmoid(a):
    return 1.0 / (1.0 + jnp.exp(-a))


def _dot(a, b):
    return jnp.dot(a, b, preferred_element_type=F32)


def _dot_nt(a, b):
    return lax.dot_general(a, b, (((1,), (1,)), ((), ())), preferred_element_type=F32)


def _head_blocks(x):
    head = lax.broadcasted_iota(jnp.int32, x.shape, 1) // GLA_DK
    zero = jnp.zeros((), x.dtype)
    return jnp.concatenate([jnp.where(head == h, x, zero) for h in range(GLA_HEADS)], axis=0)


def _chunk_cumsum(g, *, reverse):
    n = g.shape[0]
    pos = lax.broadcasted_iota(jnp.int32, g.shape, 0) % CHUNK
    acc = g
    span = 1
    while span < CHUNK:
        if reverse:
            acc = acc + jnp.where(pos + span < CHUNK, pltpu.roll(acc, n - span, axis=0), 0.0)
        else:
            acc = acc + jnp.where(pos >= span, pltpu.roll(acc, span, axis=0), 0.0)
        span *= 2
    return acc


def _mod_kernel(cond_ref, w_ref, b_ref, o_ref):
    cond = cond_ref[...]
    o_ref[0] = _dot(_silu(cond).astype(BF16), w_ref[0].astype(BF16)) + b_ref[0]


def _modulation(cond, w_mod, b_mod):
    n_cols = N_MOD * D_MODEL
    rows = cond.shape[0]
    return pl.pallas_call(
        _mod_kernel,
        grid=(DEPTH, n_cols // MOD_COL_TILE),
        in_specs=[
            pl.BlockSpec((rows, D_MODEL), lambda l, j: (0, 0)),
            pl.BlockSpec((1, D_MODEL, MOD_COL_TILE), lambda l, j: (l, 0, j)),
            pl.BlockSpec((1, 1, MOD_COL_TILE), lambda l, j: (l, 0, j)),
        ],
        out_specs=pl.BlockSpec((1, rows, MOD_COL_TILE), lambda l, j: (l, 0, j)),
        out_shape=jax.ShapeDtypeStruct((DEPTH, rows, n_cols), F32),
        compiler_params=_params("arbitrary", "arbitrary"),
        name="modulation",
    )(cond, w_mod, b_mod.reshape(DEPTH, 1, n_cols))


def _ffn_kernel(*refs, mod_row, final_norm, n_cast):
    x_ref, mod_ref, nw_ref, up_ref, down_ref, fw_ref = refs[:6]
    cast_in = refs[6:6 + n_cast]
    o_ref = refs[6 + n_cast]
    cast_out = refs[7 + n_cast:7 + 2 * n_cast]
    act_ref = refs[7 + 2 * n_cast]
    x = x_ref[0]
    mod = mod_ref[0]
    shift = mod[mod_row:mod_row + 1]
    scale = mod[mod_row + 1:mod_row + 2]
    gate = mod[mod_row + 2:mod_row + 3]
    parts = [slice(r, r + NORM_PART) for r in range(0, x.shape[0], min(NORM_PART, x.shape[0]))]
    h_parts, a_parts, b_parts = [], [], []
    for p in parts:
        h_parts.append(_modulated_norm(x[p], nw_ref[...], shift, scale).astype(BF16))
        a_parts.append(_dot(h_parts[-1], up_ref[:, :FF_CHUNK]))
        b_parts.append(_dot(h_parts[-1], up_ref[:, D_FF:D_FF + FF_CHUNK]))
    h = jnp.concatenate(h_parts, axis=0)
    n_chunks = D_FF // FF_CHUNK
    for j in range(n_chunks):
        lo = j * FF_CHUNK
        if j == 0:
            a = jnp.concatenate(a_parts, axis=0)
            b = jnp.concatenate(b_parts, axis=0)
        else:
            a = _dot(h, up_ref[:, lo:lo + FF_CHUNK])
            b = _dot(h, up_ref[:, D_FF + lo:D_FF + lo + FF_CHUNK])
        act_ref[:, lo:lo + FF_CHUNK] = (_silu(a) * b).astype(BF16)
        for k in range(j, n_cast, n_chunks):
            cast_out[k][...] = cast_in[k][0].astype(BF16)
    y = x + (0.5 * gate) * _dot(act_ref[...], down_ref[...])
    if final_norm:
        y = _rms(y) * fw_ref[...]
    o_ref[0] = y


def _ffn(x, mod, nw, up, down, fw, *, mod_row, final_norm, cast=()):
    b, t, d = x.shape
    tm = _token_tile(t, FFN_TOKEN_TILE)
    nt = t // tm
    steps = b * nt
    cast_specs, out_specs, out_shapes = [], [], []
    for w, l, start, n_rows in cast:
        rows = n_rows // steps
        assert rows * steps == n_rows and rows % (2 * SUBLANES) == 0 and start % rows == 0, (w.shape, start)
        cast_specs.append(pl.BlockSpec((1, rows, w.shape[2]),
                                       lambda b, i, l=l, first=start // rows: (l, first + b * nt + i, 0)))
        out_specs.append(pl.BlockSpec((rows, w.shape[2]), lambda b, i: (b * nt + i, 0)))
        out_shapes.append(jax.ShapeDtypeStruct((n_rows, w.shape[2]), BF16))
    kern = functools.partial(_ffn_kernel, mod_row=mod_row, final_norm=final_norm, n_cast=len(cast))
    res = pl.pallas_call(
        kern,
        grid=(b, nt),
        in_specs=[
            pl.BlockSpec((1, tm, d), lambda b, i: (b, i, 0)),
            pl.BlockSpec((1, N_MOD, d), _per_batch_or_shared(mod)),
            _resident((1, d)),
            _resident(up.shape),
            _resident(down.shape),
            _resident((1, d)),
        ] + cast_specs,
        out_specs=[pl.BlockSpec((1, tm, d), lambda b, i: (b, i, 0))] + out_specs,
        out_shape=[jax.ShapeDtypeStruct(x.shape, F32)] + out_shapes,
        scratch_shapes=[pltpu.VMEM((tm, D_FF), BF16)],
        compiler_params=_params("parallel", "parallel"),
        name="ffn",
    )(x, mod, nw, up, down, fw, *[job[0] for job in cast])
    return res[0], list(res[1:])


_PROJ_WEIGHTS = ('w_k', 'w_v', 'w_a', 'w_q', 'w_r', 'w_u', 'w_gp0', 'w_gp1', 'w_gg0', 'w_gg1', 'w_gate')


def _proj_kernel(x_ref, mod_ref, nw_ref, wk_ref, wv_ref, wa_ref, wq_ref, wr_ref, wu_ref, wgp0_ref, wgp1_ref,
                 wgg0_ref, wgg1_ref, wgate_ref, bgate_ref,
                 qf_ref, kef_ref, decf_ref, qb_ref, keb_ref, decb_ref, vst_ref, oi_ref,
                 r_ref, u_ref, gp_ref, gg_ref):
    x = x_ref[0]
    mod = mod_ref[0]
    tm = x.shape[0]
    chunks = [slice(c * CHUNK, (c + 1) * CHUNK) for c in range(tm // CHUNK)]
    h_parts, z_parts, k_parts, q_parts, v_parts = [], [], [], [], []
    wa = wa_ref[...].astype(BF16)
    for r in range(0, tm, min(PROJ_NORM_PART, tm)):
        hp = _modulated_norm(x[r:r + PROJ_NORM_PART], nw_ref[...], mod[3:4], mod[4:5]).astype(BF16)
        h_parts.append(hp)
        z_parts.append(_dot(_dot_nt(hp, wa).astype(BF16), wgate_ref[...]) + bgate_ref[...])
        k_parts.append(_dot_nt(hp, wk_ref[...]))
        q_parts.append(_dot_nt(hp, wq_ref[...]))
        v_parts.append(_dot_nt(hp, wv_ref[...]))
    h = jnp.concatenate(h_parts, axis=0)
    z = jnp.concatenate(z_parts, axis=0)
    k_all = jnp.concatenate(k_parts, axis=0)
    q_all = jnp.concatenate(q_parts, axis=0) * (GLA_DK ** -0.5)
    v_all = jnp.concatenate(v_parts, axis=0)
    todo = [(name, ref, lo) for name, ref in (('u', wu_ref), ('gp', wgp0_ref), ('gp', wgp1_ref), ('gg', wgg0_ref),
                                              ('gg', wgg1_ref), ('r', wr_ref))
            for lo in range(0, ref.shape[0], PROJ_COL_PIECE)]
    raw = {'u': [], 'gp': [], 'gg': [], 'r': []}
    q_f, q_b, k_f, k_b = [], [], [], []

    def decay_piece(r0):
        zp = z[r0:r0 + PROJ_ROW_PIECE]
        g = (jnp.minimum(zp, 0.0) - jnp.log(1.0 + jnp.exp(-jnp.abs(zp)))) / GATE_TAU
        b_f = _chunk_cumsum(g[:, :GLA_KW], reverse=False)
        b_b = _chunk_cumsum(g[:, GLA_KW:], reverse=True)
        for c in range(r0 // CHUNK, (r0 + PROJ_ROW_PIECE) // CHUNK):
            rows = chunks[c]
            local = slice(c * CHUNK - r0, (c + 1) * CHUNK - r0)
            q, k, bf, bb = q_all[rows], k_all[rows], b_f[local], b_b[local]
            edge_f = bf[CHUNK - 1:CHUNK]
            edge_b = bb[0:1]
            q_f.append((q * jnp.exp(bf)).astype(BF16))
            q_b.append((q * jnp.exp(bb)).astype(BF16))
            k_f.append((k * jnp.exp(-bf)).astype(BF16))
            k_b.append((k * jnp.exp(-bb)).astype(BF16))
            qf_ref[0, rows, :] = q_f[c]
            qb_ref[0, rows, :] = q_b[c]
            kef_ref[0, rows, :] = (k * jnp.exp(edge_f - bf)).astype(BF16)
            keb_ref[0, rows, :] = (k * jnp.exp(edge_b - bb)).astype(BF16)
            decf_ref[0, c:c + 1, :] = jnp.exp(edge_f)
            decb_ref[0, c:c + 1, :] = jnp.exp(edge_b)

    row_pieces = list(range(0, tm, min(PROJ_ROW_PIECE, tm)))
    for i in range(max(len(todo), len(row_pieces))):
        if i < len(todo):
            name, ref, lo = todo[i]
            raw[name].append(_dot_nt(h, ref[lo:lo + PROJ_COL_PIECE, :]))
        if i < len(row_pieces):
            decay_piece(row_pieces[i])
    u_ref[0] = jnp.concatenate(raw['u'], axis=1)
    gp_raw = jnp.concatenate(raw['gp'], axis=1)
    gg_raw = jnp.concatenate(raw['gg'], axis=1)
    r_raw = jnp.concatenate(raw['r'], axis=1)

    gp_ref[0] = _sigmoid(gp_raw).astype(BF16)
    arow = lax.broadcasted_iota(jnp.int32, (CHUNK, GLA_KW), 0)
    acol = lax.broadcasted_iota(jnp.int32, (CHUNK, GLA_KW), 1) % CHUNK
    causal = acol <= arow
    att = [jnp.where(causal, _dot_nt(q_f[c], _head_blocks(k_f[c])), _dot_nt(q_b[c], _head_blocks(k_b[c])))
           .astype(BF16) for c in range(len(chunks))]
    gg_ref[0] = _sigmoid(gg_raw).astype(BF16)

    v_head = lax.broadcasted_iota(jnp.int32, (CHUNK, GLA_VW), 1) // GLA_DV
    zero = jnp.zeros((), BF16)
    for c, rows in enumerate(chunks):
        v_bf = v_all[rows].astype(BF16)
        v_blk = jnp.concatenate([jnp.where(v_head == hd, v_bf, zero) for hd in range(GLA_HEADS)], axis=0)
        oi_ref[0, rows, :] = _dot(att[c], v_blk)
    r_ref[0] = _silu(r_raw).astype(BF16)

    for c, rows in enumerate(chunks):
        v = v_all[rows]
        v_stack = jnp.concatenate([v[:, hd * GLA_DV:(hd + 1) * GLA_DV] for hd in range(GLA_HEADS)], axis=0)
        vst_ref[0, c] = v_stack.T.astype(BF16)


def _proj(x, mod, nw, lw):
    b, t, d = x.shape
    tm = _token_tile(t)
    nc, tc = t // CHUNK, tm // CHUNK
    tok = lambda w: pl.BlockSpec((1, tm, w), lambda b, i: (b, i, 0))
    dec = pl.BlockSpec((1, tc, GLA_KW), lambda b, i: (b, i, 0))
    outs = [
        ('qf', (b, t, GLA_KW), BF16, tok(GLA_KW)), ('kef', (b, t, GLA_KW), BF16, tok(GLA_KW)),
        ('decf', (b, nc, GLA_KW), F32, dec),
        ('qb', (b, t, GLA_KW), BF16, tok(GLA_KW)), ('keb', (b, t, GLA_KW), BF16, tok(GLA_KW)),
        ('decb', (b, nc, GLA_KW), F32, dec),
        ('vst', (b, nc, GLA_DV, GLA_KW), BF16, pl.BlockSpec((1, tc, GLA_DV, GLA_KW), lambda b, i: (b, i, 0, 0))),
        ('oi', (b, t, GLA_VW), F32, tok(GLA_VW)),
        ('r', (b, t, GLA_VW), BF16, tok(GLA_VW)), ('u', (b, t, POOL_WIDTH), F32, tok(POOL_WIDTH)),
        ('gp', (b, t, d), BF16, tok(d)), ('gg', (b, t, d), BF16, tok(d)),
    ]
    weights = [lw[n] for n in _PROJ_WEIGHTS]
    res = pl.pallas_call(
        _proj_kernel,
        grid=(b, t // tm),
        in_specs=[tok(d), pl.BlockSpec((1, N_MOD, d), _per_batch_or_shared(mod)), _resident((1, d))]
        + [_resident(w.shape) for w in weights] + [_resident(lw['b_gate'].shape)],
        out_specs=[o[3] for o in outs],
        out_shape=[jax.ShapeDtypeStruct(o[1], o[2]) for o in outs],
        compiler_params=_params("parallel", "parallel"),
        name="mixer_proj",
    )(x, mod, nw, *weights, lw['b_gate'])
    return {o[0]: a for o, a in zip(outs, res)}


def _gla_kernel(qf_ref, kef_ref, decf_ref, qb_ref, keb_ref, decb_ref, vst_ref, oi_ref, r_ref,
                sf0_ref, sb0_ref, nw_ref, og_ref, sf_ref, sb_ref, st_ref, ob_ref, *, n_blocks, n_chunks):
    step = pl.program_id(1)
    tb = n_chunks * CHUNK

    def chunk(q_ref, ke_ref, dec_ref, c):
        rows = pl.ds(c * CHUNK, CHUNK)
        st = st_ref[...]
        st_ref[...] = st * dec_ref[0, c:c + 1, :] + _dot(vst_ref[0, c], _head_blocks(ke_ref[0, rows, :]))
        o_stack = _dot_nt(_head_blocks(q_ref[0, rows, :]), st.astype(BF16))
        return jnp.concatenate([o_stack[h * CHUNK:(h + 1) * CHUNK] for h in range(GLA_HEADS)], axis=1)

    @pl.when(step == 0)
    def _():
        st_ref[...] = sb0_ref[0]

    @pl.when(step < n_blocks)
    def _():
        base = (n_blocks - 1 - step) * tb
        for c in reversed(range(n_chunks)):
            o = chunk(qb_ref, keb_ref, decb_ref, c)
            ob_ref[pl.ds(pl.multiple_of(base + c * CHUNK, CHUNK), CHUNK), :] = o

    @pl.when(step == n_blocks - 1)
    def _():
        sb_ref[0] = st_ref[...]

    @pl.when(step == n_blocks)
    def _():
        st_ref[...] = sf0_ref[0]

    @pl.when(step >= n_blocks)
    def _():
        base = (step - n_blocks) * tb
        nw = nw_ref[...]
        for c in range(n_chunks):
            rows = pl.ds(c * CHUNK, CHUNK)
            o = chunk(qf_ref, kef_ref, decf_ref, c)
            o = o + ob_ref[pl.ds(pl.multiple_of(base + c * CHUNK, CHUNK), CHUNK), :] + oi_ref[0, rows, :]
            o = jnp.concatenate(
                [_rms(o[:, h * GLA_DV:(h + 1) * GLA_DV]) * nw for h in range(GLA_HEADS)], axis=1)
            og_ref[0, rows, :] = (o * r_ref[0, rows, :].astype(F32)).astype(og_ref.dtype)

    @pl.when(step == 2 * n_blocks - 1)
    def _():
        sf_ref[0] = st_ref[...]


def _gla(p, r, s_f0, s_b0, norm_w):
    b, t, _ = p['qf'].shape
    tb = _token_tile(t, SCAN_TOKEN_TILE)
    nb = t // tb
    tc = tb // CHUNK
    fwd = lambda b, s: (b, jnp.maximum(s - nb, 0), 0)
    rev = lambda b, s: (b, nb - 1 - jnp.minimum(s, nb - 1), 0)
    both = lambda b, s: (b, jnp.where(s < nb, nb - 1 - s, s - nb), 0, 0)
    tok = lambda w, m: pl.BlockSpec((1, tb, w), m)
    dec = lambda m: pl.BlockSpec((1, tc, GLA_KW), m)
    state = pl.BlockSpec((1, GLA_DV, GLA_KW), lambda b, s: (b, 0, 0))
    kern = functools.partial(_gla_kernel, n_blocks=nb, n_chunks=tc)
    return pl.pallas_call(
        kern,
        grid=(b, 2 * nb),
        in_specs=[tok(GLA_KW, fwd), tok(GLA_KW, fwd), dec(fwd), tok(GLA_KW, rev), tok(GLA_KW, rev), dec(rev),
                  pl.BlockSpec((1, tc, GLA_DV, GLA_KW), both), tok(GLA_VW, fwd), tok(GLA_VW, fwd),
                  state, state, _resident((1, GLA_DV))],
        out_specs=[tok(GLA_VW, fwd), state, state],
        out_shape=[jax.ShapeDtypeStruct((b, t, GLA_VW), BF16),
                   jax.ShapeDtypeStruct((b, GLA_DV, GLA_KW), F32),
                   jax.ShapeDtypeStruct((b, GLA_DV, GLA_KW), F32)],
        scratch_shapes=[pltpu.VMEM((GLA_DV, GLA_KW), F32), pltpu.VMEM((t, GLA_VW), F32)],
        compiler_params=_params("arbitrary", "arbitrary"),
        name="gla_scan",
    )(p['qf'], p['kef'], p['decf'], p['qb'], p['keb'], p['decb'], p['vst'], p['oi'], r, s_f0, s_b0, norm_w)


def _col_box_sum(x, n_cols, w):
    half = w // 2
    c_out = lax.broadcasted_iota(jnp.int32, (n_cols, 2 * n_cols), 0)
    c_in = lax.broadcasted_iota(jnp.int32, (n_cols, 2 * n_cols), 1) % n_cols
    band = jnp.where((c_in >= c_out - half) & (c_in < c_out + half), 1.0, 0.0).astype(BF16)
    hi = x.astype(BF16)
    lo = (x - hi.astype(F32)).astype(BF16)
    out = []
    for r in range(x.shape[0] // n_cols):
        rows = slice(r * n_cols, (r + 1) * n_cols)
        out.append(_dot(band, jnp.concatenate([hi[rows], lo[rows]], axis=0)))
    return jnp.concatenate(out, axis=0)


def _row_box_sum(cur, before_ref, after_ref, lanes, tile, n_tiles, n_cols, w):
    half = w // 2
    halo = half * n_cols
    tm = cur.shape[0]
    assert halo <= before_ref.shape[1] and tm % n_cols == 0
    prev = before_ref[0, before_ref.shape[1] - halo:, lanes]
    nxt = after_ref[0, :halo, lanes]
    win = jnp.concatenate([jnp.where(tile > 0, prev, 0.0), cur, jnp.where(tile < n_tiles - 1, nxt, 0.0)],
                          axis=0)
    span = 1
    while span < w:
        win = win[:-span * n_cols] + win[span * n_cols:]
        span *= 2
    return win[:tm]


def _merge_kernel(x_ref, mod_ref, og_ref, u_ref, ubefore_ref, uafter_ref, gp_ref, gg_ref, wbg_ref, wbp_ref, wo_ref,
                  pw_ref, ps_ref, o_ref, *, n_rows, n_cols, n_tiles):
    tile = pl.program_id(1)
    tm = x_ref.shape[1]
    tok = lax.broadcasted_iota(jnp.int32, (tm, POOL_GROUP), 0)
    col = (tok % n_cols).astype(F32)
    row = (tile * (tm // n_cols) + tok // n_cols).astype(F32)
    mixed = []
    for g, w in enumerate(POOL_WINDOWS):
        half = float(w // 2)
        lanes = slice(g * POOL_GROUP, (g + 1) * POOL_GROUP)
        u = u_ref[0, :, lanes]
        cnt = jnp.minimum(col, half) + jnp.minimum(n_cols - col, half)
        m = u
        if n_rows > 1:
            cnt = cnt * (jnp.minimum(row, half) + jnp.minimum(n_rows - row, half))
            m = _row_box_sum(u, ubefore_ref, uafter_ref, lanes, tile, n_tiles, n_cols, w)
        m = _col_box_sum(m, n_cols, w) / cnt
        mixed.append((_dot((m - u).astype(BF16), pw_ref[lanes, :]) * ps_ref[:, lanes]).astype(BF16))
    y_pool = _dot(jnp.concatenate(mixed, axis=1), wbp_ref[...])
    y_gla = _dot(og_ref[0], wbg_ref[...])
    merged = gp_ref[0].astype(F32) * y_pool + gg_ref[0].astype(F32) * y_gla
    gate = mod_ref[0][5:6]
    o_ref[0] = x_ref[0] + gate * _dot(merged.astype(BF16), wo_ref[...])


def _merge(x, mod, og, u, gp, gg, lw, *, n_rows, n_cols):
    b, t, d = x.shape
    tm = _token_tile(t, MERGE_TOKEN_TILE)
    tok = lambda w: pl.BlockSpec((1, tm, w), lambda b, i: (b, i, 0))
    halo = min(tm, max(POOL_WINDOWS) // 2 * n_cols)
    per_tile, last = tm // halo, t // halo - 1
    before = pl.BlockSpec((1, halo, POOL_WIDTH), lambda b, i: (b, jnp.maximum(i * per_tile - 1, 0), 0))
    after = pl.BlockSpec((1, halo, POOL_WIDTH), lambda b, i: (b, jnp.minimum((i + 1) * per_tile, last), 0))
    weights = [lw['w_br_gla'], lw['w_br_pool'], lw['w_out'], lw['pool_w']]
    kern = functools.partial(_merge_kernel, n_rows=n_rows, n_cols=n_cols, n_tiles=t // tm)
    return pl.pallas_call(
        kern,
        grid=(b, t // tm),
        in_specs=[tok(d), pl.BlockSpec((1, N_MOD, d), _per_batch_or_shared(mod)),
                  tok(GLA_VW), tok(POOL_WIDTH), before, after, tok(d), tok(d)]
        + [_resident(w.shape) for w in weights] + [_resident(lw['pool_scale'].shape)],
        out_specs=tok(d),
        out_shape=jax.ShapeDtypeStruct(x.shape, F32),
        compiler_params=_params("parallel", "parallel"),
        name="mixer_merge",
    )(x, mod, og, u, u, u, gp, gg, *weights, lw['pool_scale'])


_HALF_D = D_MODEL // 2
_W_IN_PARTS = (('w_k', OFF_K, GLA_KW), ('w_v', OFF_V, GLA_VW), ('w_q', OFF_Q, GLA_KW), ('w_r', OFF_R, GLA_VW),
               ('w_u', OFF_POOL, POOL_WIDTH), ('w_gp0', OFF_GP, _HALF_D), ('w_gp1', OFF_GP + _HALF_D, _HALF_D),
               ('w_gg0', OFF_GG, _HALF_D), ('w_gg1', OFF_GG + _HALF_D, _HALF_D))
_MIXER_MATS = ('w_br_gla', 'w_br_pool', 'w_out', 'pool_w')


def _mixer_cast_jobs(l, w_in_t, w_br_gla, w_br_pool, w_out, pool_w):
    whole = lambda w: (w, l, 0, w.shape[1])
    return ([(w_in_t, l, start, n) for _, start, n in _W_IN_PARTS]
            + [whole(w_br_gla), whole(w_br_pool), whole(w_out), whole(pool_w.reshape(DEPTH, POOL_WIDTH, POOL_GROUP))])


def _mixer(x, mod, nw, lw, s_f0, s_b0, gla_norm_w, *, n_rows, n_cols, states_only=False, sequences=None):
    p = _proj(x, mod, nw, lw)
    if sequences is not None:
        p = {k: v.reshape((sequences, v.shape[1] // sequences) + v.shape[2:]) for k, v in p.items()}
    og, s_f, s_b = _gla(p, p['r'], s_f0, s_b0, gla_norm_w)
    if sequences is not None:
        og = og.reshape((1, -1) + og.shape[2:])
        p = {k: v.reshape((1, -1) + v.shape[2:]) for k, v in p.items()}
    if states_only:
        return None, s_f, s_b
    return _merge(x, mod, og, p['u'], p['gp'], p['gg'], lw, n_rows=n_rows, n_cols=n_cols), s_f, s_b


def kernel(x, c, ctx, c_ctx, w_mod, b_mod, norm_w, ffn1_up, ffn1_down, w_in, w_af_up, b_af, w_ab_up, b_ab,
           gla_norm_w, pool_w, pool_scale, w_br_pool, w_br_gla, w_out, ffn2_up, ffn2_down, final_norm_w):
    batch, seq, d = x.shape
    ctx_len = ctx.shape[1]
    ctx = ctx.reshape(1, batch * ctx_len, d)
    cond = jnp.concatenate([c, c_ctx[None, :], jnp.zeros((SUBLANES - batch - 1, d), F32)], axis=0)
    mod_all = _modulation(cond, w_mod, b_mod)
    zero_state = jnp.zeros((batch, GLA_DV, GLA_KW), F32)
    fw = final_norm_w[None, :]
    zeros = jnp.zeros((GATE_RANK, GLA_KW), F32)
    up1, down1 = ffn1_up[0].astype(BF16), ffn1_down[0].astype(BF16)
    whole = lambda w, l: (w, l, 0, w.shape[1])
    w_in_t = jnp.swapaxes(w_in, 1, 2)
    for l in range(DEPTH):
        last = l == DEPTH - 1
        mod_x = mod_all[l, :batch].reshape(batch, N_MOD, d)
        mod_c = mod_all[l, batch:batch + 1].reshape(1, N_MOD, d)
        nw = norm_w[l][:, None, :]
        gnw = gla_norm_w[l][None, :]
        lw = {
            'w_gate': jnp.concatenate([jnp.concatenate([w_af_up[l], zeros], axis=1),
                                       jnp.concatenate([zeros, w_ab_up[l]], axis=1)], axis=0).astype(BF16),
            'b_gate': jnp.concatenate([b_af[l], b_ab[l]])[None, :],
            'pool_scale': pool_scale[l][None, :],
            'w_a': w_in_t[l, OFF_AF:OFF_Q],
        }

        x, cast = _ffn(x, mod_x, nw[0], up1, down1, fw, mod_row=0, final_norm=False,
                       cast=[whole(ffn2_up, l), whole(ffn2_down, l)]
                       + _mixer_cast_jobs(l, w_in_t, w_br_gla, w_br_pool, w_out, pool_w))
        up2, down2 = cast[:2]
        lw.update(zip([n for n, _, _ in _W_IN_PARTS] + list(_MIXER_MATS), cast[2:]))
        ctx, _ = _ffn(ctx, mod_c, nw[0], up1, down1, fw, mod_row=0, final_norm=False)

        ctx_mixed, s_f, s_b = _mixer(ctx, mod_c, nw[1], lw, zero_state, zero_state, gnw,
                                     n_rows=1, n_cols=ctx_len, states_only=last, sequences=batch)
        x, _, _ = _mixer(x, mod_x, nw[1], lw, s_f, s_b, gnw, n_rows=seq // GRID_W, n_cols=GRID_W)
        x, cast = _ffn(x, mod_x, nw[2], up2, down2, fw, mod_row=6, final_norm=last,
                       cast=[] if last else [whole(ffn1_up, l + 1), whole(ffn1_down, l + 1)])
        if not last:
            ctx, _ = _ffn(ctx_mixed, mod_c, nw[2], up2, down2, fw, mod_row=6, final_norm=False)
            up1, down1 = cast
    return x
```

```python
import functools

import jax
import jax.numpy as jnp
from jax import lax
from jax.experimental import pallas as pl
from jax.experimental.pallas import tpu as pltpu

F32 = jnp.float32
BF16 = jnp.bfloat16

D_MODEL = 1024
DEPTH = 2
GRID_W = 64
N_POOL_GROUPS = 4
POOL_GROUP = 128
POOL_WIDTH = N_POOL_GROUPS * POOL_GROUP
POOL_WINDOWS = (2, 4, 8, 16)
GLA_HEADS = 4
GLA_DK = 64
GLA_DV = 128
GLA_KW = GLA_HEADS * GLA_DK
GLA_VW = GLA_HEADS * GLA_DV
GATE_RANK = 16
GATE_TAU = 16.0
CHUNK = 64
D_FF = 2816
N_MOD = 9
EPS = 1e-6

OFF_K = 0
OFF_V = OFF_K + GLA_KW
OFF_AF = OFF_V + GLA_VW
OFF_AB = OFF_AF + GATE_RANK
OFF_Q = OFF_AB + GATE_RANK
OFF_R = OFF_Q + GLA_KW
OFF_POOL = OFF_R + GLA_VW
OFF_GP = OFF_POOL + POOL_WIDTH
OFF_GG = OFF_GP + D_MODEL
IN_COLS = OFF_GG + D_MODEL

V7X_VMEM_BYTES = 64 * 1024 * 1024
VMEM_LIMIT_BYTES = V7X_VMEM_BYTES * 7 // 8
SUBLANES = 8
TOKEN_TILE = 1024
FFN_TOKEN_TILE = 1024
SCAN_TOKEN_TILE = 2048
MERGE_TOKEN_TILE = 1024
FF_CHUNK = 256
NORM_PART = 256
PROJ_NORM_PART = 512
PROJ_ROW_PIECE = 128
PROJ_COL_PIECE = 256
MOD_COL_TILE = 2304


def _resident(shape):
    return pl.BlockSpec(shape, lambda *_: (0,) * len(shape), pipeline_mode=pl.Buffered(1))


def _params(*semantics):
    return pltpu.CompilerParams(dimension_semantics=semantics, vmem_limit_bytes=VMEM_LIMIT_BYTES)


def _per_batch_or_shared(arr):
    zeros = (0,) * (arr.ndim - 1)
    if arr.shape[0] == 1:
        return lambda b, i: (0,) + zeros
    return lambda b, i: (b,) + zeros


def _token_tile(t, tile=TOKEN_TILE):
    return min(tile, t)


def _rms(x):
    return x * lax.rsqrt(jnp.mean(x * x, axis=-1, keepdims=True) + EPS)


def _modulated_norm(x, w, shift, scale):
    return (_rms(x) * w) * (1.0 + scale) + shift


def _silu(a):
    return a / (1.0 + jnp.exp(-a))


def _sigmoid(a):
    return 1.0 / (1.0 + jnp.exp(-a))


def _dot(a, b):
    return jnp.dot(a, b, preferred_element_type=F32)


def _dot_nt(a, b):
    return lax.dot_general(a, b, (((1,), (1,)), ((), ())), preferred_element_type=F32)


def _head_blocks(x):
    head = lax.broadcasted_iota(jnp.int32, x.shape, 1) // GLA_DK
    zero = jnp.zeros((), x.dtype)
    return jnp.concatenate([jnp.where(head == h, x, zero) for h in range(GLA_HEADS)], axis=0)


def _chunk_cumsum(g, *, reverse):
    n = g.shape[0]
    pos = lax.broadcasted_iota(jnp.int32, g.shape, 0) % CHUNK
    acc = g
    span = 1
    while span < CHUNK:
        if reverse:
            acc = acc + jnp.where(pos + span < CHUNK, pltpu.roll(acc, n - span, axis=0), 0.0)
        else:
            acc = acc + jnp.where(pos >= span, pltpu.roll(acc, span, axis=0), 0.0)
        span *= 2
    return acc


def _mod_kernel(cond_ref, w_ref, b_ref, o_ref):
    cond = cond_ref[...]
    o_ref[0] = _dot(_silu(cond).astype(BF16), w_ref[0].astype(BF16)) + b_ref[0]


def _modulation(cond, w_mod, b_mod):
    n_cols = N_MOD * D_MODEL
    rows = cond.shape[0]
    return pl.pallas_call(
        _mod_kernel,
        grid=(DEPTH, n_cols // MOD_COL_TILE),
        in_specs=[
            pl.BlockSpec((rows, D_MODEL), lambda l, j: (0, 0)),
            pl.BlockSpec((1, D_MODEL, MOD_COL_TILE), lambda l, j: (l, 0, j)),
            pl.BlockSpec((1, 1, MOD_COL_TILE), lambda l, j: (l, 0, j)),
        ],
        out_specs=pl.BlockSpec((1, rows, MOD_COL_TILE), lambda l, j: (l, 0, j)),
        out_shape=jax.ShapeDtypeStruct((DEPTH, rows, n_cols), F32),
        compiler_params=_params("arbitrary", "arbitrary"),
        name="modulation",
    )(cond, w_mod, b_mod.reshape(DEPTH, 1, n_cols))


def _ffn_kernel(*refs, mod_row, final_norm, n_cast):
    x_ref, mod_ref, nw_ref, up_ref, down_ref, fw_ref = refs[:6]
    cast_in = refs[6:6 + n_cast]
    o_ref = refs[6 + n_cast]
    cast_out = refs[7 + n_cast:7 + 2 * n_cast]
    act_ref = refs[7 + 2 * n_cast]
    x = x_ref[0]
    mod = mod_ref[0]
    shift = mod[mod_row:mod_row + 1]
    scale = mod[mod_row + 1:mod_row + 2]
    gate = mod[mod_row + 2:mod_row + 3]
    parts = [slice(r, r + NORM_PART) for r in range(0, x.shape[0], min(NORM_PART, x.shape[0]))]
    h_parts, a_parts, b_parts = [], [], []
    for p in parts:
        h_parts.append(_modulated_norm(x[p], nw_ref[...], shift, scale).astype(BF16))
        a_parts.append(_dot(h_parts[-1], up_ref[:, :FF_CHUNK]))
        b_parts.append(_dot(h_parts[-1], up_ref[:, D_FF:D_FF + FF_CHUNK]))
    h = jnp.concatenate(h_parts, axis=0)
    n_chunks = D_FF // FF_CHUNK
    for j in range(n_chunks):
        lo = j * FF_CHUNK
        if j == 0:
            a = jnp.concatenate(a_parts, axis=0)
            b = jnp.concatenate(b_parts, axis=0)
        else:
            a = _dot(h, up_ref[:, lo:lo + FF_CHUNK])
            b = _dot(h, up_ref[:, D_FF + lo:D_FF + lo + FF_CHUNK])
        act_ref[:, lo:lo + FF_CHUNK] = (_silu(a) * b).astype(BF16)
        for k in range(j, n_cast, n_chunks):
            cast_out[k][...] = cast_in[k][0].astype(BF16)
    y = x + (0.5 * gate) * _dot(act_ref[...], down_ref[...])
    if final_norm:
        y = _rms(y) * fw_ref[...]
    o_ref[0] = y


def _ffn(x, mod, nw, up, down, fw, *, mod_row, final_norm, cast=()):
    b, t, d = x.shape
    tm = _token_tile(t, FFN_TOKEN_TILE)
    nt = t // tm
    steps = b * nt
    cast_specs, out_specs, out_shapes = [], [], []
    for w, l, start, n_rows in cast:
        rows = n_rows // steps
        assert rows * steps == n_rows and rows % (2 * SUBLANES) == 0 and start % rows == 0, (w.shape, start)
        cast_specs.append(pl.BlockSpec((1, rows, w.shape[2]),
                                       lambda b, i, l=l, first=start // rows: (l, first + b * nt + i, 0)))
        out_specs.append(pl.BlockSpec((rows, w.shape[2]), lambda b, i: (b * nt + i, 0)))
        out_shapes.append(jax.ShapeDtypeStruct((n_rows, w.shape[2]), BF16))
    kern = functools.partial(_ffn_kernel, mod_row=mod_row, final_norm=final_norm, n_cast=len(cast))
    res = pl.pallas_call(
        kern,
        grid=(b, nt),
        in_specs=[
            pl.BlockSpec((1, tm, d), lambda b, i: (b, i, 0)),
            pl.BlockSpec((1, N_MOD, d), _per_batch_or_shared(mod)),
            _resident((1, d)),
            _resident(up.shape),
            _resident(down.shape),
            _resident((1, d)),
        ] + cast_specs,
        out_specs=[pl.BlockSpec((1, tm, d), lambda b, i: (b, i, 0))] + out_specs,
        out_shape=[jax.ShapeDtypeStruct(x.shape, F32)] + out_shapes,
        scratch_shapes=[pltpu.VMEM((tm, D_FF), BF16)],
        compiler_params=_params("parallel", "parallel"),
        name="ffn",
    )(x, mod, nw, up, down, fw, *[job[0] for job in cast])
    return res[0], list(res[1:])


_PROJ_WEIGHTS = ('w_k', 'w_v', 'w_a', 'w_q', 'w_r', 'w_u', 'w_gp0', 'w_gp1', 'w_gg0', 'w_gg1', 'w_gate')


def _proj_kernel(x_ref, mod_ref, nw_ref, wk_ref, wv_ref, wa_ref, wq_ref, wr_ref, wu_ref, wgp0_ref, wgp1_ref,
                 wgg0_ref, wgg1_ref, wgate_ref, bgate_ref,
                 qf_ref, kef_ref, decf_ref, qb_ref, keb_ref, decb_ref, vst_ref, oi_ref,
                 r_ref, u_ref, gp_ref, gg_ref):
    x = x_ref[0]
    mod = mod_ref[0]
    tm = x.shape[0]
    chunks = [slice(c * CHUNK, (c + 1) * CHUNK) for c in range(tm // CHUNK)]
    h_parts, z_parts, k_parts, q_parts, v_parts = [], [], [], [], []
    wa = wa_ref[...].astype(BF16)
    for r in range(0, tm, min(PROJ_NORM_PART, tm)):
        hp = _modulated_norm(x[r:r + PROJ_NORM_PART], nw_ref[...], mod[3:4], mod[4:5]).astype(BF16)
        h_parts.append(hp)
        z_parts.append(_dot(_dot_nt(hp, wa).astype(BF16), wgate_ref[...]) + bgate_ref[...])
        k_parts.append(_dot_nt(hp, wk_ref[...]))
        q_parts.append(_dot_nt(hp, wq_ref[...]))
        v_parts.append(_dot_nt(hp, wv_ref[...]))
    h = jnp.concatenate(h_parts, axis=0)
    z = jnp.concatenate(z_parts, axis=0)
    k_all = jnp.concatenate(k_parts, axis=0)
    q_all = jnp.concatenate(q_parts, axis=0) * (GLA_DK ** -0.5)
    v_all = jnp.concatenate(v_parts, axis=0)
    todo = [(name, ref, lo) for name, ref in (('u', wu_ref), ('gp', wgp0_ref), ('gp', wgp1_ref), ('gg', wgg0_ref),
                                              ('gg', wgg1_ref), ('r', wr_ref))
            for lo in range(0, ref.shape[0], PROJ_COL_PIECE)]
    raw = {'u': [], 'gp': [], 'gg': [], 'r': []}
    q_f, q_b, k_f, k_b = [], [], [], []

    def decay_piece(r0):
        zp = z[r0:r0 + PROJ_ROW_PIECE]
        g = (jnp.minimum(zp, 0.0) - jnp.log(1.0 + jnp.exp(-jnp.abs(zp)))) / GATE_TAU
        b_f = _chunk_cumsum(g[:, :GLA_KW], reverse=False)
        b_b = _chunk_cumsum(g[:, GLA_KW:], reverse=True)
        for c in range(r0 // CHUNK, (r0 + PROJ_ROW_PIECE) // CHUNK):
            rows = chunks[c]
            local = slice(c * CHUNK - r0, (c + 1) * CHUNK - r0)
            q, k, bf, bb = q_all[rows], k_all[rows], b_f[local], b_b[local]
            edge_f = bf[CHUNK - 1:CHUNK]
            edge_b = bb[0:1]
            q_f.append((q * jnp.exp(bf)).astype(BF16))
            q_b.append((q * jnp.exp(bb)).astype(BF16))
            k_f.append((k * jnp.exp(-bf)).astype(BF16))
            k_b.append((k * jnp.exp(-bb)).astype(BF16))
            qf_ref[0, rows, :] = q_f[c]
            qb_ref[0, rows, :] = q_b[c]
            kef_ref[0, rows, :] = (k * jnp.exp(edge_f - bf)).astype(BF16)
            keb_ref[0, rows, :] = (k * jnp.exp(edge_b - bb)).astype(BF16)
            decf_ref[0, c:c + 1, :] = jnp.exp(edge_f)
            decb_ref[0, c:c + 1, :] = jnp.exp(edge_b)

    row_pieces = list(range(0, tm, min(PROJ_ROW_PIECE, tm)))
    for i in range(max(len(todo), len(row_pieces))):
        if i < len(todo):
            name, ref, lo = todo[i]
            raw[name].append(_dot_nt(h, ref[lo:lo + PROJ_COL_PIECE, :]))
        if i < len(row_pieces):
            decay_piece(row_pieces[i])
    u_ref[0] = jnp.concatenate(raw['u'], axis=1)
    gp_raw = jnp.concatenate(raw['gp'], axis=1)
    gg_raw = jnp.concatenate(raw['gg'], axis=1)
    r_raw = jnp.concatenate(raw['r'], axis=1)

    gp_ref[0] = _sigmoid(gp_raw).astype(BF16)
    arow = lax.broadcasted_iota(jnp.int32, (CHUNK, GLA_KW), 0)
    acol = lax.broadcasted_iota(jnp.int32, (CHUNK, GLA_KW), 1) % CHUNK
    causal = acol <= arow
    att = [jnp.where(causal, _dot_nt(q_f[c], _head_blocks(k_f[c])), _dot_nt(q_b[c], _head_blocks(k_b[c])))
           .astype(BF16) for c in range(len(chunks))]
    gg_ref[0] = _sigmoid(gg_raw).astype(BF16)

    v_head = lax.broadcasted_iota(jnp.int32, (CHUNK, GLA_VW), 1) // GLA_DV
    zero = jnp.zeros((), BF16)
    for c, rows in enumerate(chunks):
        v_bf = v_all[rows].astype(BF16)
        v_blk = jnp.concatenate([jnp.where(v_head == hd, v_bf, zero) for hd in range(GLA_HEADS)], axis=0)
        oi_ref[0, rows, :] = _dot(att[c], v_blk)
    r_ref[0] = _silu(r_raw).astype(BF16)

    for c, rows in enumerate(chunks):
        v = v_all[rows]
        v_stack = jnp.concatenate([v[:, hd * GLA_DV:(hd + 1) * GLA_DV] for hd in range(GLA_HEADS)], axis=0)
        vst_ref[0, c] = v_stack.T.astype(BF16)


def _proj(x, mod, nw, lw):
    b, t, d = x.shape
    tm = _token_tile(t)
    nc, tc = t // CHUNK, tm // CHUNK
    tok = lambda w: pl.BlockSpec((1, tm, w), lambda b, i: (b, i, 0))
    dec = pl.BlockSpec((1, tc, GLA_KW), lambda b, i: (b, i, 0))
    outs = [
        ('qf', (b, t, GLA_KW), BF16, tok(GLA_KW)), ('kef', (b, t, GLA_KW), BF16, tok(GLA_KW)),
        ('decf', (b, nc, GLA_KW), F32, dec),
        ('qb', (b, t, GLA_KW), BF16, tok(GLA_KW)), ('keb', (b, t, GLA_KW), BF16, tok(GLA_KW)),
        ('decb', (b, nc, GLA_KW), F32, dec),
        ('vst', (b, nc, GLA_DV, GLA_KW), BF16, pl.BlockSpec((1, tc, GLA_DV, GLA_KW), lambda b, i: (b, i, 0, 0))),
        ('oi', (b, t, GLA_VW), F32, tok(GLA_VW)),
        ('r', (b, t, GLA_VW), BF16, tok(GLA_VW)), ('u', (b, t, POOL_WIDTH), F32, tok(POOL_WIDTH)),
        ('gp', (b, t, d), BF16, tok(d)), ('gg', (b, t, d), BF16, tok(d)),
    ]
    weights = [lw[n] for n in _PROJ_WEIGHTS]
    res = pl.pallas_call(
        _proj_kernel,
        grid=(b, t // tm),
        in_specs=[tok(d), pl.BlockSpec((1, N_MOD, d), _per_batch_or_shared(mod)), _resident((1, d))]
        + [_resident(w.shape) for w in weights] + [_resident(lw['b_gate'].shape)],
        out_specs=[o[3] for o in outs],
        out_shape=[jax.ShapeDtypeStruct(o[1], o[2]) for o in outs],
        compiler_params=_params("parallel", "parallel"),
        name="mixer_proj",
    )(x, mod, nw, *weights, lw['b_gate'])
    return {o[0]: a for o, a in zip(outs, res)}


def _gla_kernel(qf_ref, kef_ref, decf_ref, qb_ref, keb_ref, decb_ref, vst_ref, oi_ref, r_ref,
                sf0_ref, sb0_ref, nw_ref, og_ref, sf_ref, sb_ref, st_ref, ob_ref, *, n_blocks, n_chunks):
    step = pl.program_id(1)
    tb = n_chunks * CHUNK

    def chunk(q_ref, ke_ref, dec_ref, c):
        rows = pl.ds(c * CHUNK, CHUNK)
        st = st_ref[...]
        st_ref[...] = st * dec_ref[0, c:c + 1, :] + _dot(vst_ref[0, c], _head_blocks(ke_ref[0, rows, :]))
        o_stack = _dot_nt(_head_blocks(q_ref[0, rows, :]), st.astype(BF16))
        return jnp.concatenate([o_stack[h * CHUNK:(h + 1) * CHUNK] for h in range(GLA_HEADS)], axis=1)

    @pl.when(step == 0)
    def _():
        st_ref[...] = sb0_ref[0]

    @pl.when(step < n_blocks)
    def _():
        base = (n_blocks - 1 - step) * tb
        for c in reversed(range(n_chunks)):
            o = chunk(qb_ref, keb_ref, decb_ref, c)
            ob_ref[pl.ds(pl.multiple_of(base + c * CHUNK, CHUNK), CHUNK), :] = o

    @pl.when(step == n_blocks - 1)
    def _():
        sb_ref[0] = st_ref[...]

    @pl.when(step == n_blocks)
    def _():
        st_ref[...] = sf0_ref[0]

    @pl.when(step >= n_blocks)
    def _():
        base = (step - n_blocks) * tb
        nw = nw_ref[...]
        for c in range(n_chunks):
            rows = pl.ds(c * CHUNK, CHUNK)
            o = chunk(qf_ref, kef_ref, decf_ref, c)
            o = o + ob_ref[pl.ds(pl.multiple_of(base + c * CHUNK, CHUNK), CHUNK), :] + oi_ref[0, rows, :]
            o = jnp.concatenate(
                [_rms(o[:, h * GLA_DV:(h + 1) * GLA_DV]) * nw for h in range(GLA_HEADS)], axis=1)
            og_ref[0, rows, :] = (o * r_ref[0, rows, :].astype(F32)).astype(og_ref.dtype)

    @pl.when(step == 2 * n_blocks - 1)
    def _():
        sf_ref[0] = st_ref[...]


def _gla(p, r, s_f0, s_b0, norm_w):
    b, t, _ = p['qf'].shape
    tb = _token_tile(t, SCAN_TOKEN_TILE)
    nb = t // tb
    tc = tb // CHUNK
    fwd = lambda b, s: (b, jnp.maximum(s - nb, 0), 0)
    rev = lambda b, s: (b, nb - 1 - jnp.minimum(s, nb - 1), 0)
    both = lambda b, s: (b, jnp.where(s < nb, nb - 1 - s, s - nb), 0, 0)
    tok = lambda w, m: pl.BlockSpec((1, tb, w), m)
    dec = lambda m: pl.BlockSpec((1, tc, GLA_KW), m)
    state = pl.BlockSpec((1, GLA_DV, GLA_KW), lambda b, s: (b, 0, 0))
    kern = functools.partial(_gla_kernel, n_blocks=nb, n_chunks=tc)
    return pl.pallas_call(
        kern,
        grid=(b, 2 * nb),
        in_specs=[tok(GLA_KW, fwd), tok(GLA_KW, fwd), dec(fwd), tok(GLA_KW, rev), tok(GLA_KW, rev), dec(rev),
                  pl.BlockSpec((1, tc, GLA_DV, GLA_KW), both), tok(GLA_VW, fwd), tok(GLA_VW, fwd),
                  state, state, _resident((1, GLA_DV))],
        out_specs=[tok(GLA_VW, fwd), state, state],
        out_shape=[jax.ShapeDtypeStruct((b, t, GLA_VW), BF16),
                   jax.ShapeDtypeStruct((b, GLA_DV, GLA_KW), F32),
                   jax.ShapeDtypeStruct((b, GLA_DV, GLA_KW), F32)],
        scratch_shapes=[pltpu.VMEM((GLA_DV, GLA_KW), F32), pltpu.VMEM((t, GLA_VW), F32)],
        compiler_params=_params("arbitrary", "arbitrary"),
        name="gla_scan",
    )(p['qf'], p['kef'], p['decf'], p['qb'], p['keb'], p['decb'], p['vst'], p['oi'], r, s_f0, s_b0, norm_w)


def _col_box_sum(x, n_cols, w):
    half = w // 2
    c_out = lax.broadcasted_iota(jnp.int32, (n_cols, 2 * n_cols), 0)
    c_in = lax.broadcasted_iota(jnp.int32, (n_cols, 2 * n_cols), 1) % n_cols
    band = jnp.where((c_in >= c_out - half) & (c_in < c_out + half), 1.0, 0.0).astype(BF16)
    hi = x.astype(BF16)
    lo = (x - hi.astype(F32)).astype(BF16)
    out = []
    for r in range(x.shape[0] // n_cols):
        rows = slice(r * n_cols, (r + 1) * n_cols)
        out.append(_dot(band, jnp.concatenate([hi[rows], lo[rows]], axis=0)))
    return jnp.concatenate(out, axis=0)


def _row_box_sum(cur, before_ref, after_ref, lanes, tile, n_tiles, n_cols, w):
    half = w // 2
    halo = half * n_cols
    tm = cur.shape[0]
    assert halo <= before_ref.shape[1] and tm % n_cols == 0
    prev = before_ref[0, before_ref.shape[1] - halo:, lanes]
    nxt = after_ref[0, :halo, lanes]
    win = jnp.concatenate([jnp.where(tile > 0, prev, 0.0), cur, jnp.where(tile < n_tiles - 1, nxt, 0.0)],
                          axis=0)
    span = 1
    while span < w:
        win = win[:-span * n_cols] + win[span * n_cols:]
        span *= 2
    return win[:tm]


def _merge_kernel(x_ref, mod_ref, og_ref, u_ref, ubefore_ref, uafter_ref, gp_ref, gg_ref, wbg_ref, wbp_ref, wo_ref,
                  pw_ref, ps_ref, o_ref, *, n_rows, n_cols, n_tiles):
    tile = pl.program_id(1)
    tm = x_ref.shape[1]
    tok = lax.broadcasted_iota(jnp.int32, (tm, POOL_GROUP), 0)
    col = (tok % n_cols).astype(F32)
    row = (tile * (tm // n_cols) + tok // n_cols).astype(F32)
    mixed = []
    for g, w in enumerate(POOL_WINDOWS):
        half = float(w // 2)
        lanes = slice(g * POOL_GROUP, (g + 1) * POOL_GROUP)
        u = u_ref[0, :, lanes]
        cnt = jnp.minimum(col, half) + jnp.minimum(n_cols - col, half)
        m = u
        if n_rows > 1:
            cnt = cnt * (jnp.minimum(row, half) + jnp.minimum(n_rows - row, half))
            m = _row_box_sum(u, ubefore_ref, uafter_ref, lanes, tile, n_tiles, n_cols, w)
        m = _col_box_sum(m, n_cols, w) / cnt
        mixed.append((_dot((m - u).astype(BF16), pw_ref[lanes, :]) * ps_ref[:, lanes]).astype(BF16))
    y_pool = _dot(jnp.concatenate(mixed, axis=1), wbp_ref[...])
    y_gla = _dot(og_ref[0], wbg_ref[...])
    merged = gp_ref[0].astype(F32) * y_pool + gg_ref[0].astype(F32) * y_gla
    gate = mod_ref[0][5:6]
    o_ref[0] = x_ref[0] + gate * _dot(merged.astype(BF16), wo_ref[...])


def _merge(x, mod, og, u, gp, gg, lw, *, n_rows, n_cols):
    b, t, d = x.shape
    tm = _token_tile(t, MERGE_TOKEN_TILE)
    tok = lambda w: pl.BlockSpec((1, tm, w), lambda b, i: (b, i, 0))
    halo = min(tm, max(POOL_WINDOWS) // 2 * n_cols)
    per_tile, last = tm // halo, t // halo - 1
    before = pl.BlockSpec((1, halo, POOL_WIDTH), lambda b, i: (b, jnp.maximum(i * per_tile - 1, 0), 0))
    after = pl.BlockSpec((1, halo, POOL_WIDTH), lambda b, i: (b, jnp.minimum((i + 1) * per_tile, last), 0))
    weights = [lw['w_br_gla'], lw['w_br_pool'], lw['w_out'], lw['pool_w']]
    kern = functools.partial(_merge_kernel, n_rows=n_rows, n_cols=n_cols, n_tiles=t // tm)
    return pl.pallas_call(
        kern,
        grid=(b, t // tm),
        in_specs=[tok(d), pl.BlockSpec((1, N_MOD, d), _per_batch_or_shared(mod)),
                  tok(GLA_VW), tok(POOL_WIDTH), before, after, tok(d), tok(d)]
        + [_resident(w.shape) for w in weights] + [_resident(lw['pool_scale'].shape)],
        out_specs=tok(d),
        out_shape=jax.ShapeDtypeStruct(x.shape, F32),
        compiler_params=_params("parallel", "parallel"),
        name="mixer_merge",
    )(x, mod, og, u, u, u, gp, gg, *weights, lw['pool_scale'])


_HALF_D = D_MODEL // 2
_W_IN_PARTS = (('w_k', OFF_K, GLA_KW), ('w_v', OFF_V, GLA_VW), ('w_q', OFF_Q, GLA_KW), ('w_r', OFF_R, GLA_VW),
               ('w_u', OFF_POOL, POOL_WIDTH), ('w_gp0', OFF_GP, _HALF_D), ('w_gp1', OFF_GP + _HALF_D, _HALF_D),
               ('w_gg0', OFF_GG, _HALF_D), ('w_gg1', OFF_GG + _HALF_D, _HALF_D))
_MIXER_MATS = ('w_br_gla', 'w_br_pool', 'w_out', 'pool_w')


def _mixer_cast_jobs(l, w_in_t, w_br_gla, w_br_pool, w_out, pool_w):
    whole = lambda w: (w, l, 0, w.shape[1])
    return ([(w_in_t, l, start, n) for _, start, n in _W_IN_PARTS]
            + [whole(w_br_gla), whole(w_br_pool), whole(w_out), whole(pool_w.reshape(DEPTH, POOL_WIDTH, POOL_GROUP))])


def _mixer(x, mod, nw, lw, s_f0, s_b0, gla_norm_w, *, n_rows, n_cols, states_only=False, sequences=None):
    p = _proj(x, mod, nw, lw)
    if sequences is not None:
        p = {k: v.reshape((sequences, v.shape[1] // sequences) + v.shape[2:]) for k, v in p.items()}
    og, s_f, s_b = _gla(p, p['r'], s_f0, s_b0, gla_norm_w)
    if sequences is not None:
        og = og.reshape((1, -1) + og.shape[2:])
        p = {k: v.reshape((1, -1) + v.shape[2:]) for k, v in p.items()}
    if states_only:
        return None, s_f, s_b
    return _merge(x, mod, og, p['u'], p['gp'], p['gg'], lw, n_rows=n_rows, n_cols=n_cols), s_f, s_b


def kernel(x, c, ctx, c_ctx, w_mod, b_mod, norm_w, ffn1_up, ffn1_down, w_in, w_af_up, b_af, w_ab_up, b_ab,
           gla_norm_w, pool_w, pool_scale, w_br_pool, w_br_gla, w_out, ffn2_up, ffn2_down, final_norm_w):
    batch, seq, d = x.shape
    ctx_len = ctx.shape[1]
    ctx = ctx.reshape(1, batch * ctx_len, d)
    cond = jnp.concatenate([c, c_ctx[None, :], jnp.zeros((SUBLANES - batch - 1, d), F32)], axis=0)
    mod_all = _modulation(cond, w_mod, b_mod)
    zero_state = jnp.zeros((batch, GLA_DV, GLA_KW), F32)
    fw = final_norm_w[None, :]
    zeros = jnp.zeros((GATE_RANK, GLA_KW), F32)
    up1, down1 = ffn1_up[0].astype(BF16), ffn1_down[0].astype(BF16)
    whole = lambda w, l: (w, l, 0, w.shape[1])
    w_in_t = jnp.swapaxes(w_in, 1, 2)
    for l in range(DEPTH):
        last = l == DEPTH - 1
        mod_x = mod_all[l, :batch].reshape(batch, N_MOD, d)
        mod_c = mod_all[l, batch:batch + 1].reshape(1, N_MOD, d)
        nw = norm_w[l][:, None, :]
        gnw = gla_norm_w[l][None, :]
        lw = {
            'w_gate': jnp.concatenate([jnp.concatenate([w_af_up[l], zeros], axis=1),
                                       jnp.concatenate([zeros, w_ab_up[l]], axis=1)], axis=0).astype(BF16),
            'b_gate': jnp.concatenate([b_af[l], b_ab[l]])[None, :],
            'pool_scale': pool_scale[l][None, :],
            'w_a': w_in_t[l, OFF_AF:OFF_Q],
        }

        x, cast = _ffn(x, mod_x, nw[0], up1, down1, fw, mod_row=0, final_norm=False,
                       cast=[whole(ffn2_up, l), whole(ffn2_down, l)]
                       + _mixer_cast_jobs(l, w_in_t, w_br_gla, w_br_pool, w_out, pool_w))
        up2, down2 = cast[:2]
        lw.update(zip([n for n, _, _ in _W_IN_PARTS] + list(_MIXER_MATS), cast[2:]))
        ctx, _ = _ffn(ctx, mod_c, nw[0], up1, down1, fw, mod_row=0, final_norm=False)

        ctx_mixed, s_f, s_b = _mixer(ctx, mod_c, nw[1], lw, zero_state, zero_state, gnw,
                                     n_rows=1, n_cols=ctx_len, states_only=last, sequences=batch)
        x, _, _ = _mixer(x, mod_x, nw[1], lw, s_f, s_b, gnw, n_rows=seq // GRID_W, n_cols=GRID_W)
        x, cast = _ffn(x, mod_x, nw[2], up2, down2, fw, mod_row=6, final_norm=last,
                       cast=[] if last else [whole(ffn1_up, l + 1), whole(ffn1_down, l + 1)])
        if not last:
            ctx, _ = _ffn(ctx_mixed, mod_c, nw[2], up2, down2, fw, mod_row=6, final_norm=False)
            up1, down1 = cast
    return x
```
